```python
import math
import jax, jax.numpy as jnp
from jax import lax
import numpy as np

D_MODEL = 1024
BATCH = 4
SEQ = 4096
DEPTH = 4
DEC_BATCH = 32
DEC_SEQ = 4
PAST_LEN = 8192
PAGE_SIZE = 128

CONV_DIM = D_MODEL // 2
CONV_K = 31
RWKV_HEAD_DIM = 64
RWKV_HEADS = D_MODEL // 128
RWKV_DIM = RWKV_HEADS * RWKV_HEAD_DIM
RWKV_DECAY_LORA = 64
RWKV_AAA_LORA = 64
RWKV_GATE_LORA = 128
RWKV_U = 3 * RWKV_DIM + RWKV_DECAY_LORA + RWKV_AAA_LORA + RWKV_GATE_LORA
RWKV_LN_EPS = 64e-5
HEAD_DIM = 64
ATT_HG = 4
DILATIONS = ((128, 1), (512, 4), (2048, 16))
N_DIL = len(DILATIONS)
ATT_HEADS = N_DIL * ATT_HG
ATT_QKV = 3 * ATT_HEADS * HEAD_DIM
N_BUCKETS = 32
MAX_DISTANCE = 2048
N_BRANCH = 3
IN_COLS = 2 * CONV_DIM + RWKV_U + ATT_QKV + N_BRANCH * D_MODEL
N_GROUPS = 4
EXPERTS_PER_GROUP = 8
N_EXPERTS = N_GROUPS * EXPERTS_PER_GROUP
TOP_K = 2
D_EXPERT = D_MODEL // 2
MOE_BLOCK = 128
ALPHA = (2 * DEPTH) ** 0.25
BETA = (8 * DEPTH) ** -0.25
LN_EPS = 1e-5
NEG_INF = -1e30

kernel_name = 'hybrid_conv_rwkv7_dilattn_hmoe_step'


def layer_norm(x, g, b, eps=LN_EPS):
    xf = x.astype(jnp.float32)
    mu = xf.mean(-1, keepdims=True)
    var = ((xf - mu) ** 2).mean(-1, keepdims=True)
    return ((xf - mu) * lax.rsqrt(var + eps) * g + b).astype(x.dtype)


def rel_bucket(dist):
    max_exact = N_BUCKETS // 2
    large = max_exact + (jnp.log(jnp.maximum(dist, 1).astype(jnp.float32) / max_exact)
                         / math.log(MAX_DISTANCE / max_exact) * (N_BUCKETS - max_exact)).astype(jnp.int32)
    return jnp.where(dist < max_exact, dist, jnp.minimum(large, N_BUCKETS - 1))


def conformer_conv(u_conv, buf, dw, dw_b, ln_g, ln_b, proj):
    z = u_conv[..., :CONV_DIM] * jax.nn.sigmoid(u_conv[..., CONV_DIM:])
    zc = jnp.concatenate([buf.astype(z.dtype), z], axis=1)
    y = lax.conv_general_dilated(zc, dw[:, None, :], window_strides=(1,), padding='VALID',
                                 dimension_numbers=('NWC', 'WIO', 'NWC'),
                                 feature_group_count=CONV_DIM) + dw_b
    y = jax.nn.silu(layer_norm(y, ln_g, ln_b))
    return y @ proj, zc[:, -(CONV_K - 1):]


def rwkv7_time_mix(u, shift, wkv, p):
    B, T, _ = u.shape
    C, H, N = RWKV_DIM, RWKV_HEADS, RWKV_HEAD_DIM
    u_prev = jnp.concatenate([shift[:, None, :].astype(u.dtype), u[:, :-1]], axis=1)
    um = u + (u_prev - u) * p['rwkv_mu']
    r, k, v, w_lo, a_lo, g_lo = jnp.split(
        um, [C, 2 * C, 3 * C, 3 * C + RWKV_DECAY_LORA, 3 * C + RWKV_DECAY_LORA + RWKV_AAA_LORA], axis=-1)
    w = -jax.nn.softplus(-(p['rwkv_w0'] + jnp.tanh(w_lo) @ p['rwkv_w2'])) - 0.5
    decay = jnp.exp(-jnp.exp(w.astype(jnp.float32)))
    a = jax.nn.sigmoid(p['rwkv_a0'] + a_lo @ p['rwkv_a2'])
    gate = jax.nn.sigmoid(g_lo) @ p['rwkv_g2']

    def heads(t):
        return t.reshape(B, T, H, N).astype(jnp.float32)

    kk = heads(k * p['rwkv_kk'])
    kk = kk / jnp.maximum(jnp.linalg.norm(kk, axis=-1, keepdims=True), 1e-12)
    k = k * (1.0 + (a - 1.0) * p['rwkv_ka'])
    r_h, k_h, v_h, a_h, w_h = heads(r), heads(k), heads(v), heads(a), heads(decay)

    def step(S, inp):
        r_t, w_t, k_t, v_t, kk_t, a_t = inp
        sa = jnp.einsum('bhij,bhj->bhi', S, -kk_t)
        S = (S * w_t[:, :, None, :] + sa[..., None] * (kk_t * a_t)[:, :, None, :]
             + v_t[..., None] * k_t[:, :, None, :])
        return S, jnp.einsum('bhij,bhj->bhi', S, r_t)

    xs = tuple(jnp.moveaxis(t, 1, 0) for t in (r_h, w_h, k_h, v_h, kk, a_h))
    S_fin, ys = lax.scan(step, wkv.astype(jnp.float32), xs)
    y = jnp.moveaxis(ys, 0, 1)
    mu = y.mean(-1, keepdims=True)
    var = ((y - mu) ** 2).mean(-1, keepdims=True)
    y = ((y - mu) * lax.rsqrt(var + RWKV_LN_EPS)).reshape(B, T, C) * p['rwkv_ln_g'] + p['rwkv_ln_b']
    bonus = (r_h * k_h * p['rwkv_rk']).sum(-1, keepdims=True) * v_h
    y = (y + bonus.reshape(B, T, C)) * gate
    return y.astype(u.dtype) @ p['rwkv_proj'], u[:, -1], S_fin.astype(wkv.dtype)


def dilated_attn_prompt(q, k, v, dil, span, bias_tab):
    B, S, H, dh = q.shape
    M = S // dil
    nb = -(-M // span)
    Mp = nb * span

    def to_blocks(a):
        a = a.reshape(B, M, dil, H, dh).transpose(0, 2, 1, 3, 4).reshape(B * dil, M, H, dh)
        a = jnp.pad(a, ((0, 0), (0, Mp - M), (0, 0), (0, 0)))
        return a.reshape(B * dil, nb, span, H, dh)

    qb, kb, vb = to_blocks(q), to_blocks(k), to_blocks(v)
    k2 = jnp.concatenate([jnp.concatenate([jnp.zeros_like(kb[:, :1]), kb[:, :-1]], axis=1), kb], axis=2)
    v2 = jnp.concatenate([jnp.concatenate([jnp.zeros_like(vb[:, :1]), vb[:, :-1]], axis=1), vb], axis=2)
    qi = jnp.arange(span)[:, None]
    ki = jnp.arange(2 * span)[None, :]
    j = span + qi - ki
    band = (j >= 0) & (j <= span)
    bias = jnp.transpose(bias_tab[rel_bucket(dil * jnp.clip(j, 0, span))], (2, 0, 1))
    has_prev = (jnp.arange(nb) > 0)[:, None, None, None] | (ki >= span)[None, None]
    valid = band[None, None] & has_prev
    s = jnp.einsum('bnqhd,bnkhd->bnhqk', qb, k2, preferred_element_type=jnp.float32) * HEAD_DIM ** -0.5 + bias
    s = jnp.where(valid, s, NEG_INF)
    m = s.max(-1, keepdims=True)
    e = jnp.exp(s - m)
    den = e.sum(-1, keepdims=True)
    o = jnp.einsum('bnhqk,bnkhd->bnqhd', (e / den).astype(v.dtype), v2)
    lse = jnp.transpose((m + jnp.log(den))[..., 0], (0, 1, 3, 2))

    def from_blocks(a):
        a = a.reshape((B * dil, Mp) + a.shape[3:])[:, :M]
        a = a.reshape((B, dil, M) + a.shape[2:])
        return jnp.swapaxes(a, 1, 2).reshape((B, S) + a.shape[3:])

    return from_blocks(o), from_blocks(lse)


def dilated_attn_sample(q, k_cat, v_cat, dil, span, past_rows, bias_tab):
    B, T, H, dh = q.shape
    j = jnp.arange(span + 1)
    idx = past_rows + jnp.arange(T)[:, None] - dil * j[None, :]
    valid = idx >= 0
    idx = jnp.maximum(idx, 0)
    kg = k_cat[:, idx]
    vg = v_cat[:, idx]
    bias = jnp.transpose(bias_tab[rel_bucket(dil * j)])
    s = jnp.einsum('bthd,btjhd->bthj', q, kg, preferred_element_type=jnp.float32) * HEAD_DIM ** -0.5 + bias
    s = jnp.where(valid[None, :, None, :], s, NEG_INF)
    m = s.max(-1, keepdims=True)
    e = jnp.exp(s - m)
    den = e.sum(-1, keepdims=True)
    o = jnp.einsum('bthj,btjhd->bthd', (e / den).astype(v_cat.dtype), vg)
    return o, (m + jnp.log(den))[..., 0]


def dilated_mixture(u_att, attn_bufs, rel_bias, attn_proj):
    B, T, _ = u_att.shape
    qkv = u_att.reshape(B, T, 3, N_DIL, ATT_HG, HEAD_DIM)
    outs, lses, bufs = [], [], []
    for g, (window, dil) in enumerate(DILATIONS):
        span = window // dil
        q, k, v = qkv[:, :, 0, g], qkv[:, :, 1, g], qkv[:, :, 2, g]
        tab = rel_bias[:, g * ATT_HG:(g + 1) * ATT_HG]
        if attn_bufs is None:
            o, lse = dilated_attn_prompt(q, k, v, dil, span, tab)
            rows = min(window, T)
            bufs.append(jnp.stack([k[:, T - rows:], v[:, T - rows:]], axis=1))
        else:
            buf = attn_bufs[g]
            past_rows = buf.shape[2]
            k_cat = jnp.concatenate([buf[:, 0], k], axis=1)
            v_cat = jnp.concatenate([buf[:, 1], v], axis=1)
            o, lse = dilated_attn_sample(q, k_cat, v_cat, dil, span, past_rows, tab)
            bufs.append(jnp.stack([k_cat[:, T:], v_cat[:, T:]], axis=1))
        outs.append(o)
        lses.append(lse)
    wts = jax.nn.softmax(jnp.stack(lses, 0), axis=0)
    o = jnp.einsum('gbthd,gbth->bthd', jnp.stack(outs, 0), wts.astype(outs[0].dtype))
    return o.reshape(B, T, ATT_HG * HEAD_DIM) @ attn_proj, bufs


def hier_moe(h, router_group, router_expert, w_gate, w_up, w_down):
    B, T, D = h.shape
    n = B * T
    x = h.reshape(n, D)
    lg = (x @ router_group).astype(jnp.float32)
    grp = jnp.argmax(lg, axis=-1)
    p_grp = jnp.take_along_axis(jax.nn.softmax(lg, axis=-1), grp[:, None], axis=-1)
    le = (x @ router_expert).astype(jnp.float32).reshape(n, N_GROUPS, EXPERTS_PER_GROUP)
    le = jnp.take_along_axis(le, grp[:, None, None], axis=1)[:, 0]
    top_v, top_i = lax.top_k(le, TOP_K)
    gate = p_grp * jax.nn.softmax(top_v, axis=-1)
    eid = (grp[:, None] * EXPERTS_PER_GROUP + top_i).reshape(-1).astype(jnp.int32)
    tok = jnp.repeat(jnp.arange(n, dtype=jnp.int32), TOP_K)
    gw = gate.reshape(-1)
    A = n * TOP_K
    n_blocks = -(-A // MOE_BLOCK) + N_EXPERTS
    order = jnp.argsort(eid)
    e_sorted = eid[order]
    counts = jnp.bincount(eid, length=N_EXPERTS)
    padded = (counts + MOE_BLOCK - 1) // MOE_BLOCK * MOE_BLOCK
    start = jnp.cumsum(counts) - counts
    pend = jnp.cumsum(padded)
    pstart = pend - padded
    dest = pstart[e_sorted] + jnp.arange(A, dtype=jnp.int32) - start[e_sorted]
    slot_tok = jnp.full((n_blocks * MOE_BLOCK,), n, jnp.int32).at[dest].set(tok[order])
    slot_w = jnp.zeros((n_blocks * MOE_BLOCK,), jnp.float32).at[dest].set(gw[order])
    block_e = jnp.minimum(jnp.searchsorted(pend, jnp.arange(n_blocks, dtype=pend.dtype) * MOE_BLOCK,
                                           side='right'), N_EXPERTS - 1)
    x_pad = jnp.concatenate([x, jnp.zeros((1, D), x.dtype)], axis=0)
    xb = x_pad[slot_tok].reshape(n_blocks, MOE_BLOCK, D)

    def expert_block(args):
        xblk, e = args
        return (jax.nn.silu(xblk @ w_gate[e]) * (xblk @ w_up[e])) @ w_down[e]

    yb = lax.map(expert_block, (xb, block_e)).reshape(-1, D)
    y = jnp.zeros((n + 1, D), jnp.float32).at[slot_tok].add(yb.astype(jnp.float32) * slot_w[:, None])[:n]
    return y.astype(h.dtype).reshape(B, T, D)


def trunk_layer(x, p, conv_buf, shift, wkv, attn_bufs):
    B, T, _ = x.shape
    u = x @ p['w_in']
    o1 = 2 * CONV_DIM
    o2 = o1 + RWKV_U
    o3 = o2 + ATT_QKV
    u_conv, u_rwkv, u_att, u_gate = u[..., :o1], u[..., o1:o2], u[..., o2:o3], u[..., o3:]
    y_conv, conv_new = conformer_conv(u_conv, conv_buf, p['conv_dw'], p['conv_dw_b'],
                                      p['conv_ln_g'], p['conv_ln_b'], p['conv_proj'])
    y_rwkv, shift_new, wkv_new = rwkv7_time_mix(u_rwkv, shift, wkv, p)
    y_att, att_new = dilated_mixture(u_att, attn_bufs, p['rel_bias'], p['attn_proj'])
    gates = jax.nn.sigmoid(u_gate + p['b_gate']).reshape(B, T, N_BRANCH, D_MODEL)
    merged = gates[:, :, 0] * y_conv + gates[:, :, 1] * y_rwkv + gates[:, :, 2] * y_att
    h = layer_norm(ALPHA * x + merged @ p['w_out'], p['ln1_g'], p['ln1_b'])
    f = hier_moe(h, p['router_group'], p['router_expert'], p['moe_w_gate'], p['moe_w_up'], p['moe_w_down'])
    y = layer_norm(ALPHA * h + f, p['ln2_g'], p['ln2_b'])
    return y, conv_new, shift_new, wkv_new, att_new


def setup_inputs(seed: int = 0) -> dict:
    key = jax.random.key(seed)
    ks = iter(jax.random.split(key, 64))

    def nrm(shape, scale):
        return jax.random.normal(next(ks), shape, jnp.float32) * scale

    def unif(shape, lo, hi):
        return jax.random.uniform(next(ks), shape, jnp.float32, minval=lo, maxval=hi)

    L = [min(w, PAST_LEN) for w, _ in DILATIONS]
    return {
        'x_prompt': nrm((BATCH, SEQ, D_MODEL), 1.0),
        'x_sample': nrm((DEC_BATCH, DEC_SEQ, D_MODEL), 1.0),
        'state_conv': nrm((DEPTH, DEC_BATCH, CONV_K - 1, CONV_DIM), 0.5),
        'state_shift': nrm((DEPTH, DEC_BATCH, RWKV_U), 1.0),
        'state_wkv': nrm((DEPTH, DEC_BATCH, RWKV_HEADS, RWKV_HEAD_DIM, RWKV_HEAD_DIM), 0.3),
        'cache_attn_w128': nrm((DEPTH, DEC_BATCH, 2, L[0], ATT_HG, HEAD_DIM), 1.0),
        'cache_attn_w512': nrm((DEPTH, DEC_BATCH, 2, L[1], ATT_HG, HEAD_DIM), 1.0),
        'cache_attn_w2048': nrm((DEPTH, DEC_BATCH, 2, L[2], ATT_HG, HEAD_DIM), 1.0),
        'rel_bias': nrm((N_BUCKETS, ATT_HEADS), 0.5),
        'w_in': nrm((DEPTH, D_MODEL, IN_COLS), D_MODEL ** -0.5),
        'b_gate': nrm((DEPTH, N_BRANCH * D_MODEL), 0.1),
        'conv_dw': nrm((DEPTH, CONV_K, CONV_DIM), CONV_K ** -0.5),
        'conv_dw_b': nrm((DEPTH, CONV_DIM), 0.02),
        'conv_ln_g': 1.0 + nrm((DEPTH, CONV_DIM), 0.02),
        'conv_ln_b': nrm((DEPTH, CONV_DIM), 0.02),
        'conv_proj': nrm((DEPTH, CONV_DIM, D_MODEL), CONV_DIM ** -0.5),
        'rwkv_mu': unif((DEPTH, RWKV_U), 0.0, 1.0),
        'rwkv_w0': unif((DEPTH, RWKV_DIM), -5.0, 1.0),
        'rwkv_w2': nrm((DEPTH, RWKV_DECAY_LORA, RWKV_DIM), 0.5 * RWKV_DECAY_LORA ** -0.5),
        'rwkv_a0': nrm((DEPTH, RWKV_DIM), 0.1),
        'rwkv_a2': nrm((DEPTH, RWKV_AAA_LORA, RWKV_DIM), 0.5 * RWKV_AAA_LORA ** -0.5),
        'rwkv_g2': nrm((DEPTH, RWKV_GATE_LORA, RWKV_DIM), RWKV_GATE_LORA ** -0.5),
        'rwkv_kk': 0.85 + nrm((DEPTH, RWKV_DIM), 0.02),
        'rwkv_ka': 1.0 + nrm((DEPTH, RWKV_DIM), 0.02),
        'rwkv_rk': nrm((DEPTH, RWKV_HEADS, RWKV_HEAD_DIM), 0.1),
        'rwkv_ln_g': 1.0 + nrm((DEPTH, RWKV_DIM), 0.02),
        'rwkv_ln_b': nrm((DEPTH, RWKV_DIM), 0.02),
        'rwkv_proj': nrm((DEPTH, RWKV_DIM, D_MODEL), RWKV_DIM ** -0.5),
        'attn_proj': nrm((DEPTH, ATT_HG * HEAD_DIM, D_MODEL), (ATT_HG * HEAD_DIM) ** -0.5),
        'w_out': nrm((DEPTH, D_MODEL, D_MODEL), BETA * D_MODEL ** -0.5),
        'ln1_g': 1.0 + nrm((DEPTH, D_MODEL), 0.02),
        'ln1_b': nrm((DEPTH, D_MODEL), 0.02),
        'router_group': nrm((DEPTH, D_MODEL, N_GROUPS), D_MODEL ** -0.5),
        'router_expert': nrm((DEPTH, D_MODEL, N_EXPERTS), D_MODEL ** -0.5),
        'moe_w_gate': nrm((DEPTH, N_EXPERTS, D_MODEL, D_EXPERT), D_MODEL ** -0.5),
        'moe_w_up': nrm((DEPTH, N_EXPERTS, D_MODEL, D_EXPERT), D_MODEL ** -0.5),
        'moe_w_down': nrm((DEPTH, N_EXPERTS, D_EXPERT, D_MODEL), BETA * D_EXPERT ** -0.5),
        'ln2_g': 1.0 + nrm((DEPTH, D_MODEL), 0.02),
        'ln2_b': nrm((DEPTH, D_MODEL), 0.02),
    }


def reference(x_prompt, x_sample, state_conv, state_shift, state_wkv,
              cache_attn_w128, cache_attn_w512, cache_attn_w2048,
              rel_bias, w_in, b_gate, conv_dw, conv_dw_b, conv_ln_g, conv_ln_b, conv_proj,
              rwkv_mu, rwkv_w0, rwkv_w2, rwkv_a0, rwkv_a2, rwkv_g2, rwkv_kk, rwkv_ka, rwkv_rk,
              rwkv_ln_g, rwkv_ln_b, rwkv_proj, attn_proj, w_out, ln1_g, ln1_b,
              router_group, router_expert, moe_w_gate, moe_w_up, moe_w_down, ln2_g, ln2_b):
    xp, xs = x_prompt, x_sample
    conv_p, conv_s, shift_p, shift_s, wkv_p, wkv_s = [], [], [], [], [], []
    att_p = [[] for _ in range(N_DIL)]
    att_s = [[] for _ in range(N_DIL)]
    for l in range(DEPTH):
        p = dict(rel_bias=rel_bias, w_in=w_in[l], b_gate=b_gate[l], conv_dw=conv_dw[l], conv_dw_b=conv_dw_b[l],
                 conv_ln_g=conv_ln_g[l], conv_ln_b=conv_ln_b[l], conv_proj=conv_proj[l],
                 rwkv_mu=rwkv_mu[l], rwkv_w0=rwkv_w0[l], rwkv_w2=rwkv_w2[l], rwkv_a0=rwkv_a0[l],
                 rwkv_a2=rwkv_a2[l], rwkv_g2=rwkv_g2[l], rwkv_kk=rwkv_kk[l], rwkv_ka=rwkv_ka[l],
                 rwkv_rk=rwkv_rk[l], rwkv_ln_g=rwkv_ln_g[l], rwkv_ln_b=rwkv_ln_b[l], rwkv_proj=rwkv_proj[l],
                 attn_proj=attn_proj[l], w_out=w_out[l], ln1_g=ln1_g[l], ln1_b=ln1_b[l],
                 router_group=router_group[l], router_expert=router_expert[l],
                 moe_w_gate=moe_w_gate[l], moe_w_up=moe_w_up[l], moe_w_down=moe_w_down[l],
                 ln2_g=ln2_g[l], ln2_b=ln2_b[l])
        xp, c_new, s_new, w_new, a_new = trunk_layer(
            xp, p,
            jnp.zeros((BATCH, CONV_K - 1, CONV_DIM), xp.dtype),
            jnp.zeros((BATCH, RWKV_U), xp.dtype),
            jnp.zeros((BATCH, RWKV_HEADS, RWKV_HEAD_DIM, RWKV_HEAD_DIM), xp.dtype),
            None)
        conv_p.append(c_new)
        shift_p.append(s_new)
        wkv_p.append(w_new)
        for g in range(N_DIL):
            att_p[g].append(a_new[g])
        xs, c_new, s_new, w_new, a_new = trunk_layer(
            xs, p, state_conv[l], state_shift[l], state_wkv[l],
            (cache_attn_w128[l], cache_attn_w512[l], cache_attn_w2048[l]))
        conv_s.append(c_new)
        shift_s.append(s_new)
        wkv_s.append(w_new)
        for g in range(N_DIL):
            att_s[g].append(a_new[g])
    new_conv_prompt = jnp.stack(conv_p)
    new_conv_sample = jnp.stack(conv_s)
    new_shift_prompt = jnp.stack(shift_p)
    new_shift_sample = jnp.stack(shift_s)
    new_wkv_prompt = jnp.stack(wkv_p)
    new_wkv_sample = jnp.stack(wkv_s)
    new_attn_w128_prompt = jnp.stack(att_p[0])
    new_attn_w128_sample = jnp.stack(att_s[0])
    new_attn_w512_prompt = jnp.stack(att_p[1])
    new_attn_w512_sample = jnp.stack(att_s[1])
    new_attn_w2048_prompt = jnp.stack(att_p[2])
    new_attn_w2048_sample = jnp.stack(att_s[2])
    return (xp, xs, new_conv_prompt, new_conv_sample, new_shift_prompt, new_shift_sample,
            new_wkv_prompt, new_wkv_sample, new_attn_w128_prompt, new_attn_w128_sample,
            new_attn_w512_prompt, new_attn_w512_sample, new_attn_w2048_prompt, new_attn_w2048_sample)
```

```python
import functools
import math

import jax
import jax.numpy as jnp
import numpy as np
from jax import lax
from jax.experimental import pallas as pl
from jax.experimental.pallas import tpu as pltpu

F32 = jnp.float32
BF16 = jnp.bfloat16

D_MODEL = 1024
CONV_DIM = D_MODEL // 2
CONV_K = 31
RWKV_N = 64
RWKV_H = D_MODEL // 128
RWKV_C = RWKV_H * RWKV_N
LORA_W, LORA_A, LORA_G = 64, 64, 128
RWKV_U = 3 * RWKV_C + LORA_W + LORA_A + LORA_G
RWKV_LN_EPS = 64e-5
HEAD_DIM = 64
ATT_HG = 4
ATT_W = ATT_HG * HEAD_DIM
DILATIONS = ((128, 1), (512, 4), (2048, 16))
N_DIL = len(DILATIONS)
ATT_QKV = 3 * N_DIL * ATT_W
N_BUCKETS = 32
MAX_DISTANCE = 2048
N_BRANCH = 3
N_GROUPS = 4
EXPERTS_PER_GROUP = 8
N_EXPERTS = N_GROUPS * EXPERTS_PER_GROUP
TOP_K = 2
D_EXPERT = D_MODEL // 2
MOE_BLOCK = 128
LN_EPS = 1e-5
NEG_INF = -1e30
ROUTER_PAD = 128

LANES = 128
SUBLANES = 8
VMEM_LIMIT = 48 * 1024 * 1024

NN = (((1,), (0,)), ((), ()))
NT = (((1,), (1,)), ((), ()))
TN = (((0,), (0,)), ((), ()))


def _cparams(*sem):
    return pltpu.CompilerParams(dimension_semantics=sem, vmem_limit_bytes=VMEM_LIMIT)


def _dot(a, b, dims=NN):
    return lax.dot_general(a, b, dims, preferred_element_type=F32)


def _pieces(x, n):
    if x.dtype == BF16:
        return [x]
    out, r = [], x
    for i in range(n):
        p = r.astype(BF16)
        out.append(p)
        if i + 1 < n:
            r = r - p.astype(F32)
    return out


def _mm(a, b, dims=NN, pa=1, pb=1):
    ap, bp = _pieces(a, pa), _pieces(b, pb)
    order = max(len(ap), len(bp))
    acc = None
    for i in reversed(range(len(ap))):
        for j in reversed(range(len(bp))):
            if i + j < order:
                t = _dot(ap[i], bp[j], dims)
                acc = t if acc is None else acc + t
    return acc


def _sigmoid(x):
    return 1.0 / (1.0 + jnp.exp(-x))


def _layer_norm(x, g, b, eps):
    mu = jnp.mean(x, axis=-1, keepdims=True)
    xc = x - mu
    var = jnp.mean(xc * xc, axis=-1, keepdims=True)
    return xc * lax.rsqrt(var + eps) * g + b


def _proj_kernel(x_ref, w_ref, b_ref, o_ref, *, epilogue):
    acc = _dot(x_ref[...], w_ref[...])
    if epilogue == "glu":
        half = acc.shape[1] // 2
        o_ref[...] = acc[:, :half] * _sigmoid(acc[:, half:])
    elif epilogue == "sigmoid_bias":
        o_ref[...] = _sigmoid(acc + b_ref[...])
    else:
        o_ref[...] = acc


def _proj(x, w, bias, *, epilogue, tn, tm):
    n, k = x.shape
    n_cols = w.shape[1]
    out_cols = n_cols // 2 if epilogue == "glu" else n_cols
    out_tn = tn // 2 if epilogue == "glu" else tn
    return pl.pallas_call(
        functools.partial(_proj_kernel, epilogue=epilogue),
        grid=(n_cols // tn, n // tm),
        in_specs=[pl.BlockSpec((tm, k), lambda j, i: (i, 0)),
                  pl.BlockSpec((k, tn), lambda j, i: (0, j)),
                  pl.BlockSpec((1, tn), lambda j, i: (0, j))],
        out_specs=pl.BlockSpec((tm, out_tn), lambda j, i: (i, j)),
        out_shape=jax.ShapeDtypeStruct((n, out_cols), F32),
        compiler_params=_cparams("parallel", "parallel"),
        name="in_proj_" + epilogue,
    )(x, w, bias)


CONV_HIST = 32
CONV_ROWS = 64


def _conv_kernel(z_ref, hist_ref, dw_ref, dwb_ref, g_ref, b_ref, o_ref, zbuf):
    tt = z_ref.shape[1]

    @pl.when(pl.program_id(1) == 0)
    def _():
        zbuf[0:CONV_HIST, :] = hist_ref[0]

    @pl.when(pl.program_id(1) != 0)
    def _():
        zbuf[0:CONV_HIST, :] = zbuf[tt:tt + CONV_HIST, :]

    zbuf[CONV_HIST:CONV_HIST + tt, :] = z_ref[0]
    pad = CONV_HIST - (CONV_K - 1)
    rb = min(tt, CONV_ROWS)
    for r0 in range(0, tt, rb):
        acc = jnp.zeros((rb, CONV_DIM), F32)
        for k in range(CONV_K):
            acc = acc + zbuf[r0 + pad + k:r0 + pad + k + rb, :] * dw_ref[k:k + 1, :]
        y = _layer_norm(acc + dwb_ref[...], g_ref[...], b_ref[...], LN_EPS)
        o_ref[0, r0:r0 + rb, :] = (y * _sigmoid(y)).astype(BF16)


def _conv(z, hist, dw, dwb, g, b, *, tt):
    bsz, t, _ = z.shape
    vec = pl.BlockSpec((1, CONV_DIM), lambda i, j: (0, 0))
    return pl.pallas_call(
        _conv_kernel,
        grid=(bsz, t // tt),
        in_specs=[pl.BlockSpec((1, tt, CONV_DIM), lambda i, j: (i, j, 0)),
                  pl.BlockSpec((1, CONV_HIST, CONV_DIM), lambda i, j: (i, 0, 0)),
                  pl.BlockSpec((CONV_K, CONV_DIM), lambda i, j: (0, 0)),
                  vec, vec, vec],
        out_specs=pl.BlockSpec((1, tt, CONV_DIM), lambda i, j: (i, j, 0)),
        out_shape=jax.ShapeDtypeStruct((bsz, t, CONV_DIM), BF16),
        scratch_shapes=[pltpu.VMEM((CONV_HIST + tt, CONV_DIM), F32)],
        compiler_params=_cparams("parallel", "arbitrary"),
        name="conv",
    )(z, hist, dw, dwb, g, b)


def _head_sum(x, ones_bd):
    return _mm(x, ones_bd, NN, pa=3, pb=1)


def _rwkv_pre_kernel(u_ref, shift_ref, mu_ref, w0_ref, w2_ref, a0_ref, a2_ref, g2_ref, kk_ref, ka_ref,
                     rk_ref, ones_ref,
                     r_ref, k_ref, v_ref, lw_ref, al_ref, be_ref, gate_ref, bonus_ref, ubuf):
    tt = u_ref.shape[1]
    c = RWKV_C

    @pl.when(pl.program_id(1) == 0)
    def _():
        ubuf[0:SUBLANES, :] = shift_ref[0]

    @pl.when(pl.program_id(1) != 0)
    def _():
        ubuf[0:SUBLANES, :] = ubuf[tt:tt + SUBLANES, :]

    u = u_ref[0]
    ubuf[SUBLANES:SUBLANES + tt, :] = u
    u_prev = ubuf[SUBLANES - 1:SUBLANES - 1 + tt, :]
    um = u + (u_prev - u) * mu_ref[...]
    r, k, v = um[:, 0:c], um[:, c:2 * c], um[:, 2 * c:3 * c]
    lo = um[:, 3 * c:3 * c + LORA_W + LORA_A]
    lane = lax.broadcasted_iota(jnp.int32, lo.shape, 1)
    lo = jnp.where(lane < LORA_W, jnp.tanh(lo), lo).astype(BF16)
    g_lo = _sigmoid(um[:, 3 * c + LORA_W + LORA_A:]).astype(BF16)
    xw = -(w0_ref[...] + _dot(lo, w2_ref[...]))
    softplus = jnp.maximum(xw, 0.0) + jnp.log(1.0 + jnp.exp(-jnp.abs(xw)))
    lw_ref[0] = -jnp.exp(-softplus - 0.5)
    a = _sigmoid(a0_ref[...] + _dot(lo, a2_ref[...]))
    gate_ref[0] = _dot(g_lo, g2_ref[...])
    ones_bd = ones_ref[...]
    kk = k * kk_ref[...]
    norm = jnp.sqrt(_head_sum(kk * kk, ones_bd))
    kk = kk / jnp.maximum(norm, 1e-12)
    k = k * (1.0 + (a - 1.0) * ka_ref[...])
    r_ref[0] = r
    k_ref[0] = k
    v_ref[0] = v
    al_ref[0] = -kk
    be_ref[0] = kk * a
    bonus_ref[0] = _head_sum(r * k * rk_ref[...], ones_bd) * v


def _rwkv_pre(u, shift8, mu, w0, w2p, a0, a2p, g2, kkp, ka, rk, ones_bd, *, tt):
    bsz, t, _ = u.shape
    full = lambda a: pl.BlockSpec(a.shape, lambda i, j: (0,) * a.ndim)
    seq = pl.BlockSpec((1, tt, RWKV_C), lambda i, j: (i, j, 0))
    return pl.pallas_call(
        _rwkv_pre_kernel,
        grid=(bsz, t // tt),
        in_specs=[pl.BlockSpec((1, tt, RWKV_U), lambda i, j: (i, j, 0)),
                  pl.BlockSpec((1, SUBLANES, RWKV_U), lambda i, j: (i, 0, 0)),
                  full(mu), full(w0), full(w2p), full(a0), full(a2p), full(g2), full(kkp), full(ka),
                  full(rk), full(ones_bd)],
        out_specs=[seq] * 8,
        out_shape=[jax.ShapeDtypeStruct((bsz, t, RWKV_C), F32)] * 8,
        scratch_shapes=[pltpu.VMEM((SUBLANES + tt, RWKV_U), F32)],
        compiler_params=_cparams("parallel", "arbitrary"),
        name="rwkv_pre",
    )(u, shift8, mu, w0, w2p, a0, a2p, g2, kkp, ka, rk, ones_bd)


def _scan_kernel(r_ref, k_ref, v_ref, lw_ref, al_ref, be_ref, gate_ref, bonus_ref, s0_ref, g_ref, b_ref,
                 y_ref, s_ref, state):
    c = r_ref.shape[1]
    n = RWKV_N

    @pl.when(pl.program_id(1) == 0)
    def _():
        state[...] = s0_ref[0]

    row = lax.broadcasted_iota(jnp.int32, (c, c), 0)
    col = lax.broadcasted_iota(jnp.int32, (c, c), 1)
    incl = row >= col
    strict = row > col
    lower = incl.astype(BF16)
    eye = (row == col).astype(F32)

    lw = lw_ref[0]
    cum = _mm(lower, lw, NN, pa=1, pb=3)
    e_p = jnp.exp(cum)
    e_m = jnp.exp(-cum)
    cum_end = cum[c - 1:c, :]
    e_end = jnp.exp(cum_end - cum)
    p_end = jnp.exp(cum_end)
    rq_all = r_ref[0] * e_p
    aq_all = al_ref[0] * jnp.exp(cum - lw)
    kd_all = k_ref[0] * e_m
    bd_all = be_ref[0] * e_m
    kend_all = k_ref[0] * e_end
    bend_all = be_ref[0] * e_end
    v_all = v_ref[0]

    for h in range(RWKV_H):
        sl = slice(h * n, (h + 1) * n)
        rq, aq, kd, bd = rq_all[:, sl], aq_all[:, sl], kd_all[:, sl], bd_all[:, sl]
        v = v_all[:, sl]
        s_h = state[h]
        a_ab = jnp.where(strict, _mm(aq, bd, NT, 2, 2), 0.0)
        a_ak = jnp.where(strict, _mm(aq, kd, NT, 2, 2), 0.0)
        a_rb = jnp.where(incl, _mm(rq, bd, NT, 2, 2), 0.0)
        a_rk = jnp.where(incl, _mm(rq, kd, NT, 2, 2), 0.0)
        inv = eye + a_ab
        pw = a_ab
        span = 2
        while span < c:
            pw = _mm(pw, pw, NN, 2, 2)
            inv = inv + _mm(inv, pw, NN, 2, 2)
            span *= 2
        rhs = _mm(aq, s_h, NT, 2, 2) + _mm(a_ak, v, NN, 2, 2)
        u = _mm(inv, rhs, NN, 2, 2)
        y = _mm(rq, s_h, NT, 2, 2) + _mm(a_rb, u, NN, 2, 2) + _mm(a_rk, v, NN, 2, 2)
        state[h] = (s_h * p_end[:, sl] + _mm(u, bend_all[:, sl], TN, 2, 2)
                    + _mm(v, kend_all[:, sl], TN, 2, 2))
        mu = jnp.mean(y, axis=-1, keepdims=True)
        yc = y - mu
        var = jnp.mean(yc * yc, axis=-1, keepdims=True)
        y = yc * lax.rsqrt(var + RWKV_LN_EPS) * g_ref[:, sl] + b_ref[:, sl]
        y_ref[0, :, sl] = ((y + bonus_ref[0, :, sl]) * gate_ref[0, :, sl]).astype(BF16)

    @pl.when(pl.program_id(1) == pl.num_programs(1) - 1)
    def _():
        s_ref[0] = state[...]


def _scan(r, k, v, lw, al, be, gate, bonus, s0, g, b, *, chunk):
    bsz, t, _ = r.shape
    seq = pl.BlockSpec((1, chunk, RWKV_C), lambda i, j: (i, j, 0))
    st = pl.BlockSpec((1, RWKV_H, RWKV_N, RWKV_N), lambda i, j: (i, 0, 0, 0))
    vec = pl.BlockSpec((1, RWKV_C), lambda i, j: (0, 0))
    return pl.pallas_call(
        _scan_kernel,
        grid=(bsz, t // chunk),
        in_specs=[seq] * 8 + [st, vec, vec],
        out_specs=[seq, st],
        out_shape=[jax.ShapeDtypeStruct((bsz, t, RWKV_C), BF16),
                   jax.ShapeDtypeStruct((bsz, RWKV_H, RWKV_N, RWKV_N), F32)],
        scratch_shapes=[pltpu.VMEM((RWKV_H, RWKV_N, RWKV_N), F32)],
        compiler_params=_cparams("parallel", "arbitrary"),
        name="rwkv_scan",
    )(r, k, v, lw, al, be, gate, bonus, s0, g, b)


ATT_SPAN = 128
SAMPLE_TAIL = LANES


def _softmax_pv(s, v2):
    m = jnp.max(s, axis=-1, keepdims=True)
    e = jnp.exp(s - m)
    den = jnp.sum(e, axis=-1, keepdims=True)
    o = _dot((e / den).astype(BF16), v2)
    return o, m + jnp.log(den)


def _attn_prompt_kernel(q_ref, kc_ref, kp_ref, vc_ref, vp_ref, bias_ref, o_ref, lse_ref):
    scale = HEAD_DIM ** -0.5
    for h in range(ATT_HG):
        sl = slice(h * HEAD_DIM, (h + 1) * HEAD_DIM)
        q = q_ref[0, :, sl].astype(BF16)
        s = jnp.concatenate([_dot(q, kp_ref[0, :, sl].astype(BF16), NT),
                             _dot(q, kc_ref[0, :, sl].astype(BF16), NT)], axis=1)
        s = s * scale + bias_ref[0, h]
        v2 = jnp.concatenate([vp_ref[0, :, sl], vc_ref[0, :, sl]], axis=0).astype(BF16)
        o, lse = _softmax_pv(s, v2)
        o_ref[0, :, sl] = o
        lse_ref[0, :, sl] = jnp.broadcast_to(lse, o.shape)


def _attn_prompt(ua, bias, *, g, dil):
    bsz, s, _ = ua.shape
    m = s // dil
    nb = m // ATT_SPAN
    per_tok = ATT_QKV // ATT_W
    uav = ua.reshape(bsz, m, dil * ATT_QKV)

    def col(which):
        return lambda b, r, i: (b, i, r * per_tok + which * N_DIL + g)

    def col_prev(which):
        return lambda b, r, i: (b, jnp.maximum(i - 1, 0), r * per_tok + which * N_DIL + g)

    blk = (1, ATT_SPAN, ATT_W)
    o, lse = pl.pallas_call(
        _attn_prompt_kernel,
        grid=(bsz, dil, nb),
        in_specs=[pl.BlockSpec(blk, col(0)), pl.BlockSpec(blk, col(1)), pl.BlockSpec(blk, col_prev(1)),
                  pl.BlockSpec(blk, col(2)), pl.BlockSpec(blk, col_prev(2)),
                  pl.BlockSpec((1, ATT_HG, ATT_SPAN, 2 * ATT_SPAN), lambda b, r, i: (jnp.minimum(i, 1), 0, 0, 0))],
        out_specs=[pl.BlockSpec(blk, lambda b, r, i: (b, i, r))] * 2,
        out_shape=[jax.ShapeDtypeStruct((bsz, m, dil * ATT_W), F32)] * 2,
        compiler_params=_cparams("parallel", "parallel", "parallel"),
        name="attn_prompt",
    )(uav, uav, uav, uav, uav, bias)
    return o.reshape(bsz, s, ATT_W), lse.reshape(bsz, s, ATT_W)


def _attn_sample_kernel(u_ref, cache_ref, bias_ref, o_ref, lse_ref, new_ref, kcat, vcat, *, g):
    t = u_ref.shape[1]
    rows = cache_ref.shape[2]
    scale = HEAD_DIM ** -0.5
    base = g * ATT_W
    kcat[0:rows, :] = cache_ref[0, 0]
    vcat[0:rows, :] = cache_ref[0, 1]
    tail = jnp.zeros((SAMPLE_TAIL, ATT_W), F32)
    kcat[rows:rows + SAMPLE_TAIL, :] = tail
    vcat[rows:rows + SAMPLE_TAIL, :] = tail
    kcat[rows:rows + t, :] = u_ref[0, :, N_DIL * ATT_W + base:N_DIL * ATT_W + base + ATT_W]
    vcat[rows:rows + t, :] = u_ref[0, :, 2 * N_DIL * ATT_W + base:2 * N_DIL * ATT_W + base + ATT_W]
    new_ref[0, 0] = kcat[t:t + rows, :]
    new_ref[0, 1] = vcat[t:t + rows, :]
    for h in range(ATT_HG):
        sl = slice(h * HEAD_DIM, (h + 1) * HEAD_DIM)
        q = u_ref[0, :, base + h * HEAD_DIM:base + (h + 1) * HEAD_DIM].astype(BF16)
        s = _dot(q, kcat[:, sl].astype(BF16), NT) * scale + bias_ref[h]
        o, lse = _softmax_pv(s, vcat[:, sl].astype(BF16))
        o_ref[0, :, sl] = o
        lse_ref[0, :, sl] = jnp.broadcast_to(lse, o.shape)


def _attn_sample(ua, cache, bias, *, g):
    bsz, t, _ = ua.shape
    rows = cache.shape[2]
    out = pl.BlockSpec((1, t, ATT_W), lambda b: (b, 0, 0))
    cb = pl.BlockSpec((1, 2, rows, ATT_W), lambda b: (b, 0, 0, 0))
    return pl.pallas_call(
        functools.partial(_attn_sample_kernel, g=g),
        grid=(bsz,),
        in_specs=[pl.BlockSpec((1, t, ATT_QKV), lambda b: (b, 0, 0)), cb,
                  pl.BlockSpec(bias.shape, lambda b: (0, 0, 0))],
        out_specs=[out, out, cb],
        out_shape=[jax.ShapeDtypeStruct((bsz, t, ATT_W), F32)] * 2
                  + [jax.ShapeDtypeStruct(cache.shape, F32)],
        scratch_shapes=[pltpu.VMEM((rows + SAMPLE_TAIL, ATT_W), F32)] * 2,
        compiler_params=_cparams("parallel"),
        name="attn_sample",
    )(ua, cache, bias)


def _rel_bucket(dist):
    max_exact = N_BUCKETS // 2
    large = max_exact + (jnp.log(jnp.maximum(dist, 1).astype(F32) / max_exact)
                         / math.log(MAX_DISTANCE / max_exact) * (N_BUCKETS - max_exact)).astype(jnp.int32)
    return jnp.where(dist < max_exact, dist, jnp.minimum(large, N_BUCKETS - 1))


def _prompt_bias(tab, dil):
    span = ATT_SPAN
    qi = np.arange(span)[:, None]
    ki = np.arange(2 * span)[None, :]
    j = span + qi - ki
    band = (j >= 0) & (j <= span)
    bias = jnp.transpose(tab[_rel_bucket(jnp.asarray(dil * np.clip(j, 0, span)))], (2, 0, 1))
    rest = jnp.where(band[None], bias, NEG_INF)
    first = jnp.where((band & (ki >= span))[None], bias, NEG_INF)
    return jnp.stack([first, rest])


def _sample_bias(tab, dil, rows, t):
    rho = np.arange(rows + SAMPLE_TAIL)[None, :]
    d = rows + np.arange(t)[:, None] - rho
    valid = (d >= 0) & (d % dil == 0) & (d // dil <= ATT_SPAN) & (rho < rows + t)
    bias = jnp.transpose(tab[_rel_bucket(jnp.asarray(np.clip(d, 0, None)))], (2, 0, 1))
    return jnp.where(valid[None], bias, NEG_INF)


def _merge_kernel(x_ref, yc_ref, yr_ref, o0_ref, o1_ref, o2_ref, l0_ref, l1_ref, l2_ref, gates_ref,
                  wc_ref, wr_ref, wa_ref, wo_ref, g_ref, b_ref, rt_ref, h_ref, logit_ref, *, alpha):
    l0, l1, l2 = l0_ref[...], l1_ref[...], l2_ref[...]
    m = jnp.maximum(jnp.maximum(l0, l1), l2)
    e0, e1, e2 = jnp.exp(l0 - m), jnp.exp(l1 - m), jnp.exp(l2 - m)
    den = e0 + e1 + e2
    o = o0_ref[...] * (e0 / den) + o1_ref[...] * (e1 / den) + o2_ref[...] * (e2 / den)
    d = D_MODEL
    merged = (gates_ref[:, 0:d] * _dot(yc_ref[...], wc_ref[...])
              + gates_ref[:, d:2 * d] * _dot(yr_ref[...], wr_ref[...])
              + gates_ref[:, 2 * d:3 * d] * _dot(o.astype(BF16), wa_ref[...]))
    pre = alpha * x_ref[...] + _dot(merged.astype(BF16), wo_ref[...])
    h = _layer_norm(pre, g_ref[...], b_ref[...], LN_EPS)
    h_ref[...] = h
    logit_ref[...] = _mm(h, rt_ref[...], NN, 3, 3)


def _merge(x, yc, yr, os_, ls_, gates, wc, wr, wa, wo, g, b, rt, *, alpha, tm):
    n = x.shape[0]
    row = lambda c: pl.BlockSpec((tm, c), lambda i: (i, 0))
    full = lambda a: pl.BlockSpec(a.shape, lambda i: (0,) * a.ndim)
    return pl.pallas_call(
        functools.partial(_merge_kernel, alpha=alpha),
        grid=(n // tm,),
        in_specs=[row(D_MODEL), row(CONV_DIM), row(RWKV_C)] + [row(ATT_W)] * 6 + [row(N_BRANCH * D_MODEL)]
                 + [full(a) for a in (wc, wr, wa, wo, g, b, rt)],
        out_specs=[row(D_MODEL), row(ROUTER_PAD)],
        out_shape=[jax.ShapeDtypeStruct((n, D_MODEL), F32), jax.ShapeDtypeStruct((n, ROUTER_PAD), F32)],
        compiler_params=_cparams("parallel"),
        name="merge",
    )(x, yc, yr, *os_, *ls_, gates, wc, wr, wa, wo, g, b, rt)


def _expert_kernel(be_ref, cnt_ref, src_ref, dst_ref, h_hbm, w_ref, wg_ref, wu_ref, wd_ref, out_hbm,
                   xbuf, ybuf, sem):
    i = pl.program_id(0)
    cnt = cnt_ref[i]
    base = i * MOE_BLOCK

    def gather(r):
        return pltpu.make_async_copy(h_hbm.at[pl.ds(src_ref[base + r], 1)], xbuf.at[pl.ds(r, 1)], sem.at[0])

    def scatter(r):
        return pltpu.make_async_copy(ybuf.at[pl.ds(r, 1)], out_hbm.at[pl.ds(dst_ref[base + r], 1)], sem.at[1])

    @pl.when(i == 0)
    def _():
        xbuf[...] = jnp.zeros(xbuf.shape, F32)

    @pl.when(cnt > 0)
    def _():
        def start_g(r, carry):
            @pl.when(r < cnt)
            def _():
                gather(r).start()
            return carry

        def wait_g(r, carry):
            @pl.when(r < cnt)
            def _():
                gather(r).wait()
            return carry

        def start_s(r, carry):
            @pl.when(r < cnt)
            def _():
                scatter(r).start()
            return carry

        def wait_s(r, carry):
            @pl.when(r < cnt)
            def _():
                scatter(r).wait()
            return carry

        lax.fori_loop(0, MOE_BLOCK, start_g, 0)
        lax.fori_loop(0, MOE_BLOCK, wait_g, 0)
        row = lax.broadcasted_iota(jnp.int32, (MOE_BLOCK, 1), 0)
        xb = jnp.where(row < cnt, xbuf[...], 0.0).astype(BF16)
        gate = _dot(xb, wg_ref[0])
        act = (gate * _sigmoid(gate) * _dot(xb, wu_ref[0])).astype(BF16)
        ybuf[...] = _dot(act, wd_ref[0]) * w_ref[0]
        lax.fori_loop(0, MOE_BLOCK, start_s, 0)
        lax.fori_loop(0, MOE_BLOCK, wait_s, 0)


def _experts(h, block_e, block_cnt, slot_src, slot_dst, slot_w, wg, wu, wd):
    n = h.shape[0]
    n_blocks = block_e.shape[0]
    grid_spec = pltpu.PrefetchScalarGridSpec(
        num_scalar_prefetch=4,
        grid=(n_blocks,),
        in_specs=[pl.BlockSpec(memory_space=pl.ANY),
                  pl.BlockSpec((1, MOE_BLOCK, 1), lambda i, be, cnt, src, dst: (i, 0, 0)),
                  pl.BlockSpec((1, D_MODEL, D_EXPERT), lambda i, be, cnt, src, dst: (be[i], 0, 0)),
                  pl.BlockSpec((1, D_MODEL, D_EXPERT), lambda i, be, cnt, src, dst: (be[i], 0, 0)),
                  pl.BlockSpec((1, D_EXPERT, D_MODEL), lambda i, be, cnt, src, dst: (be[i], 0, 0))],
        out_specs=pl.BlockSpec(memory_space=pl.ANY),
        scratch_shapes=[pltpu.VMEM((MOE_BLOCK, D_MODEL), F32), pltpu.VMEM((MOE_BLOCK, D_MODEL), F32),
                        pltpu.SemaphoreType.DMA((2,))],
    )
    return pl.pallas_call(
        _expert_kernel,
        grid_spec=grid_spec,
        out_shape=jax.ShapeDtypeStruct((TOP_K * n, D_MODEL), F32),
        compiler_params=_cparams("arbitrary"),
        name="experts",
    )(block_e, block_cnt, slot_src, slot_dst, h, slot_w, wg, wu, wd)


def _route(logits):
    n = logits.shape[0]
    lg = logits[:, :N_GROUPS]
    grp = jnp.argmax(lg, axis=-1)
    p_grp = jnp.take_along_axis(jax.nn.softmax(lg, axis=-1), grp[:, None], axis=-1)
    le = logits[:, N_GROUPS:N_GROUPS + N_EXPERTS].reshape(n, N_GROUPS, EXPERTS_PER_GROUP)
    le = jnp.take_along_axis(le, grp[:, None, None], axis=1)[:, 0]
    top_v, top_i = lax.top_k(le, TOP_K)
    gate = p_grp * jax.nn.softmax(top_v, axis=-1)
    eid = (grp[:, None] * EXPERTS_PER_GROUP + top_i).reshape(-1).astype(jnp.int32)
    a = n * TOP_K
    n_blocks = -(-a // MOE_BLOCK) + N_EXPERTS
    onehot = (eid[:, None] == jnp.arange(N_EXPERTS, dtype=jnp.int32)[None, :]).astype(jnp.int32)
    rank = jnp.sum((jnp.cumsum(onehot, axis=0) - onehot) * onehot, axis=1)
    counts = jnp.sum(onehot, axis=0)
    padded = (counts + MOE_BLOCK - 1) // MOE_BLOCK * MOE_BLOCK
    pend = jnp.cumsum(padded)
    pstart = pend - padded
    dest = pstart[eid] + rank
    assign = jnp.arange(a, dtype=jnp.int32)
    slot_src = jnp.zeros((n_blocks * MOE_BLOCK,), jnp.int32).at[dest].set(assign // TOP_K)
    slot_dst = jnp.zeros((n_blocks * MOE_BLOCK,), jnp.int32).at[dest].set(assign)
    slot_w = jnp.zeros((n_blocks * MOE_BLOCK,), F32).at[dest].set(gate.reshape(-1))
    blk_start = jnp.arange(n_blocks, dtype=jnp.int32) * MOE_BLOCK
    block_e = jnp.minimum(jnp.searchsorted(pend, blk_start, side="right"), N_EXPERTS - 1).astype(jnp.int32)
    used = pstart[block_e] + counts[block_e]
    block_cnt = jnp.clip(used - blk_start, 0, MOE_BLOCK).astype(jnp.int32)
    block_cnt = jnp.where(blk_start < pend[-1], block_cnt, 0)
    return block_e, block_cnt, slot_src, slot_dst, slot_w.reshape(n_blocks, MOE_BLOCK, 1)


def _ln2_kernel(h_ref, f_ref, g_ref, b_ref, y_ref, yb_ref, *, alpha):
    pre = alpha * h_ref[...] + f_ref[:, 0:D_MODEL] + f_ref[:, D_MODEL:2 * D_MODEL]
    y = _layer_norm(pre, g_ref[...], b_ref[...], LN_EPS)
    y_ref[...] = y
    yb_ref[...] = y.astype(BF16)


def _ln2(h, f, g, b, *, alpha, tm):
    n = h.shape[0]
    vec = pl.BlockSpec((1, D_MODEL), lambda i: (0, 0))
    return pl.pallas_call(
        functools.partial(_ln2_kernel, alpha=alpha),
        grid=(n // tm,),
        in_specs=[pl.BlockSpec((tm, D_MODEL), lambda i: (i, 0)),
                  pl.BlockSpec((tm, TOP_K * D_MODEL), lambda i: (i, 0)), vec, vec],
        out_specs=[pl.BlockSpec((tm, D_MODEL), lambda i: (i, 0))] * 2,
        out_shape=[jax.ShapeDtypeStruct((n, D_MODEL), F32), jax.ShapeDtypeStruct((n, D_MODEL), BF16)],
        compiler_params=_cparams("parallel"),
        name="ln2",
    )(h, f, g, b)


def _row(v):
    return v.reshape(1, -1)


def _layer(x, xb, p, conv_state, shift, wkv, caches, *, alpha, prompt):
    bsz, t, d = x.shape
    n = bsz * t
    tm = 512 if prompt else n
    xb2 = xb.reshape(n, d)
    zero_b = jnp.zeros((1, N_BRANCH * D_MODEL), F32)
    o1 = 2 * CONV_DIM
    o2 = o1 + RWKV_U
    o3 = o2 + ATT_QKV

    z = _proj(xb2, p["w_in"][:, :o1], zero_b[:, :o1], epilogue="glu", tn=o1, tm=tm)
    ur = _proj(xb2, p["w_in"][:, o1:o2], zero_b[:, :RWKV_U], epilogue="none", tn=RWKV_U, tm=tm)
    ua = _proj(xb2, p["w_in"][:, o2:o3], zero_b[:, :ATT_QKV], epilogue="none", tn=ATT_QKV // 2, tm=tm)
    gates = _proj(xb2, p["w_in"][:, o3:], p["b_gate"], epilogue="sigmoid_bias", tn=D_MODEL, tm=tm)
    z = z.reshape(bsz, t, CONV_DIM)
    ur = ur.reshape(bsz, t, RWKV_U)
    ua = ua.reshape(bsz, t, ATT_QKV)

    hist = jnp.pad(conv_state, ((0, 0), (CONV_HIST - (CONV_K - 1), 0), (0, 0)))
    yc = _conv(z, hist, p["conv_dw"], p["conv_dw_b"], p["conv_ln_g"], p["conv_ln_b"], tt=256 if prompt else t)
    conv_new = jnp.concatenate([conv_state, z], axis=1)[:, -(CONV_K - 1):]

    shift8 = jnp.pad(shift[:, None, :], ((0, 0), (SUBLANES - 1, 0), (0, 0)))
    pre = _rwkv_pre(ur, shift8, p["rwkv_mu"], p["rwkv_w0"], p["rwkv_w2p"], p["rwkv_a0"], p["rwkv_a2p"],
                    p["rwkv_g2"], p["rwkv_kk"], p["rwkv_ka"], p["rwkv_rk"], p["ones_bd"],
                    tt=256 if prompt else t)
    chunk = 64 if prompt else SUBLANES
    if t % chunk:
        pre = [jnp.pad(a, ((0, 0), (0, chunk - t % chunk), (0, 0))) for a in pre]
    yr, wkv_new = _scan(*pre, wkv, p["rwkv_ln_g"], p["rwkv_ln_b"], chunk=chunk)
    yr = yr[:, :t]
    shift_new = ur[:, -1]

    os_, ls_, att_new = [], [], []
    for g, (window, dil) in enumerate(DILATIONS):
        tab = p["rel_bias"][:, g * ATT_HG:(g + 1) * ATT_HG]
        if prompt:
            o, lse = _attn_prompt(ua, _prompt_bias(tab, dil), g=g, dil=dil)
            rows = min(window, t)
            kv = ua[:, t - rows:].reshape(bsz, rows, 3, N_DIL, ATT_HG, HEAD_DIM)
            att_new.append(jnp.stack([kv[:, :, 1, g], kv[:, :, 2, g]], axis=1))
        else:
            cache = caches[g]
            rows = cache.shape[2]
            o, lse, new = _attn_sample(ua, cache.reshape(bsz, 2, rows, ATT_W),
                                       _sample_bias(tab, dil, rows, t), g=g)
            att_new.append(new.reshape(cache.shape))
        os_.append(o.reshape(n, ATT_W))
        ls_.append(lse.reshape(n, ATT_W))

    h, logits = _merge(x.reshape(n, d), yc.reshape(n, CONV_DIM), yr.reshape(n, RWKV_C), os_, ls_, gates,
                       p["conv_proj"], p["rwkv_proj"], p["attn_proj"], p["w_out"], p["ln1_g"], p["ln1_b"],
                       p["router"], alpha=alpha, tm=256 if prompt else n)

    block_e, block_cnt, slot_src, slot_dst, slot_w = _route(logits)
    f = _experts(h, block_e, block_cnt, slot_src, slot_dst, slot_w,
                 p["moe_w_gate"], p["moe_w_up"], p["moe_w_down"])
    y, yb = _ln2(h, f.reshape(n, TOP_K * D_MODEL), p["ln2_g"], p["ln2_b"], alpha=alpha, tm=tm)
    return y.reshape(bsz, t, d), yb.reshape(bsz, t, d), conv_new, shift_new, wkv_new, att_new


def kernel(x_prompt, x_sample, state_conv, state_shift, state_wkv, cache_attn_w128, cache_attn_w512, cache_attn_w2048, rel_bias, w_in, b_gate, conv_dw, conv_dw_b, conv_ln_g, conv_ln_b, conv_proj, rwkv_mu, rwkv_w0, rwkv_w2, rwkv_a0, rwkv_a2, rwkv_g2, rwkv_kk, rwkv_ka, rwkv_rk, rwkv_ln_g, rwkv_ln_b, rwkv_proj, attn_proj, w_out, ln1_g, ln1_b, router_group, router_expert, moe_w_gate, moe_w_up, moe_w_down, ln2_g, ln2_b):
    depth = w_in.shape[0]
    alpha = (2 * depth) ** 0.25
    bp, tp, _ = x_prompt.shape
    caches = (cache_attn_w128, cache_attn_w512, cache_attn_w2048)
    head = np.arange(RWKV_C) // RWKV_N
    ones_bd = jnp.asarray(head[:, None] == head[None, :], BF16)

    xp, xs = x_prompt, x_sample
    xpb, xsb = xp.astype(BF16), xs.astype(BF16)
    outs_p, outs_s = [], []
    for l in range(depth):
        zeros_lora = jnp.zeros((LORA_W, RWKV_C), F32)
        router = jnp.concatenate(
            [router_group[l], router_expert[l],
             jnp.zeros((D_MODEL, ROUTER_PAD - N_GROUPS - N_EXPERTS), F32)], axis=1)
        p = dict(
            rel_bias=rel_bias, w_in=w_in[l].astype(BF16), b_gate=_row(b_gate[l]),
            conv_dw=conv_dw[l], conv_dw_b=_row(conv_dw_b[l]), conv_ln_g=_row(conv_ln_g[l]),
            conv_ln_b=_row(conv_ln_b[l]), conv_proj=conv_proj[l].astype(BF16),
            rwkv_mu=_row(rwkv_mu[l]), rwkv_w0=_row(rwkv_w0[l]),
            rwkv_w2p=jnp.concatenate([rwkv_w2[l], zeros_lora], axis=0).astype(BF16),
            rwkv_a0=_row(rwkv_a0[l]),
            rwkv_a2p=jnp.concatenate([zeros_lora, rwkv_a2[l]], axis=0).astype(BF16),
            rwkv_g2=rwkv_g2[l].astype(BF16), rwkv_kk=_row(rwkv_kk[l]), rwkv_ka=_row(rwkv_ka[l]),
            rwkv_rk=_row(rwkv_rk[l]), rwkv_ln_g=_row(rwkv_ln_g[l]), rwkv_ln_b=_row(rwkv_ln_b[l]),
            rwkv_proj=rwkv_proj[l].astype(BF16), attn_proj=attn_proj[l].astype(BF16),
            w_out=w_out[l].astype(BF16), ln1_g=_row(ln1_g[l]), ln1_b=_row(ln1_b[l]), router=router,
            moe_w_gate=moe_w_gate[l].astype(BF16), moe_w_up=moe_w_up[l].astype(BF16),
            moe_w_down=moe_w_down[l].astype(BF16), ln2_g=_row(ln2_g[l]), ln2_b=_row(ln2_b[l]),
            ones_bd=ones_bd)
        xp, xpb, *new_p = _layer(
            xp, xpb, p,
            jnp.zeros((bp, CONV_K - 1, CONV_DIM), F32), jnp.zeros((bp, RWKV_U), F32),
            jnp.zeros((bp, RWKV_H, RWKV_N, RWKV_N), F32), None, alpha=alpha, prompt=True)
        xs, xsb, *new_s = _layer(
            xs, xsb, p, state_conv[l], state_shift[l], state_wkv[l], [c[l] for c in caches],
            alpha=alpha, prompt=False)
        outs_p.append(new_p)
        outs_s.append(new_s)

    def stack(outs, i):
        return jnp.stack([o[i] for o in outs])

    def stack_att(outs, g):
        return jnp.stack([o[3][g] for o in outs])

    return (xp, xs, stack(outs_p, 0), stack(outs_s, 0), stack(outs_p, 1), stack(outs_s, 1),
            stack(outs_p, 2), stack(outs_s, 2),
            stack_att(outs_p, 0), stack_att(outs_s, 0), stack_att(outs_p, 1), stack_att(outs_s, 1),
            stack_att(outs_p, 2), stack_att(outs_s, 2))
```

```python
import functools
import math

import jax
import jax.numpy as jnp
import numpy as np
from jax import lax
from jax.experimental import pallas as pl
from jax.experimental.pallas import tpu as pltpu

F32 = jnp.float32
BF16 = jnp.bfloat16

D_MODEL = 1024
CONV_DIM = D_MODEL // 2
CONV_K = 31
RWKV_N = 64
RWKV_H = D_MODEL // 128
RWKV_C = RWKV_H * RWKV_N
LORA_W, LORA_A, LORA_G = 64, 64, 128
RWKV_U = 3 * RWKV_C + LORA_W + LORA_A + LORA_G
RWKV_LN_EPS = 64e-5
HEAD_DIM = 64
ATT_HG = 4
ATT_W = ATT_HG * HEAD_DIM
DILATIONS = ((128, 1), (512, 4), (2048, 16))
N_DIL = len(DILATIONS)
ATT_QKV = 3 * N_DIL * ATT_W
N_BUCKETS = 32
MAX_DISTANCE = 2048
N_BRANCH = 3
N_GROUPS = 4
EXPERTS_PER_GROUP = 8
N_EXPERTS = N_GROUPS * EXPERTS_PER_GROUP
TOP_K = 2
D_EXPERT = D_MODEL // 2
MOE_BLOCK = 128
LN_EPS = 1e-5
NEG_INF = -1e30
ROUTER_PAD = 128

LANES = 128
SUBLANES = 8
VMEM_LIMIT = 48 * 1024 * 1024

NN = (((1,), (0,)), ((), ()))
NT = (((1,), (1,)), ((), ()))
TN = (((0,), (0,)), ((), ()))


def _cparams(*sem):
    return pltpu.CompilerParams(dimension_semantics=sem, vmem_limit_bytes=VMEM_LIMIT)


def _dot(a, b, dims=NN):
    return lax.dot_general(a, b, dims, preferred_element_type=F32)


def _pieces(x, n):
    if x.dtype == BF16:
        return [x]
    out, r = [], x
    for i in range(n):
        p = r.astype(BF16)
        out.append(p)
        if i + 1 < n:
            r = r - p.astype(F32)
    return out


def _mm(a, b, dims=NN, pa=1, pb=1):
    ap, bp = _pieces(a, pa), _pieces(b, pb)
    order = max(len(ap), len(bp))
    acc = None
    for i in reversed(range(len(ap))):
        for j in reversed(range(len(bp))):
            if i + j < order:
                t = _dot(ap[i], bp[j], dims)
                acc = t if acc is None else acc + t
    return acc


def _sigmoid(x):
    return 1.0 / (1.0 + jnp.exp(-x))


def _layer_norm(x, g, b, eps):
    mu = jnp.mean(x, axis=-1, keepdims=True)
    xc = x - mu
    var = jnp.mean(xc * xc, axis=-1, keepdims=True)
    return xc * lax.rsqrt(var + eps) * g + b


def _proj_kernel(x_ref, w_ref, b_ref, o_ref, *, epilogue):
    acc = _dot(x_ref[...], w_ref[...])
    if epilogue == "glu":
        half = acc.shape[1] // 2
        o_ref[...] = acc[:, :half] * _sigmoid(acc[:, half:])
    elif epilogue == "sigmoid_bias":
        o_ref[...] = _sigmoid(acc + b_ref[...])
    else:
        o_ref[...] = acc


def _proj(x, w, bias, *, epilogue, tn, tm):
    n, k = x.shape
    n_cols = w.shape[1]
    out_cols = n_cols // 2 if epilogue == "glu" else n_cols
    out_tn = tn // 2 if epilogue == "glu" else tn
    return pl.pallas_call(
        functools.partial(_proj_kernel, epilogue=epilogue),
        grid=(n_cols // tn, n // tm),
        in_specs=[pl.BlockSpec((tm, k), lambda j, i: (i, 0)),
                  pl.BlockSpec((k, tn), lambda j, i: (0, j)),
                  pl.BlockSpec((1, tn), lambda j, i: (0, j))],
        out_specs=pl.BlockSpec((tm, out_tn), lambda j, i: (i, j)),
        out_shape=jax.ShapeDtypeStruct((n, out_cols), F32),
        compiler_params=_cparams("parallel", "parallel"),
        name="in_proj_" + epilogue,
    )(x, w, bias)


CONV_HIST = 32
CONV_ROWS = 64


def _conv_kernel(z_ref, hist_ref, dw_ref, dwb_ref, g_ref, b_ref, o_ref, zbuf):
    tt = z_ref.shape[1]

    @pl.when(pl.program_id(1) == 0)
    def _():
        zbuf[0:CONV_HIST, :] = hist_ref[0]

    @pl.when(pl.program_id(1) != 0)
    def _():
        zbuf[0:CONV_HIST, :] = zbuf[tt:tt + CONV_HIST, :]

    zbuf[CONV_HIST:CONV_HIST + tt, :] = z_ref[0]
    pad = CONV_HIST - (CONV_K - 1)
    rb = min(tt, CONV_ROWS)
    for r0 in range(0, tt, rb):
        acc = jnp.zeros((rb, CONV_DIM), F32)
        for k in range(CONV_K):
            acc = acc + zbuf[r0 + pad + k:r0 + pad + k + rb, :] * dw_ref[k:k + 1, :]
        y = _layer_norm(acc + dwb_ref[...], g_ref[...], b_ref[...], LN_EPS)
        o_ref[0, r0:r0 + rb, :] = (y * _sigmoid(y)).astype(BF16)


def _conv(z, hist, dw, dwb, g, b, *, tt):
    bsz, t, _ = z.shape
    vec = pl.BlockSpec((1, CONV_DIM), lambda i, j: (0, 0))
    return pl.pallas_call(
        _conv_kernel,
        grid=(bsz, t // tt),
        in_specs=[pl.BlockSpec((1, tt, CONV_DIM), lambda i, j: (i, j, 0)),
                  pl.BlockSpec((1, CONV_HIST, CONV_DIM), lambda i, j: (i, 0, 0)),
                  pl.BlockSpec((CONV_K, CONV_DIM), lambda i, j: (0, 0)),
                  vec, vec, vec],
        out_specs=pl.BlockSpec((1, tt, CONV_DIM), lambda i, j: (i, j, 0)),
        out_shape=jax.ShapeDtypeStruct((bsz, t, CONV_DIM), BF16),
        scratch_shapes=[pltpu.VMEM((CONV_HIST + tt, CONV_DIM), F32)],
        compiler_params=_cparams("parallel", "arbitrary"),
        name="conv",
    )(z, hist, dw, dwb, g, b)


def _head_sum(x, ones_bd):
    return _mm(x, ones_bd, NN, pa=3, pb=1)


def _rwkv_pre_kernel(u_ref, shift_ref, mu_ref, w0_ref, w2_ref, a0_ref, a2_ref, g2_ref, kk_ref, ka_ref,
                     rk_ref, ones_ref,
                     r_ref, k_ref, v_ref, lw_ref, al_ref, be_ref, gate_ref, bonus_ref, ubuf):
    tt = u_ref.shape[1]
    c = RWKV_C

    @pl.when(pl.program_id(1) == 0)
    def _():
        ubuf[0:SUBLANES, :] = shift_ref[0]

    @pl.when(pl.program_id(1) != 0)
    def _():
        ubuf[0:SUBLANES, :] = ubuf[tt:tt + SUBLANES, :]

    u = u_ref[0]
    ubuf[SUBLANES:SUBLANES + tt, :] = u
    u_prev = ubuf[SUBLANES - 1:SUBLANES - 1 + tt, :]
    um = u + (u_prev - u) * mu_ref[...]
    r, k, v = um[:, 0:c], um[:, c:2 * c], um[:, 2 * c:3 * c]
    lo = um[:, 3 * c:3 * c + LORA_W + LORA_A]
    lane = lax.broadcasted_iota(jnp.int32, lo.shape, 1)
    lo = jnp.where(lane < LORA_W, jnp.tanh(lo), lo).astype(BF16)
    g_lo = _sigmoid(um[:, 3 * c + LORA_W + LORA_A:]).astype(BF16)
    xw = -(w0_ref[...] + _dot(lo, w2_ref[...]))
    softplus = jnp.maximum(xw, 0.0) + jnp.log(1.0 + jnp.exp(-jnp.abs(xw)))
    lw_ref[0] = -jnp.exp(-softplus - 0.5)
    a = _sigmoid(a0_ref[...] + _dot(lo, a2_ref[...]))
    gate_ref[0] = _dot(g_lo, g2_ref[...])
    ones_bd = ones_ref[...]
    kk = k * kk_ref[...]
    norm = jnp.sqrt(_head_sum(kk * kk, ones_bd))
    kk = kk / jnp.maximum(norm, 1e-12)
    k = k * (1.0 + (a - 1.0) * ka_ref[...])
    r_ref[0] = r
    k_ref[0] = k
    v_ref[0] = v
    al_ref[0] = -kk
    be_ref[0] = kk * a
    bonus_ref[0] = _head_sum(r * k * rk_ref[...], ones_bd) * v


def _rwkv_pre(u, shift8, mu, w0, w2p, a0, a2p, g2, kkp, ka, rk, ones_bd, *, tt):
    bsz, t, _ = u.shape
    full = lambda a: pl.BlockSpec(a.shape, lambda i, j: (0,) * a.ndim)
    seq = pl.BlockSpec((1, tt, RWKV_C), lambda i, j: (i, j, 0))
    return pl.pallas_call(
        _rwkv_pre_kernel,
        grid=(bsz, t // tt),
        in_specs=[pl.BlockSpec((1, tt, RWKV_U), lambda i, j: (i, j, 0)),
                  pl.BlockSpec((1, SUBLANES, RWKV_U), lambda i, j: (i, 0, 0)),
                  full(mu), full(w0), full(w2p), full(a0), full(a2p), full(g2), full(kkp), full(ka),
                  full(rk), full(ones_bd)],
        out_specs=[seq] * 8,
        out_shape=[jax.ShapeDtypeStruct((bsz, t, RWKV_C), F32)] * 8,
        scratch_shapes=[pltpu.VMEM((SUBLANES + tt, RWKV_U), F32)],
        compiler_params=_cparams("parallel", "arbitrary"),
        name="rwkv_pre",
    )(u, shift8, mu, w0, w2p, a0, a2p, g2, kkp, ka, rk, ones_bd)


SCAN_PREC = dict(g=(1, 1), xs=(1, 1), gv=(1, 1), inv=(1, 1), u=(1, 1), y=(1, 1), st=(1, 1))


def _scan_kernel(r_ref, k_ref, v_ref, lw_ref, al_ref, be_ref, gate_ref, bonus_ref, s0_ref, g_ref, b_ref,
                 y_ref, s_ref, state):
    c = r_ref.shape[1]
    n = RWKV_N

    @pl.when(pl.program_id(1) == 0)
    def _():
        state[...] = s0_ref[0]

    row = lax.broadcasted_iota(jnp.int32, (c, c), 0)
    col = lax.broadcasted_iota(jnp.int32, (c, c), 1)
    lower = (row >= col).astype(BF16)
    eye = (row == col).astype(F32)

    lw = lw_ref[0]
    cum = _mm(lower, lw, NN, pa=1, pb=3)
    e_p = jnp.exp(cum)
    e_m = jnp.exp(-cum)
    cum_end = cum[c - 1:c, :]
    e_end = jnp.exp(cum_end - cum)
    p_end = jnp.exp(cum_end)
    rq_all = r_ref[0] * e_p
    aq_all = al_ref[0] * jnp.exp(cum - lw)
    kd_all = k_ref[0] * e_m
    bd_all = be_ref[0] * e_m
    kend_all = k_ref[0] * e_end
    bend_all = be_ref[0] * e_end
    v_all = v_ref[0]

    heads = range(RWKV_H)
    sls = [slice(h * n, (h + 1) * n) for h in heads]
    row2 = lax.broadcasted_iota(jnp.int32, (2 * c, 2 * c), 0)
    col2 = lax.broadcasted_iota(jnp.int32, (2 * c, 2 * c), 1) & (c - 1)
    mask2 = col2 < jnp.where(row2 < c, row2, row2 - c + 1)
    zeros_v = jnp.zeros((c, n), F32)
    pg, pxs, pgv, pinv, pu, py, pst = (SCAN_PREC[k] for k in ("g", "xs", "gv", "inv", "u", "y", "st"))
    xq = [jnp.concatenate([aq_all[:, sl], rq_all[:, sl]], axis=0) for sl in sls]
    wd = [jnp.concatenate([bd_all[:, sl], kd_all[:, sl]], axis=0) for sl in sls]
    zv = [jnp.concatenate([zeros_v, v_all[:, sl]], axis=0) for sl in sls]
    s_old = [state[h] for h in heads]
    gm = [jnp.where(mask2, _mm(xq[h], wd[h], NT, *pg), 0.0) for h in heads]
    xs = [_mm(xq[h], s_old[h], NT, *pxs) for h in heads]
    gv = [_mm(gm[h], zv[h], NN, *pgv) for h in heads]
    pw = [gm[h][:c, :c] for h in heads]
    inv = [eye + a for a in pw]
    span = 2
    while span < c:
        pw = [_mm(a, a, NN, *pinv) for a in pw]
        inv = [i + _mm(i, a, NN, *pinv) for i, a in zip(inv, pw)]
        span *= 2
    us = [_mm(inv[h], xs[h][:c] + gv[h][:c], NN, *pu) for h in heads]
    ys = [xs[h][c:] + gv[h][c:] + _mm(gm[h][c:, :c], us[h], NN, *py) for h in heads]
    for h in heads:
        sl = sls[h]
        uv = jnp.concatenate([us[h], v_all[:, sl]], axis=0)
        ends = jnp.concatenate([bend_all[:, sl], kend_all[:, sl]], axis=0)
        state[h] = s_old[h] * p_end[:, sl] + _mm(uv, ends, TN, *pst)
    for h in heads:
        sl = sls[h]
        y = ys[h]
        mu = jnp.mean(y, axis=-1, keepdims=True)
        yc = y - mu
        var = jnp.mean(yc * yc, axis=-1, keepdims=True)
        y = yc * lax.rsqrt(var + RWKV_LN_EPS) * g_ref[:, sl] + b_ref[:, sl]
        y_ref[0, :, sl] = ((y + bonus_ref[0, :, sl]) * gate_ref[0, :, sl]).astype(BF16)

    @pl.when(pl.program_id(1) == pl.num_programs(1) - 1)
    def _():
        s_ref[0] = state[...]


def _scan(r, k, v, lw, al, be, gate, bonus, s0, g, b, *, chunk):
    bsz, t, _ = r.shape
    seq = pl.BlockSpec((1, chunk, RWKV_C), lambda i, j: (i, j, 0))
    st = pl.BlockSpec((1, RWKV_H, RWKV_N, RWKV_N), lambda i, j: (i, 0, 0, 0))
    vec = pl.BlockSpec((1, RWKV_C), lambda i, j: (0, 0))
    return pl.pallas_call(
        _scan_kernel,
        grid=(bsz, t // chunk),
        in_specs=[seq] * 8 + [st, vec, vec],
        out_specs=[seq, st],
        out_shape=[jax.ShapeDtypeStruct((bsz, t, RWKV_C), BF16),
                   jax.ShapeDtypeStruct((bsz, RWKV_H, RWKV_N, RWKV_N), F32)],
        scratch_shapes=[pltpu.VMEM((RWKV_H, RWKV_N, RWKV_N), F32)],
        compiler_params=_cparams("parallel", "arbitrary"),
        name="rwkv_scan",
    )(r, k, v, lw, al, be, gate, bonus, s0, g, b)


ATT_SPAN = 128


def _softmax_pv(s, v2):
    m = jnp.max(s, axis=-1, keepdims=True)
    e = jnp.exp(s - m)
    den = jnp.sum(e, axis=-1, keepdims=True)
    o = _dot((e / den).astype(BF16), v2)
    return o, m + jnp.log(den)


HEADS_PER_TILE = LANES // HEAD_DIM


def _attn_prompt_kernel(q_ref, kc_ref, kp_ref, vc_ref, vp_ref, bias_ref, o_ref, lse_ref, *, dil):
    scale = HEAD_DIM ** -0.5

    def one_residue(res, carry):
        rows = pl.ds(res, ATT_SPAN, stride=dil) if dil > 1 else pl.ds(0, ATT_SPAN)
        q, kc, kp, vc, vp = (ref[0, rows, :] for ref in (q_ref, kc_ref, kp_ref, vc_ref, vp_ref))
        outs, lses = [], []
        for h in range(HEADS_PER_TILE):
            sl = slice(h * HEAD_DIM, (h + 1) * HEAD_DIM)
            qh = q[:, sl].astype(BF16)
            s = jnp.concatenate([_dot(qh, kp[:, sl].astype(BF16), NT),
                                 _dot(qh, kc[:, sl].astype(BF16), NT)], axis=1)
            s = s * scale + bias_ref[0, h]
            v2 = jnp.concatenate([vp[:, sl], vc[:, sl]], axis=0).astype(BF16)
            o, lse = _softmax_pv(s, v2)
            outs.append(o)
            lses.append(jnp.broadcast_to(lse, o.shape))
        o_ref[0, rows, :] = jnp.concatenate(outs, axis=1)
        lse_ref[0, rows, :] = jnp.concatenate(lses, axis=1)
        return carry

    if dil > 1:
        lax.fori_loop(0, dil, one_residue, 0)
    else:
        one_residue(0, 0)


def _attn_prompt(ua, bias, *, g, dil):
    bsz, s, _ = ua.shape
    rows = ATT_SPAN * dil
    tiles = ATT_W // LANES

    def col(which):
        return lambda b, hp, i: (b, i, (which * N_DIL + g) * tiles + hp)

    def col_prev(which):
        return lambda b, hp, i: (b, jnp.maximum(i - 1, 0), (which * N_DIL + g) * tiles + hp)

    blk = (1, rows, LANES)
    return pl.pallas_call(
        functools.partial(_attn_prompt_kernel, dil=dil),
        grid=(bsz, tiles, s // rows),
        in_specs=[pl.BlockSpec(blk, col(0)), pl.BlockSpec(blk, col(1)), pl.BlockSpec(blk, col_prev(1)),
                  pl.BlockSpec(blk, col(2)), pl.BlockSpec(blk, col_prev(2)),
                  pl.BlockSpec((1, HEADS_PER_TILE, ATT_SPAN, 2 * ATT_SPAN),
                               lambda b, hp, i: (jnp.minimum(i, 1), hp, 0, 0))],
        out_specs=[pl.BlockSpec(blk, lambda b, hp, i: (b, i, hp))] * 2,
        out_shape=[jax.ShapeDtypeStruct((bsz, s, ATT_W), F32)] * 2,
        compiler_params=_cparams("parallel", "parallel", "parallel"),
        name="attn_prompt",
    )(ua, ua, ua, ua, ua, bias)


def _attn_sample_kernel(u_ref, cache_ref, bo_ref, bn_ref, *rest, g):
    o_ref, lse_ref, new_ref = rest[-3:]
    t = u_ref.shape[1]
    rows = cache_ref.shape[-1]
    scale = HEAD_DIM ** -0.5
    base = g * ATT_W
    lane = lax.broadcasted_iota(jnp.int32, (HEAD_DIM, LANES), 1)
    place = (lax.broadcasted_iota(jnp.int32, (t, LANES), 1)
             == lax.broadcasted_iota(jnp.int32, (t, LANES), 0) + (LANES - t)).astype(BF16)
    for h in range(ATT_HG):
        sl = slice(h * HEAD_DIM, (h + 1) * HEAD_DIM)
        col = base + h * HEAD_DIM
        q = u_ref[0, :, col:col + HEAD_DIM].astype(BF16)
        k_new = u_ref[0, :, N_DIL * ATT_W + col:N_DIL * ATT_W + col + HEAD_DIM]
        v_new = u_ref[0, :, 2 * N_DIL * ATT_W + col:2 * N_DIL * ATT_W + col + HEAD_DIM]
        k_t = cache_ref[0, 0, 0, h]
        v_t = cache_ref[0, 0, 1, h]
        s_old = _dot(q, k_t.astype(BF16)) * scale + bo_ref[h]
        s_new = _dot(q, k_new.astype(BF16), NT) * scale + bn_ref[h]
        m = jnp.maximum(jnp.max(s_old, axis=-1, keepdims=True), jnp.max(s_new, axis=-1, keepdims=True))
        e_old = jnp.exp(s_old - m)
        e_new = jnp.exp(s_new - m)
        den = jnp.sum(e_old, axis=-1, keepdims=True) + jnp.sum(e_new, axis=-1, keepdims=True)
        o = _dot((e_old / den).astype(BF16), v_t.astype(BF16), NT)
        p_new = e_new / den
        for j in range(t):
            o = o + p_new[:, j:j + 1] * v_new[j:j + 1, :]
        o_ref[0, :, sl] = o
        lse_ref[0, :, sl] = jnp.broadcast_to(m + jnp.log(den), o.shape)
        for kv, (old, new) in enumerate(((k_t, k_new), (v_t, v_new))):
            moved = pltpu.roll(old, rows - t, axis=1)
            tail = jnp.where(lane >= LANES - t, _mm(new, place, TN, 3, 1), moved[:, rows - LANES:])
            if rows > LANES:
                new_ref[0, 0, kv, h, :, 0:rows - LANES] = moved[:, 0:rows - LANES]
            new_ref[0, 0, kv, h, :, rows - LANES:rows] = tail


def _attn_sample(ua, cache_t, prev, bias_old, bias_new, *, g, layer):
    bsz, t, _ = ua.shape
    out = pl.BlockSpec((1, t, ATT_W), lambda b: (b, 0, 0))
    cb = pl.BlockSpec((1, 1) + cache_t.shape[2:], lambda b: (layer, b, 0, 0, 0, 0))
    full = lambda a: pl.BlockSpec(a.shape, lambda b: (0,) * a.ndim)
    in_specs = [pl.BlockSpec((1, t, ATT_QKV), lambda b: (b, 0, 0)), cb, full(bias_old), full(bias_new)]
    args = [ua, cache_t, bias_old, bias_new]
    aliases = {}
    if prev is not None:
        in_specs.append(pl.BlockSpec(memory_space=pl.ANY))
        args.append(prev)
        aliases = {len(args) - 1: 2}
    return pl.pallas_call(
        functools.partial(_attn_sample_kernel, g=g),
        grid=(bsz,),
        in_specs=in_specs,
        out_specs=[out, out, cb],
        out_shape=[jax.ShapeDtypeStruct((bsz, t, ATT_W), F32)] * 2
                  + [jax.ShapeDtypeStruct(cache_t.shape, F32)],
        input_output_aliases=aliases,
        compiler_params=_cparams("parallel"),
        name="attn_sample",
    )(*args)


def _rel_bucket(dist):
    max_exact = N_BUCKETS // 2
    large = max_exact + (jnp.log(jnp.maximum(dist, 1).astype(F32) / max_exact)
                         / math.log(MAX_DISTANCE / max_exact) * (N_BUCKETS - max_exact)).astype(jnp.int32)
    return jnp.where(dist < max_exact, dist, jnp.minimum(large, N_BUCKETS - 1))


def _prompt_bias(tab, dil):
    span = ATT_SPAN
    qi = np.arange(span)[:, None]
    ki = np.arange(2 * span)[None, :]
    j = span + qi - ki
    band = (j >= 0) & (j <= span)
    bias = jnp.transpose(tab[_rel_bucket(jnp.asarray(dil * np.clip(j, 0, span)))], (2, 0, 1))
    rest = jnp.where(band[None], bias, NEG_INF)
    first = jnp.where((band & (ki >= span))[None], bias, NEG_INF)
    return jnp.stack([first, rest])


def _sample_bias(tab, dil, rows, t):
    rho = np.arange(rows + t)[None, :]
    d = rows + np.arange(t)[:, None] - rho
    valid = (d >= 0) & (d % dil == 0) & (d // dil <= ATT_SPAN)
    bias = jnp.transpose(tab[_rel_bucket(jnp.asarray(np.clip(d, 0, None)))], (2, 0, 1))
    bias = jnp.where(valid[None], bias, NEG_INF)
    return bias[:, :, :rows], bias[:, :, rows:]


def _merge_kernel(x_ref, yc_ref, yr_ref, o0_ref, o1_ref, o2_ref, l0_ref, l1_ref, l2_ref, gates_ref,
                  wc_ref, wr_ref, wa_ref, wo_ref, g_ref, b_ref, rt_ref, h_ref, gate_ref, eid_ref, *, alpha):
    l0, l1, l2 = l0_ref[...], l1_ref[...], l2_ref[...]
    m = jnp.maximum(jnp.maximum(l0, l1), l2)
    e0, e1, e2 = jnp.exp(l0 - m), jnp.exp(l1 - m), jnp.exp(l2 - m)
    den = e0 + e1 + e2
    o = o0_ref[...] * (e0 / den) + o1_ref[...] * (e1 / den) + o2_ref[...] * (e2 / den)
    d = D_MODEL
    merged = (gates_ref[:, 0:d] * _dot(yc_ref[...], wc_ref[...])
              + gates_ref[:, d:2 * d] * _dot(yr_ref[...], wr_ref[...])
              + gates_ref[:, 2 * d:3 * d] * _dot(o.astype(BF16), wa_ref[...]))
    pre = alpha * x_ref[...] + _dot(merged.astype(BF16), wo_ref[...])
    h = _layer_norm(pre, g_ref[...], b_ref[...], LN_EPS)
    h_ref[...] = h
    logits = _mm(h, rt_ref[...], NN, 3, 3)
    lane = lax.broadcasted_iota(jnp.int32, logits.shape, 1)
    big = jnp.int32(ROUTER_PAD)
    is_grp = lane < N_GROUPS
    lg = jnp.where(is_grp, logits, NEG_INF)
    m_g = jnp.max(lg, axis=-1, keepdims=True)
    grp = jnp.min(jnp.where(lg == m_g, lane, big), axis=-1, keepdims=True)
    p_grp = 1.0 / jnp.sum(jnp.where(is_grp, jnp.exp(logits - m_g), 0.0), axis=-1, keepdims=True)
    lo_lane = N_GROUPS + grp * EXPERTS_PER_GROUP
    in_grp = (lane >= lo_lane) & (lane < lo_lane + EXPERTS_PER_GROUP)
    le = jnp.where(in_grp, logits, NEG_INF)
    v1 = jnp.max(le, axis=-1, keepdims=True)
    i1 = jnp.min(jnp.where(le == v1, lane, big), axis=-1, keepdims=True)
    le = jnp.where(lane == i1, NEG_INF, le)
    v2 = jnp.max(le, axis=-1, keepdims=True)
    i2 = jnp.min(jnp.where(le == v2, lane, big), axis=-1, keepdims=True)
    e2 = jnp.exp(v2 - v1)
    den = 1.0 + e2
    gate_ref[...] = jnp.where(lane == 0, p_grp * (1.0 / den), jnp.where(lane == 1, p_grp * (e2 / den), 0.0))
    eid_ref[...] = jnp.where(lane == 0, i1 - N_GROUPS, jnp.where(lane == 1, i2 - N_GROUPS, 0))


def _merge(x, yc, yr, os_, ls_, gates, wc, wr, wa, wo, g, b, rt, *, alpha, tm):
    n = x.shape[0]
    row = lambda c: pl.BlockSpec((tm, c), lambda i: (i, 0))
    full = lambda a: pl.BlockSpec(a.shape, lambda i: (0,) * a.ndim)
    return pl.pallas_call(
        functools.partial(_merge_kernel, alpha=alpha),
        grid=(n // tm,),
        in_specs=[row(D_MODEL), row(CONV_DIM), row(RWKV_C)] + [row(ATT_W)] * 6 + [row(N_BRANCH * D_MODEL)]
                 + [full(a) for a in (wc, wr, wa, wo, g, b, rt)],
        out_specs=[row(D_MODEL), row(ROUTER_PAD), row(ROUTER_PAD)],
        out_shape=[jax.ShapeDtypeStruct((n, D_MODEL), F32), jax.ShapeDtypeStruct((n, ROUTER_PAD), F32),
                   jax.ShapeDtypeStruct((n, ROUTER_PAD), jnp.int32)],
        compiler_params=_cparams("parallel"),
        name="merge",
    )(x, yc, yr, *os_, *ls_, gates, wc, wr, wa, wo, g, b, rt)


DMA_UNROLL = 8


def _expert_kernel(be_ref, off_ref, cnt_ref, order_ref, h_hbm, wg_ref, wu_ref, wd_ref, out_hbm,
                   xbuf, ybuf, gsem, ssem):
    i = pl.program_id(0)
    nb = pl.num_programs(0)
    slot = i & 1
    n_out = out_hbm.shape[0] - 2 * MOE_BLOCK

    def gather_start(blk, s):
        base = off_ref[blk]

        def body(r, carry):
            a = order_ref[base + r]
            pltpu.make_async_copy(h_hbm.at[pl.ds(a // TOP_K, 1)], xbuf.at[s, pl.ds(r, 1)], gsem.at[s]).start()
            return carry

        lax.fori_loop(0, MOE_BLOCK, body, 0, unroll=DMA_UNROLL)

    def scatter_start(blk, s):
        base = off_ref[blk]
        cnt = cnt_ref[blk]

        def body(r, carry):
            dst = jnp.where(r < cnt, order_ref[base + r], n_out + s * MOE_BLOCK + r)
            pltpu.make_async_copy(ybuf.at[s, pl.ds(r, 1)], out_hbm.at[pl.ds(dst, 1)], ssem.at[s]).start()
            return carry

        lax.fori_loop(0, MOE_BLOCK, body, 0, unroll=DMA_UNROLL)

    def gather_wait(s):
        def body(r, carry):
            pltpu.make_async_copy(h_hbm.at[pl.ds(0, 1)], xbuf.at[s, pl.ds(r, 1)], gsem.at[s]).wait()
            return carry

        lax.fori_loop(0, MOE_BLOCK, body, 0, unroll=DMA_UNROLL)

    def scatter_wait(s):
        def body(r, carry):
            pltpu.make_async_copy(ybuf.at[s, pl.ds(r, 1)], out_hbm.at[pl.ds(0, 1)], ssem.at[s]).wait()
            return carry

        lax.fori_loop(0, MOE_BLOCK, body, 0, unroll=DMA_UNROLL)

    active = cnt_ref[i] > 0

    @pl.when(i == 0)
    def _():
        ybuf[...] = jnp.zeros(ybuf.shape, F32)
        for s in range(2):
            spare = pltpu.make_async_copy(ybuf.at[s], out_hbm.at[pl.ds(n_out + s * MOE_BLOCK, MOE_BLOCK)], ssem.at[s])
            spare.start()
            spare.wait()

    @pl.when((i == 0) & active)
    def _():
        gather_start(0, 0)

    @pl.when((i >= 2) & (cnt_ref[jnp.maximum(i - 2, 0)] > 0))
    def _():
        scatter_wait(slot)

    @pl.when(active)
    def _():
        nxt = jnp.minimum(i + 1, nb - 1)

        @pl.when((i + 1 < nb) & (cnt_ref[nxt] > 0))
        def _():
            gather_start(nxt, 1 - slot)

        gather_wait(slot)
        xb = xbuf[slot].astype(BF16)
        gate = _dot(xb, wg_ref[0])
        act = (gate * _sigmoid(gate) * _dot(xb, wu_ref[0])).astype(BF16)
        ybuf[slot] = _dot(act, wd_ref[0])
        scatter_start(i, slot)

    @pl.when(i == nb - 1)
    def _():
        @pl.when((nb >= 2) & (cnt_ref[jnp.maximum(nb - 2, 0)] > 0))
        def _():
            scatter_wait(1 - slot)

        @pl.when(active)
        def _():
            scatter_wait(slot)


def _experts(h, block_e, block_off, block_cnt, order, wg, wu, wd):
    n = h.shape[0]
    n_blocks = block_e.shape[0]
    wmap = lambda i, be, off, cnt, order: (be[i], 0, 0)
    grid_spec = pltpu.PrefetchScalarGridSpec(
        num_scalar_prefetch=4,
        grid=(n_blocks,),
        in_specs=[pl.BlockSpec(memory_space=pl.ANY),
                  pl.BlockSpec((1, D_MODEL, D_EXPERT), wmap),
                  pl.BlockSpec((1, D_MODEL, D_EXPERT), wmap),
                  pl.BlockSpec((1, D_EXPERT, D_MODEL), wmap)],
        out_specs=pl.BlockSpec(memory_space=pl.ANY),
        scratch_shapes=[pltpu.VMEM((2, MOE_BLOCK, D_MODEL), F32), pltpu.VMEM((2, MOE_BLOCK, D_MODEL), F32),
                        pltpu.SemaphoreType.DMA((2,)), pltpu.SemaphoreType.DMA((2,))],
    )
    out = pl.pallas_call(
        _expert_kernel,
        grid_spec=grid_spec,
        out_shape=jax.ShapeDtypeStruct((TOP_K * n + 2 * MOE_BLOCK, D_MODEL), F32),
        compiler_params=_cparams("arbitrary"),
        name="experts",
    )(block_e, block_off, block_cnt, order, h, wg, wu, wd)
    return out.reshape(n + 2 * MOE_BLOCK // TOP_K, TOP_K * D_MODEL)


def _route(eid):
    n = eid.shape[0]
    a = n * TOP_K
    n_blocks = -(-a // MOE_BLOCK) + N_EXPERTS
    skey = jnp.sort(eid.reshape(-1) * a + jnp.arange(a, dtype=jnp.int32))
    order = skey % a
    start = jnp.searchsorted(skey, jnp.arange(N_EXPERTS + 1, dtype=jnp.int32) * a).astype(jnp.int32)
    counts = start[1:] - start[:-1]
    padded = (counts + MOE_BLOCK - 1) // MOE_BLOCK * MOE_BLOCK
    pend = jnp.cumsum(padded)
    pstart = pend - padded
    blk_start = jnp.arange(n_blocks, dtype=jnp.int32) * MOE_BLOCK
    block_e = jnp.minimum(jnp.searchsorted(pend, blk_start, side="right"), N_EXPERTS - 1).astype(jnp.int32)
    within = blk_start - pstart[block_e]
    block_cnt = jnp.where(blk_start < pend[-1], jnp.clip(counts[block_e] - within, 0, MOE_BLOCK), 0)
    block_off = jnp.where(block_cnt > 0, start[block_e] + within, 0)
    order = jnp.concatenate([order, jnp.zeros((MOE_BLOCK,), jnp.int32)])
    return block_e, block_off.astype(jnp.int32), block_cnt.astype(jnp.int32), order


def _ln2_kernel(h_ref, f_ref, gate_ref, g_ref, b_ref, y_ref, yb_ref, *, alpha):
    f = f_ref[:, 0:D_MODEL] * gate_ref[:, 0:1] + f_ref[:, D_MODEL:2 * D_MODEL] * gate_ref[:, 1:2]
    y = _layer_norm(alpha * h_ref[...] + f, g_ref[...], b_ref[...], LN_EPS)
    y_ref[...] = y
    yb_ref[...] = y.astype(BF16)


def _ln2(h, f, gate, g, b, *, alpha, tm):
    n = h.shape[0]
    vec = pl.BlockSpec((1, D_MODEL), lambda i: (0, 0))
    return pl.pallas_call(
        functools.partial(_ln2_kernel, alpha=alpha),
        grid=(n // tm,),
        in_specs=[pl.BlockSpec((tm, D_MODEL), lambda i: (i, 0)),
                  pl.BlockSpec((tm, TOP_K * D_MODEL), lambda i: (i, 0)),
                  pl.BlockSpec((tm, ROUTER_PAD), lambda i: (i, 0)), vec, vec],
        out_specs=[pl.BlockSpec((tm, D_MODEL), lambda i: (i, 0))] * 2,
        out_shape=[jax.ShapeDtypeStruct((n, D_MODEL), F32), jax.ShapeDtypeStruct((n, D_MODEL), BF16)],
        compiler_params=_cparams("parallel"),
        name="ln2",
    )(h, f, gate, g, b)


def _row(v):
    return v.reshape(1, -1)


def _layer(x, xb, p, conv_state, shift, wkv, caches, *, alpha, prompt, layer):
    bsz, t, d = x.shape
    n = bsz * t
    tm = 512 if prompt else n
    xb2 = xb.reshape(n, d)
    zero_b = jnp.zeros((1, N_BRANCH * D_MODEL), F32)
    o1 = 2 * CONV_DIM
    o2 = o1 + RWKV_U
    o3 = o2 + ATT_QKV

    z = _proj(xb2, p["w_in"][:, :o1], zero_b[:, :o1], epilogue="glu", tn=o1, tm=tm)
    ur = _proj(xb2, p["w_in"][:, o1:o2], zero_b[:, :RWKV_U], epilogue="none", tn=RWKV_U, tm=tm)
    ua = _proj(xb2, p["w_in"][:, o2:o3], zero_b[:, :ATT_QKV], epilogue="none", tn=ATT_QKV // 2, tm=tm)
    gates = _proj(xb2, p["w_in"][:, o3:], p["b_gate"], epilogue="sigmoid_bias", tn=D_MODEL, tm=tm)
    z = z.reshape(bsz, t, CONV_DIM)
    ur = ur.reshape(bsz, t, RWKV_U)
    ua = ua.reshape(bsz, t, ATT_QKV)

    hist = jnp.pad(conv_state, ((0, 0), (CONV_HIST - (CONV_K - 1), 0), (0, 0)))
    yc = _conv(z, hist, p["conv_dw"], p["conv_dw_b"], p["conv_ln_g"], p["conv_ln_b"], tt=256 if prompt else t)
    conv_new = jnp.concatenate([conv_state, z], axis=1)[:, -(CONV_K - 1):]

    shift8 = jnp.pad(shift[:, None, :], ((0, 0), (SUBLANES - 1, 0), (0, 0)))
    pre = _rwkv_pre(ur, shift8, p["rwkv_mu"], p["rwkv_w0"], p["rwkv_w2p"], p["rwkv_a0"], p["rwkv_a2p"],
                    p["rwkv_g2"], p["rwkv_kk"], p["rwkv_ka"], p["rwkv_rk"], p["ones_bd"],
                    tt=256 if prompt else t)
    chunk = 64 if prompt else SUBLANES
    if t % chunk:
        pre = [jnp.pad(a, ((0, 0), (0, chunk - t % chunk), (0, 0))) for a in pre]
    yr, wkv_new = _scan(*pre, wkv, p["rwkv_ln_g"], p["rwkv_ln_b"], chunk=chunk)
    yr = yr[:, :t]
    shift_new = ur[:, -1]

    os_, ls_, att_new = [], [], []
    for g, (window, dil) in enumerate(DILATIONS):
        if prompt:
            o, lse = _attn_prompt(ua, p["att_bias"][g], g=g, dil=dil)
            rows = min(window, t)
            kv = ua[:, t - rows:].reshape(bsz, rows, 3, N_DIL, ATT_HG, HEAD_DIM)
            att_new.append(jnp.stack([kv[:, :, 1, g], kv[:, :, 2, g]], axis=1))
        else:
            cache_t, prev = caches[g]
            o, lse, new = _attn_sample(ua, cache_t, prev, *p["att_bias"][g], g=g, layer=layer)
            att_new.append(new)
        os_.append(o.reshape(n, ATT_W))
        ls_.append(lse.reshape(n, ATT_W))

    h, gate, eid = _merge(x.reshape(n, d), yc.reshape(n, CONV_DIM), yr.reshape(n, RWKV_C), os_, ls_, gates,
                          p["conv_proj"], p["rwkv_proj"], p["attn_proj"], p["w_out"], p["ln1_g"], p["ln1_b"],
                          p["router"], alpha=alpha, tm=256 if prompt else n)

    f = _experts(h, *_route(eid[:, :TOP_K]), p["moe_w_gate"], p["moe_w_up"], p["moe_w_down"])
    y, yb = _ln2(h, f, gate, p["ln2_g"], p["ln2_b"], alpha=alpha, tm=tm)
    return y.reshape(bsz, t, d), yb.reshape(bsz, t, d), conv_new, shift_new, wkv_new, att_new


def kernel(x_prompt, x_sample, state_conv, state_shift, state_wkv, cache_attn_w128, cache_attn_w512, cache_attn_w2048, rel_bias, w_in, b_gate, conv_dw, conv_dw_b, conv_ln_g, conv_ln_b, conv_proj, rwkv_mu, rwkv_w0, rwkv_w2, rwkv_a0, rwkv_a2, rwkv_g2, rwkv_kk, rwkv_ka, rwkv_rk, rwkv_ln_g, rwkv_ln_b, rwkv_proj, attn_proj, w_out, ln1_g, ln1_b, router_group, router_expert, moe_w_gate, moe_w_up, moe_w_down, ln2_g, ln2_b):
    depth = w_in.shape[0]
    alpha = (2 * depth) ** 0.25
    bp, tp, _ = x_prompt.shape
    ts = x_sample.shape[1]
    caches_t = [jnp.transpose(c, (0, 1, 2, 4, 5, 3)) for c in (cache_attn_w128, cache_attn_w512, cache_attn_w2048)]
    new_caches = [None] * N_DIL
    head = np.arange(RWKV_C) // RWKV_N
    ones_bd = jnp.asarray(head[:, None] == head[None, :], BF16)
    tabs = [rel_bias[:, g * ATT_HG:(g + 1) * ATT_HG] for g in range(N_DIL)]
    bias_p = [_prompt_bias(tabs[g], dil) for g, (_, dil) in enumerate(DILATIONS)]
    bias_s = [_sample_bias(tabs[g], dil, caches_t[g].shape[-1], ts) for g, (_, dil) in enumerate(DILATIONS)]

    xp, xs = x_prompt, x_sample
    xpb, xsb = xp.astype(BF16), xs.astype(BF16)
    outs_p, outs_s = [], []
    for l in range(depth):
        zeros_lora = jnp.zeros((LORA_W, RWKV_C), F32)
        router = jnp.concatenate(
            [router_group[l], router_expert[l],
             jnp.zeros((D_MODEL, ROUTER_PAD - N_GROUPS - N_EXPERTS), F32)], axis=1)
        p = dict(
            w_in=w_in[l].astype(BF16), b_gate=_row(b_gate[l]),
            conv_dw=conv_dw[l], conv_dw_b=_row(conv_dw_b[l]), conv_ln_g=_row(conv_ln_g[l]),
            conv_ln_b=_row(conv_ln_b[l]), conv_proj=conv_proj[l].astype(BF16),
            rwkv_mu=_row(rwkv_mu[l]), rwkv_w0=_row(rwkv_w0[l]),
            rwkv_w2p=jnp.concatenate([rwkv_w2[l], zeros_lora], axis=0).astype(BF16),
            rwkv_a0=_row(rwkv_a0[l]),
            rwkv_a2p=jnp.concatenate([zeros_lora, rwkv_a2[l]], axis=0).astype(BF16),
            rwkv_g2=rwkv_g2[l].astype(BF16), rwkv_kk=_row(rwkv_kk[l]), rwkv_ka=_row(rwkv_ka[l]),
            rwkv_rk=_row(rwkv_rk[l]), rwkv_ln_g=_row(rwkv_ln_g[l]), rwkv_ln_b=_row(rwkv_ln_b[l]),
            rwkv_proj=rwkv_proj[l].astype(BF16), attn_proj=attn_proj[l].astype(BF16),
            w_out=w_out[l].astype(BF16), ln1_g=_row(ln1_g[l]), ln1_b=_row(ln1_b[l]), router=router,
            moe_w_gate=moe_w_gate[l].astype(BF16), moe_w_up=moe_w_up[l].astype(BF16),
            moe_w_down=moe_w_down[l].astype(BF16), ln2_g=_row(ln2_g[l]), ln2_b=_row(ln2_b[l]),
            ones_bd=ones_bd)
        xp, xpb, *new_p = _layer(
            xp, xpb, dict(p, att_bias=bias_p),
            jnp.zeros((bp, CONV_K - 1, CONV_DIM), F32), jnp.zeros((bp, RWKV_U), F32),
            jnp.zeros((bp, RWKV_H, RWKV_N, RWKV_N), F32), None, alpha=alpha, prompt=True, layer=l)
        xs, xsb, *new_s = _layer(
            xs, xsb, dict(p, att_bias=bias_s), state_conv[l], state_shift[l], state_wkv[l],
            list(zip(caches_t, new_caches)), alpha=alpha, prompt=False, layer=l)
        new_caches = new_s[3]
        outs_p.append(new_p)
        outs_s.append(new_s)

    def stack(outs, i):
        return jnp.stack([o[i] for o in outs])

    att_p = [jnp.stack([o[3][g] for o in outs_p]) for g in range(N_DIL)]
    att_s = [jnp.transpose(c, (0, 1, 2, 5, 3, 4)) for c in new_caches]
    return (xp, xs, stack(outs_p, 0), stack(outs_s, 0), stack(outs_p, 1), stack(outs_s, 1),
            stack(outs_p, 2), stack(outs_s, 2),
            att_p[0], att_s[0], att_p[1], att_s[1], att_p[2], att_s[2])
```

```python
import functools
import math

import jax
import jax.numpy as jnp
import numpy as np
from jax import lax
from jax.experimental import pallas as pl
from jax.experimental.pallas import tpu as pltpu

F32 = jnp.float32
BF16 = jnp.bfloat16

D_MODEL = 1024
CONV_DIM = D_MODEL // 2
CONV_K = 31
RWKV_N = 64
RWKV_H = D_MODEL // 128
RWKV_C = RWKV_H * RWKV_N
LORA_W, LORA_A, LORA_G = 64, 64, 128
RWKV_U = 3 * RWKV_C + LORA_W + LORA_A + LORA_G
RWKV_LN_EPS = 64e-5
HEAD_DIM = 64
ATT_HG = 4
ATT_W = ATT_HG * HEAD_DIM
DILATIONS = ((128, 1), (512, 4), (2048, 16))
N_DIL = len(DILATIONS)
ATT_QKV = 3 * N_DIL * ATT_W
N_BUCKETS = 32
MAX_DISTANCE = 2048
N_BRANCH = 3
N_GROUPS = 4
EXPERTS_PER_GROUP = 8
N_EXPERTS = N_GROUPS * EXPERTS_PER_GROUP
TOP_K = 2
D_EXPERT = D_MODEL // 2
MOE_BLOCK = 128
LN_EPS = 1e-5
NEG_INF = -1e30
ROUTER_PAD = 128

LANES = 128
SUBLANES = 8
VMEM_LIMIT = 48 * 1024 * 1024

NN = (((1,), (0,)), ((), ()))
NT = (((1,), (1,)), ((), ()))
TN = (((0,), (0,)), ((), ()))


def _cparams(*sem):
    return pltpu.CompilerParams(dimension_semantics=sem, vmem_limit_bytes=VMEM_LIMIT)


def _dot(a, b, dims=NN):
    return lax.dot_general(a, b, dims, preferred_element_type=F32)


def _pieces(x, n):
    if x.dtype == BF16:
        return [x]
    out, r = [], x
    for i in range(n):
        p = r.astype(BF16)
        out.append(p)
        if i + 1 < n:
            r = r - p.astype(F32)
    return out


def _mm(a, b, dims=NN, pa=1, pb=1):
    ap, bp = _pieces(a, pa), _pieces(b, pb)
    order = max(len(ap), len(bp))
    acc = None
    for i in reversed(range(len(ap))):
        for j in reversed(range(len(bp))):
            if i + j < order:
                t = _dot(ap[i], bp[j], dims)
                acc = t if acc is None else acc + t
    return acc


def _sigmoid(x):
    return 1.0 / (1.0 + jnp.exp(-x))


def _layer_norm(x, g, b, eps):
    mu = jnp.mean(x, axis=-1, keepdims=True)
    xc = x - mu
    var = jnp.mean(xc * xc, axis=-1, keepdims=True)
    return xc * lax.rsqrt(var + eps) * g + b


def _proj_kernel(x_ref, w_ref, b_ref, o_ref, *, epilogue):
    acc = _dot(x_ref[...], w_ref[...])
    if epilogue == "glu":
        half = acc.shape[1] // 2
        o_ref[...] = acc[:, :half] * _sigmoid(acc[:, half:])
    elif epilogue == "sigmoid_bias":
        o_ref[...] = _sigmoid(acc + b_ref[...])
    else:
        o_ref[...] = acc


def _proj(x, w_all, b_all, *, layer, col0, n_cols, epilogue, tn, tm):
    n, k = x.shape
    out_cols = n_cols // 2 if epilogue == "glu" else n_cols
    out_tn = tn // 2 if epilogue == "glu" else tn
    c0 = col0 // tn
    return pl.pallas_call(
        functools.partial(_proj_kernel, epilogue=epilogue),
        grid=(n // tm, n_cols // tn),
        in_specs=[pl.BlockSpec((tm, k), lambda i, j: (i, 0)),
                  pl.BlockSpec((None, k, tn), lambda i, j: (layer, 0, c0 + j)),
                  pl.BlockSpec((None, 1, tn), lambda i, j: (layer, 0, c0 + j))],
        out_specs=pl.BlockSpec((tm, out_tn), lambda i, j: (i, j)),
        out_shape=jax.ShapeDtypeStruct((n, out_cols), F32),
        compiler_params=_cparams("parallel", "parallel"),
        name="in_proj_" + epilogue,
    )(x, w_all, b_all)


CONV_HIST = 32
CONV_ROWS = 64


def _conv_kernel(z_ref, hist_ref, dw_ref, dwb_ref, g_ref, b_ref, o_ref, zbuf):
    tt = z_ref.shape[1]

    @pl.when(pl.program_id(1) == 0)
    def _():
        zbuf[0:CONV_HIST, :] = hist_ref[0]

    @pl.when(pl.program_id(1) != 0)
    def _():
        zbuf[0:CONV_HIST, :] = zbuf[tt:tt + CONV_HIST, :]

    zbuf[CONV_HIST:CONV_HIST + tt, :] = z_ref[0]
    pad = CONV_HIST - (CONV_K - 1)
    rb = min(tt, CONV_ROWS)
    for r0 in range(0, tt, rb):
        acc = jnp.zeros((rb, CONV_DIM), F32)
        for k in range(CONV_K):
            acc = acc + zbuf[r0 + pad + k:r0 + pad + k + rb, :] * dw_ref[k:k + 1, :]
        y = _layer_norm(acc + dwb_ref[...], g_ref[...], b_ref[...], LN_EPS)
        o_ref[0, r0:r0 + rb, :] = (y * _sigmoid(y)).astype(BF16)


def _conv(z, hist, dw, dwb, g, b, *, tt):
    bsz, t, _ = z.shape
    vec = pl.BlockSpec((1, CONV_DIM), lambda i, j: (0, 0))
    return pl.pallas_call(
        _conv_kernel,
        grid=(bsz, t // tt),
        in_specs=[pl.BlockSpec((1, tt, CONV_DIM), lambda i, j: (i, j, 0)),
                  pl.BlockSpec((1, CONV_HIST, CONV_DIM), lambda i, j: (i, 0, 0)),
                  pl.BlockSpec((CONV_K, CONV_DIM), lambda i, j: (0, 0)),
                  vec, vec, vec],
        out_specs=pl.BlockSpec((1, tt, CONV_DIM), lambda i, j: (i, j, 0)),
        out_shape=jax.ShapeDtypeStruct((bsz, t, CONV_DIM), BF16),
        scratch_shapes=[pltpu.VMEM((CONV_HIST + tt, CONV_DIM), F32)],
        compiler_params=_cparams("parallel", "arbitrary"),
        name="conv",
    )(z, hist, dw, dwb, g, b)


def _head_sum(x, ones_bd):
    return _mm(x, ones_bd, NN, pa=3, pb=1)


def _rwkv_pre_kernel(u_ref, shift_ref, mu_ref, w0_ref, w2_ref, a0_ref, a2_ref, g2_ref, kk_ref, ka_ref,
                     rk_ref, ones_ref,
                     r_ref, k_ref, v_ref, lw_ref, al_ref, be_ref, gate_ref, bonus_ref, ubuf):
    tt = u_ref.shape[1]
    c = RWKV_C

    @pl.when(pl.program_id(1) == 0)
    def _():
        ubuf[0:SUBLANES, :] = shift_ref[0]

    @pl.when(pl.program_id(1) != 0)
    def _():
        ubuf[0:SUBLANES, :] = ubuf[tt:tt + SUBLANES, :]

    u = u_ref[0]
    ubuf[SUBLANES:SUBLANES + tt, :] = u
    u_prev = ubuf[SUBLANES - 1:SUBLANES - 1 + tt, :]
    um = u + (u_prev - u) * mu_ref[...]
    r, k, v = um[:, 0:c], um[:, c:2 * c], um[:, 2 * c:3 * c]
    lo = um[:, 3 * c:3 * c + LORA_W + LORA_A]
    lane = lax.broadcasted_iota(jnp.int32, lo.shape, 1)
    lo = jnp.where(lane < LORA_W, jnp.tanh(lo), lo).astype(BF16)
    g_lo = _sigmoid(um[:, 3 * c + LORA_W + LORA_A:]).astype(BF16)
    xw = -(w0_ref[...] + _dot(lo, w2_ref[...]))
    softplus = jnp.maximum(xw, 0.0) + jnp.log(1.0 + jnp.exp(-jnp.abs(xw)))
    lw_ref[0] = -jnp.exp(-softplus - 0.5)
    a = _sigmoid(a0_ref[...] + _dot(lo, a2_ref[...]))
    gate_ref[0] = _dot(g_lo, g2_ref[...])
    ones_bd = ones_ref[...]
    kk = k * kk_ref[...]
    norm = jnp.sqrt(_head_sum(kk * kk, ones_bd))
    kk = kk / jnp.maximum(norm, 1e-12)
    k = k * (1.0 + (a - 1.0) * ka_ref[...])
    r_ref[0] = r
    k_ref[0] = k
    v_ref[0] = v
    al_ref[0] = -kk
    be_ref[0] = kk * a
    bonus_ref[0] = _head_sum(r * k * rk_ref[...], ones_bd) * v


def _rwkv_pre(u, shift8, mu, w0, w2p, a0, a2p, g2, kkp, ka, rk, ones_bd, *, tt):
    bsz, t, _ = u.shape
    full = lambda a: pl.BlockSpec(a.shape, lambda i, j: (0,) * a.ndim)
    seq = pl.BlockSpec((1, tt, RWKV_C), lambda i, j: (i, j, 0))
    return pl.pallas_call(
        _rwkv_pre_kernel,
        grid=(bsz, t // tt),
        in_specs=[pl.BlockSpec((1, tt, RWKV_U), lambda i, j: (i, j, 0)),
                  pl.BlockSpec((1, SUBLANES, RWKV_U), lambda i, j: (i, 0, 0)),
                  full(mu), full(w0), full(w2p), full(a0), full(a2p), full(g2), full(kkp), full(ka),
                  full(rk), full(ones_bd)],
        out_specs=[seq] * 8,
        out_shape=[jax.ShapeDtypeStruct((bsz, t, RWKV_C), F32)] * 8,
        scratch_shapes=[pltpu.VMEM((SUBLANES + tt, RWKV_U), F32)],
        compiler_params=_cparams("parallel", "arbitrary"),
        name="rwkv_pre",
    )(u, shift8, mu, w0, w2p, a0, a2p, g2, kkp, ka, rk, ones_bd)


SCAN_PREC = dict(g=(1, 1), xs=(1, 1), gv=(1, 1), inv=(1, 1), u=(1, 1), y=(1, 1), st=(1, 1))


def _scan_kernel(r_ref, k_ref, v_ref, lw_ref, al_ref, be_ref, gate_ref, bonus_ref, s0_ref, g_ref, b_ref,
                 y_ref, s_ref, state):
    c = r_ref.shape[1]
    n = RWKV_N

    @pl.when(pl.program_id(1) == 0)
    def _():
        state[...] = s0_ref[0]

    row = lax.broadcasted_iota(jnp.int32, (c, c), 0)
    col = lax.broadcasted_iota(jnp.int32, (c, c), 1)
    lower = (row >= col).astype(BF16)
    eye = (row == col).astype(F32)

    lw = lw_ref[0]
    cum = _mm(lower, lw, NN, pa=1, pb=3)
    e_p = jnp.exp(cum)
    e_m = jnp.exp(-cum)
    cum_end = cum[c - 1:c, :]
    e_end = jnp.exp(cum_end - cum)
    p_end = jnp.exp(cum_end)
    rq_all = r_ref[0] * e_p
    aq_all = al_ref[0] * jnp.exp(cum - lw)
    kd_all = k_ref[0] * e_m
    bd_all = be_ref[0] * e_m
    kend_all = k_ref[0] * e_end
    bend_all = be_ref[0] * e_end
    v_all = v_ref[0]

    heads = range(RWKV_H)
    sls = [slice(h * n, (h + 1) * n) for h in heads]
    row2 = lax.broadcasted_iota(jnp.int32, (2 * c, 2 * c), 0)
    col2 = lax.broadcasted_iota(jnp.int32, (2 * c, 2 * c), 1) & (c - 1)
    mask2 = col2 < jnp.where(row2 < c, row2, row2 - c + 1)
    zeros_v = jnp.zeros((c, n), F32)
    pg, pxs, pgv, pinv, pu, py, pst = (SCAN_PREC[k] for k in ("g", "xs", "gv", "inv", "u", "y", "st"))
    xq = [jnp.concatenate([aq_all[:, sl], rq_all[:, sl]], axis=0) for sl in sls]
    wd = [jnp.concatenate([bd_all[:, sl], kd_all[:, sl]], axis=0) for sl in sls]
    zv = [jnp.concatenate([zeros_v, v_all[:, sl]], axis=0) for sl in sls]
    s_old = [state[h] for h in heads]
    gm = [jnp.where(mask2, _mm(xq[h], wd[h], NT, *pg), 0.0) for h in heads]
    xs = [_mm(xq[h], s_old[h], NT, *pxs) for h in heads]
    gv = [_mm(gm[h], zv[h], NN, *pgv) for h in heads]
    pw = [gm[h][:c, :c] for h in heads]
    inv = [eye + a for a in pw]
    span = 2
    while span < c:
        pw = [_mm(a, a, NN, *pinv) for a in pw]
        inv = [i + _mm(i, a, NN, *pinv) for i, a in zip(inv, pw)]
        span *= 2
    us = [_mm(inv[h], xs[h][:c] + gv[h][:c], NN, *pu) for h in heads]
    ys = [xs[h][c:] + gv[h][c:] + _mm(gm[h][c:, :c], us[h], NN, *py) for h in heads]
    for h in heads:
        sl = sls[h]
        uv = jnp.concatenate([us[h], v_all[:, sl]], axis=0)
        ends = jnp.concatenate([bend_all[:, sl], kend_all[:, sl]], axis=0)
        state[h] = s_old[h] * p_end[:, sl] + _mm(uv, ends, TN, *pst)
    for h in heads:
        sl = sls[h]
        y = ys[h]
        mu = jnp.mean(y, axis=-1, keepdims=True)
        yc = y - mu
        var = jnp.mean(yc * yc, axis=-1, keepdims=True)
        y = yc * lax.rsqrt(var + RWKV_LN_EPS) * g_ref[:, sl] + b_ref[:, sl]
        y_ref[0, :, sl] = ((y + bonus_ref[0, :, sl]) * gate_ref[0, :, sl]).astype(BF16)

    @pl.when(pl.program_id(1) == pl.num_programs(1) - 1)
    def _():
        s_ref[0] = state[...]


def _scan(r, k, v, lw, al, be, gate, bonus, s0, g, b, *, chunk):
    bsz, t, _ = r.shape
    seq = pl.BlockSpec((1, chunk, RWKV_C), lambda i, j: (i, j, 0))
    st = pl.BlockSpec((1, RWKV_H, RWKV_N, RWKV_N), lambda i, j: (i, 0, 0, 0))
    vec = pl.BlockSpec((1, RWKV_C), lambda i, j: (0, 0))
    return pl.pallas_call(
        _scan_kernel,
        grid=(bsz, t // chunk),
        in_specs=[seq] * 8 + [st, vec, vec],
        out_specs=[seq, st],
        out_shape=[jax.ShapeDtypeStruct((bsz, t, RWKV_C), BF16),
                   jax.ShapeDtypeStruct((bsz, RWKV_H, RWKV_N, RWKV_N), F32)],
        scratch_shapes=[pltpu.VMEM((RWKV_H, RWKV_N, RWKV_N), F32)],
        compiler_params=_cparams("parallel", "arbitrary"),
        name="rwkv_scan",
    )(r, k, v, lw, al, be, gate, bonus, s0, g, b)


ATT_SPAN = 128


def _softmax_pv(s, v2):
    m = jnp.max(s, axis=-1, keepdims=True)
    e = jnp.exp(s - m)
    den = jnp.sum(e, axis=-1, keepdims=True)
    o = _dot((e / den).astype(BF16), v2)
    return o, m + jnp.log(den)


HEADS_PER_TILE = LANES // HEAD_DIM


ATT_UNITS = 4


def _attn_prompt_kernel(q_ref, kc_ref, kp_ref, vc_ref, vp_ref, bias_ref, o_ref, lse_ref, *, dil, nq):
    scale = HEAD_DIM ** -0.5
    first = jnp.minimum(pl.program_id(2), 1)

    def rows_of(b, res):
        start = b * ATT_SPAN * dil + res
        return pl.ds(start, ATT_SPAN, stride=dil) if dil > 1 else pl.ds(start, ATT_SPAN)

    def run(units):
        loaded = []
        for b, res in units:
            cur = rows_of(b, res)
            q, kc, vc = (ref[0, cur, :] for ref in (q_ref, kc_ref, vc_ref))
            if b == 0:
                kp, vp = (ref[0, rows_of(0, res), :] for ref in (kp_ref, vp_ref))
            else:
                kp, vp = (ref[0, rows_of(b - 1, res), :] for ref in (kc_ref, vc_ref))
            loaded.append((cur, q, kc, kp, vc, vp, first if b == 0 else 1))
        heads = [(u, h) for u in range(len(units)) for h in range(HEADS_PER_TILE)]
        sls = [slice(h * HEAD_DIM, (h + 1) * HEAD_DIM) for h in range(HEADS_PER_TILE)]
        ss = []
        for u, h in heads:
            _, q, kc, kp, _, _, variant = loaded[u]
            qh = q[:, sls[h]].astype(BF16)
            s = jnp.concatenate([_dot(qh, kp[:, sls[h]].astype(BF16), NT),
                                 _dot(qh, kc[:, sls[h]].astype(BF16), NT)], axis=1)
            ss.append(s * scale + bias_ref[variant, h])
        ms = [jnp.max(s, axis=-1, keepdims=True) for s in ss]
        es = [jnp.exp(s - m) for s, m in zip(ss, ms)]
        dens = [jnp.sum(e, axis=-1, keepdims=True) for e in es]
        outs = []
        for (u, h), e, den in zip(heads, es, dens):
            _, _, _, _, vc, vp, _ = loaded[u]
            v2 = jnp.concatenate([vp[:, sls[h]], vc[:, sls[h]]], axis=0).astype(BF16)
            outs.append(_dot((e / den).astype(BF16), v2))
        for u in range(len(units)):
            cur = loaded[u][0]
            mine = [i for i, (uu, _) in enumerate(heads) if uu == u]
            o_ref[0, cur, :] = jnp.concatenate([outs[i] for i in mine], axis=1)
            lse_ref[0, cur, :] = jnp.concatenate(
                [jnp.broadcast_to(ms[i] + jnp.log(dens[i]), outs[i].shape) for i in mine], axis=1)

    if dil > ATT_UNITS:
        def group(gi, carry):
            run([(0, gi * ATT_UNITS + r) for r in range(ATT_UNITS)])
            return carry

        lax.fori_loop(0, dil // ATT_UNITS, group, 0)
    else:
        run([(b, r) for b in range(nq) for r in range(dil)])


def _attn_prompt(ua, bias, *, g, dil):
    bsz, s, _ = ua.shape
    nq = max(ATT_UNITS // dil, 1)
    prev_rows = ATT_SPAN * dil
    rows = nq * prev_rows
    tiles = ATT_W // LANES

    def col(which):
        return lambda b, hp, i: (b, i, (which * N_DIL + g) * tiles + hp)

    def col_prev(which):
        return lambda b, hp, i: (b, jnp.maximum(i * nq - 1, 0), (which * N_DIL + g) * tiles + hp)

    blk = (1, rows, LANES)
    prev = (1, prev_rows, LANES)
    return pl.pallas_call(
        functools.partial(_attn_prompt_kernel, dil=dil, nq=nq),
        grid=(bsz, tiles, s // rows),
        in_specs=[pl.BlockSpec(blk, col(0)), pl.BlockSpec(blk, col(1)), pl.BlockSpec(prev, col_prev(1)),
                  pl.BlockSpec(blk, col(2)), pl.BlockSpec(prev, col_prev(2)),
                  pl.BlockSpec((2, HEADS_PER_TILE, ATT_SPAN, 2 * ATT_SPAN), lambda b, hp, i: (0, hp, 0, 0))],
        out_specs=[pl.BlockSpec(blk, lambda b, hp, i: (b, i, hp))] * 2,
        out_shape=[jax.ShapeDtypeStruct((bsz, s, ATT_W), F32)] * 2,
        compiler_params=_cparams("parallel", "parallel", "parallel"),
        name="attn_prompt",
    )(ua, ua, ua, ua, ua, bias)


def _attn_sample_kernel(u_ref, cache_ref, bo_ref, bn_ref, *rest, g):
    o_ref, lse_ref, new_ref = rest[-3:]
    t = u_ref.shape[1]
    rows = cache_ref.shape[-1]
    scale = HEAD_DIM ** -0.5
    base = g * ATT_W
    lane = lax.broadcasted_iota(jnp.int32, (HEAD_DIM, LANES), 1)
    place = (lax.broadcasted_iota(jnp.int32, (t, LANES), 1)
             == lax.broadcasted_iota(jnp.int32, (t, LANES), 0) + (LANES - t)).astype(BF16)
    for h in range(ATT_HG):
        sl = slice(h * HEAD_DIM, (h + 1) * HEAD_DIM)
        col = base + h * HEAD_DIM
        q = u_ref[0, :, col:col + HEAD_DIM].astype(BF16)
        k_new = u_ref[0, :, N_DIL * ATT_W + col:N_DIL * ATT_W + col + HEAD_DIM]
        v_new = u_ref[0, :, 2 * N_DIL * ATT_W + col:2 * N_DIL * ATT_W + col + HEAD_DIM]
        k_t = cache_ref[0, 0, 0, h]
        v_t = cache_ref[0, 0, 1, h]
        s_old = _dot(q, k_t.astype(BF16)) * scale + bo_ref[h]
        s_new = _dot(q, k_new.astype(BF16), NT) * scale + bn_ref[h]
        m = jnp.maximum(jnp.max(s_old, axis=-1, keepdims=True), jnp.max(s_new, axis=-1, keepdims=True))
        e_old = jnp.exp(s_old - m)
        e_new = jnp.exp(s_new - m)
        den = jnp.sum(e_old, axis=-1, keepdims=True) + jnp.sum(e_new, axis=-1, keepdims=True)
        o = _dot((e_old / den).astype(BF16), v_t.astype(BF16), NT)
        p_new = e_new / den
        for j in range(t):
            o = o + p_new[:, j:j + 1] * v_new[j:j + 1, :]
        o_ref[0, :, sl] = o
        lse_ref[0, :, sl] = jnp.broadcast_to(m + jnp.log(den), o.shape)
        for kv, (old, new) in enumerate(((k_t, k_new), (v_t, v_new))):
            moved = pltpu.roll(old, rows - t, axis=1)
            tail = jnp.where(lane >= LANES - t, _mm(new, place, TN, 3, 1), moved[:, rows - LANES:])
            if rows > LANES:
                new_ref[0, 0, kv, h, :, 0:rows - LANES] = moved[:, 0:rows - LANES]
            new_ref[0, 0, kv, h, :, rows - LANES:rows] = tail


def _attn_sample(ua, cache_t, prev, bias_old, bias_new, *, g, layer):
    bsz, t, _ = ua.shape
    out = pl.BlockSpec((1, t, ATT_W), lambda b: (b, 0, 0))
    cb = pl.BlockSpec((1, 1) + cache_t.shape[2:], lambda b: (layer, b, 0, 0, 0, 0))
    full = lambda a: pl.BlockSpec(a.shape, lambda b: (0,) * a.ndim)
    in_specs = [pl.BlockSpec((1, t, ATT_QKV), lambda b: (b, 0, 0)), cb, full(bias_old), full(bias_new)]
    args = [ua, cache_t, bias_old, bias_new]
    aliases = {}
    if prev is not None:
        in_specs.append(pl.BlockSpec(memory_space=pl.ANY))
        args.append(prev)
        aliases = {len(args) - 1: 2}
    return pl.pallas_call(
        functools.partial(_attn_sample_kernel, g=g),
        grid=(bsz,),
        in_specs=in_specs,
        out_specs=[out, out, cb],
        out_shape=[jax.ShapeDtypeStruct((bsz, t, ATT_W), F32)] * 2
                  + [jax.ShapeDtypeStruct(cache_t.shape, F32)],
        input_output_aliases=aliases,
        compiler_params=_cparams("parallel"),
        name="attn_sample",
    )(*args)


def _rel_bucket(dist):
    max_exact = N_BUCKETS // 2
    large = max_exact + (jnp.log(jnp.maximum(dist, 1).astype(F32) / max_exact)
                         / math.log(MAX_DISTANCE / max_exact) * (N_BUCKETS - max_exact)).astype(jnp.int32)
    return jnp.where(dist < max_exact, dist, jnp.minimum(large, N_BUCKETS - 1))


def _bias_lookup(tab, dist):
    onehot = (_rel_bucket(jnp.asarray(dist))[..., None] == jnp.arange(N_BUCKETS)).astype(F32)
    return jnp.einsum("...b,bh->h...", onehot, tab, precision=lax.Precision.HIGHEST)


def _prompt_bias(tab, dil):
    span = ATT_SPAN
    qi = np.arange(span)[:, None]
    ki = np.arange(2 * span)[None, :]
    j = span + qi - ki
    band = (j >= 0) & (j <= span)
    bias = _bias_lookup(tab, dil * np.clip(j, 0, span))
    rest = jnp.where(band[None], bias, NEG_INF)
    first = jnp.where((band & (ki >= span))[None], bias, NEG_INF)
    return jnp.stack([first, rest])


def _sample_bias(tab, dil, rows, t):
    rho = np.arange(rows + t)[None, :]
    d = rows + np.arange(t)[:, None] - rho
    valid = (d >= 0) & (d % dil == 0) & (d // dil <= ATT_SPAN)
    bias = jnp.where(valid[None], _bias_lookup(tab, np.clip(d, 0, None)), NEG_INF)
    return bias[:, :, :rows], bias[:, :, rows:]


def _merge_kernel(x_ref, yc_ref, yr_ref, o0_ref, o1_ref, o2_ref, l0_ref, l1_ref, l2_ref, gates_ref,
                  wc_ref, wr_ref, wa_ref, wo_ref, g_ref, b_ref, rt_ref, h_ref, gate_ref, eid_ref, *, alpha):
    l0, l1, l2 = l0_ref[...], l1_ref[...], l2_ref[...]
    m = jnp.maximum(jnp.maximum(l0, l1), l2)
    e0, e1, e2 = jnp.exp(l0 - m), jnp.exp(l1 - m), jnp.exp(l2 - m)
    den = e0 + e1 + e2
    o = o0_ref[...] * (e0 / den) + o1_ref[...] * (e1 / den) + o2_ref[...] * (e2 / den)
    d = D_MODEL
    merged = (gates_ref[:, 0:d] * _dot(yc_ref[...], wc_ref[...])
              + gates_ref[:, d:2 * d] * _dot(yr_ref[...], wr_ref[...])
              + gates_ref[:, 2 * d:3 * d] * _dot(o.astype(BF16), wa_ref[...]))
    pre = alpha * x_ref[...] + _dot(merged.astype(BF16), wo_ref[...])
    h = _layer_norm(pre, g_ref[...], b_ref[...], LN_EPS)
    for c in range(ROW_TILE):
        h_ref[pl.ds(c, h.shape[0], stride=ROW_TILE), :] = h[:, c * LANES:(c + 1) * LANES]
    logits = _mm(h, rt_ref[...], NN, 3, 3)
    lane = lax.broadcasted_iota(jnp.int32, logits.shape, 1)
    big = jnp.int32(ROUTER_PAD)
    is_grp = lane < N_GROUPS
    lg = jnp.where(is_grp, logits, NEG_INF)
    m_g = jnp.max(lg, axis=-1, keepdims=True)
    grp = jnp.min(jnp.where(lg == m_g, lane, big), axis=-1, keepdims=True)
    p_grp = 1.0 / jnp.sum(jnp.where(is_grp, jnp.exp(logits - m_g), 0.0), axis=-1, keepdims=True)
    lo_lane = N_GROUPS + grp * EXPERTS_PER_GROUP
    in_grp = (lane >= lo_lane) & (lane < lo_lane + EXPERTS_PER_GROUP)
    le = jnp.where(in_grp, logits, NEG_INF)
    v1 = jnp.max(le, axis=-1, keepdims=True)
    i1 = jnp.min(jnp.where(le == v1, lane, big), axis=-1, keepdims=True)
    le = jnp.where(lane == i1, NEG_INF, le)
    v2 = jnp.max(le, axis=-1, keepdims=True)
    i2 = jnp.min(jnp.where(le == v2, lane, big), axis=-1, keepdims=True)
    e2 = jnp.exp(v2 - v1)
    den = 1.0 + e2
    gate_ref[...] = jnp.where(lane == 0, p_grp * (1.0 / den), jnp.where(lane == 1, p_grp * (e2 / den), 0.0))
    eid_ref[...] = jnp.where(lane == 0, i1 - N_GROUPS, jnp.where(lane == 1, i2 - N_GROUPS, 0))


def _merge(x, yc, yr, os_, ls_, gates, wc, wr, wa, wo, g, b, rt, *, alpha, tm):
    n = x.shape[0]
    row = lambda c: pl.BlockSpec((tm, c), lambda i: (i, 0))
    full = lambda a: pl.BlockSpec(a.shape, lambda i: (0,) * a.ndim)
    return pl.pallas_call(
        functools.partial(_merge_kernel, alpha=alpha),
        grid=(n // tm,),
        in_specs=[row(D_MODEL), row(CONV_DIM), row(RWKV_C)] + [row(ATT_W)] * 6 + [row(N_BRANCH * D_MODEL)]
                 + [full(a) for a in (wc, wr, wa, wo, g, b, rt)],
        out_specs=[pl.BlockSpec((tm * ROW_TILE, LANES), lambda i: (i, 0)), row(ROUTER_PAD), row(ROUTER_PAD)],
        out_shape=[jax.ShapeDtypeStruct((n * ROW_TILE, LANES), F32), jax.ShapeDtypeStruct((n, ROUTER_PAD), F32),
                   jax.ShapeDtypeStruct((n, ROUTER_PAD), jnp.int32)],
        compiler_params=_cparams("parallel"),
        name="merge",
    )(x, yc, yr, *os_, *ls_, gates, wc, wr, wa, wo, g, b, rt)


DMA_UNROLL = 8
ROW_TILE = D_MODEL // LANES


def _expert_kernel(be_ref, off_ref, cnt_ref, order_ref, h_hbm, wg_ref, wu_ref, wd_ref, out_hbm,
                   xbuf, ybuf, gsem, ssem):
    i = pl.program_id(0)
    nb = pl.num_programs(0)
    slot = i & 1
    n_out = out_hbm.shape[0] // ROW_TILE - 2 * MOE_BLOCK

    def tile(row):
        return pl.ds(pl.multiple_of(row * ROW_TILE, ROW_TILE), ROW_TILE)

    def chunk(c):
        return pl.ds(c, MOE_BLOCK, stride=ROW_TILE)

    def gather_start(blk, s):
        base = off_ref[blk]

        def body(r, carry):
            tok = order_ref[base + r] // TOP_K
            pltpu.make_async_copy(h_hbm.at[tile(tok)], xbuf.at[s, tile(r)], gsem.at[s]).start()
            return carry

        lax.fori_loop(0, MOE_BLOCK, body, 0, unroll=DMA_UNROLL)

    def scatter_start(blk, s):
        base = off_ref[blk]
        cnt = cnt_ref[blk]

        def body(r, carry):
            dst = jnp.where(r < cnt, order_ref[base + r], n_out + s * MOE_BLOCK + r)
            pltpu.make_async_copy(ybuf.at[s, tile(r)], out_hbm.at[tile(dst)], ssem.at[s]).start()
            return carry

        lax.fori_loop(0, MOE_BLOCK, body, 0, unroll=DMA_UNROLL)

    def gather_wait(s):
        def body(r, carry):
            pltpu.make_async_copy(h_hbm.at[tile(0)], xbuf.at[s, tile(r)], gsem.at[s]).wait()
            return carry

        lax.fori_loop(0, MOE_BLOCK, body, 0, unroll=DMA_UNROLL)

    def scatter_wait(s):
        def body(r, carry):
            pltpu.make_async_copy(ybuf.at[s, tile(r)], out_hbm.at[tile(0)], ssem.at[s]).wait()
            return carry

        lax.fori_loop(0, MOE_BLOCK, body, 0, unroll=DMA_UNROLL)

    active = cnt_ref[i] > 0

    @pl.when(i == 0)
    def _():
        ybuf[...] = jnp.zeros(ybuf.shape, F32)
        for s in range(2):
            spare = pltpu.make_async_copy(
                ybuf.at[s], out_hbm.at[pl.ds((n_out + s * MOE_BLOCK) * ROW_TILE, MOE_BLOCK * ROW_TILE)], ssem.at[s])
            spare.start()
            spare.wait()

    @pl.when((i == 0) & active)
    def _():
        gather_start(0, 0)

    @pl.when((i >= 2) & (cnt_ref[jnp.maximum(i - 2, 0)] > 0))
    def _():
        scatter_wait(slot)

    @pl.when(active)
    def _():
        nxt = jnp.minimum(i + 1, nb - 1)

        @pl.when((i + 1 < nb) & (cnt_ref[nxt] > 0))
        def _():
            gather_start(nxt, 1 - slot)

        gather_wait(slot)
        gate = jnp.zeros((MOE_BLOCK, D_EXPERT), F32)
        up = jnp.zeros((MOE_BLOCK, D_EXPERT), F32)
        for c in range(ROW_TILE):
            xc = xbuf[slot, chunk(c), :].astype(BF16)
            gate = gate + _dot(xc, wg_ref[c * LANES:(c + 1) * LANES, :])
            up = up + _dot(xc, wu_ref[c * LANES:(c + 1) * LANES, :])
        act = (gate * _sigmoid(gate) * up).astype(BF16)
        for c in range(ROW_TILE):
            ybuf[slot, chunk(c), :] = _dot(act, wd_ref[:, c * LANES:(c + 1) * LANES])
        scatter_start(i, slot)

    @pl.when(i == nb - 1)
    def _():
        @pl.when((nb >= 2) & (cnt_ref[jnp.maximum(nb - 2, 0)] > 0))
        def _():
            scatter_wait(1 - slot)

        @pl.when(active)
        def _():
            scatter_wait(slot)


def _experts(ht, block_e, block_off, block_cnt, order, wg, wu, wd, *, layer):
    n = ht.shape[0] // ROW_TILE
    n_blocks = block_e.shape[0]
    wmap = lambda i, be, off, cnt, order: (layer, be[i], 0, 0)
    grid_spec = pltpu.PrefetchScalarGridSpec(
        num_scalar_prefetch=4,
        grid=(n_blocks,),
        in_specs=[pl.BlockSpec(memory_space=pl.ANY),
                  pl.BlockSpec((None, None, D_MODEL, D_EXPERT), wmap),
                  pl.BlockSpec((None, None, D_MODEL, D_EXPERT), wmap),
                  pl.BlockSpec((None, None, D_EXPERT, D_MODEL), wmap)],
        out_specs=pl.BlockSpec(memory_space=pl.ANY),
        scratch_shapes=[pltpu.VMEM((2, MOE_BLOCK * ROW_TILE, LANES), F32),
                        pltpu.VMEM((2, MOE_BLOCK * ROW_TILE, LANES), F32),
                        pltpu.SemaphoreType.DMA((2,)), pltpu.SemaphoreType.DMA((2,))],
    )
    return pl.pallas_call(
        _expert_kernel,
        grid_spec=grid_spec,
        out_shape=jax.ShapeDtypeStruct(((TOP_K * n + 2 * MOE_BLOCK) * ROW_TILE, LANES), F32),
        compiler_params=_cparams("arbitrary"),
        name="experts",
    )(block_e, block_off, block_cnt, order, ht, wg, wu, wd)


def _route(eid):
    n = eid.shape[0]
    a = n * TOP_K
    n_blocks = -(-a // MOE_BLOCK) + N_EXPERTS
    skey = jnp.sort(eid.reshape(-1) * a + jnp.arange(a, dtype=jnp.int32))
    order = skey % a
    experts = jnp.arange(N_EXPERTS, dtype=jnp.int32)
    start = jnp.sum(skey[None, :] < (experts * a)[:, None], axis=1, dtype=jnp.int32)
    counts = jnp.sum(skey[None, :] < ((experts + 1) * a)[:, None], axis=1, dtype=jnp.int32) - start
    padded = (counts + MOE_BLOCK - 1) // MOE_BLOCK * MOE_BLOCK
    pend = jnp.cumsum(padded)
    pstart = pend - padded
    blk_start = jnp.arange(n_blocks, dtype=jnp.int32) * MOE_BLOCK
    block_e = jnp.minimum(jnp.sum(pend[None, :] <= blk_start[:, None], axis=1, dtype=jnp.int32), N_EXPERTS - 1)
    pick = (block_e[:, None] == experts[None, :]).astype(jnp.int32)
    within = blk_start - jnp.sum(pick * pstart[None, :], axis=1)
    block_cnt = jnp.where(blk_start < pend[-1],
                          jnp.clip(jnp.sum(pick * counts[None, :], axis=1) - within, 0, MOE_BLOCK), 0)
    block_off = jnp.where(block_cnt > 0, jnp.sum(pick * start[None, :], axis=1) + within, 0)
    order = jnp.concatenate([order, jnp.zeros((MOE_BLOCK,), jnp.int32)])
    return block_e, block_off.astype(jnp.int32), block_cnt.astype(jnp.int32), order


def _ln2_kernel(h_ref, f_ref, gate_ref, g_ref, b_ref, y_ref, yb_ref, *, alpha):
    tm = y_ref.shape[0]
    g0, g1 = gate_ref[:, 0:1], gate_ref[:, 1:2]
    pre = []
    for c in range(ROW_TILE):
        f0 = f_ref[pl.ds(c, tm, stride=TOP_K * ROW_TILE), :]
        f1 = f_ref[pl.ds(ROW_TILE + c, tm, stride=TOP_K * ROW_TILE), :]
        pre.append(alpha * h_ref[pl.ds(c, tm, stride=ROW_TILE), :] + (f0 * g0 + f1 * g1))
    mu = sum(jnp.sum(p, axis=-1, keepdims=True) for p in pre) * (1.0 / D_MODEL)
    cen = [p - mu for p in pre]
    var = sum(jnp.sum(p * p, axis=-1, keepdims=True) for p in cen) * (1.0 / D_MODEL)
    rstd = lax.rsqrt(var + LN_EPS)
    for c in range(ROW_TILE):
        sl = slice(c * LANES, (c + 1) * LANES)
        y = cen[c] * rstd * g_ref[:, sl] + b_ref[:, sl]
        y_ref[:, sl] = y
        yb_ref[:, sl] = y.astype(BF16)


def _ln2(ht, ft, gate, g, b, *, alpha, tm):
    n = ht.shape[0] // ROW_TILE
    vec = pl.BlockSpec((1, D_MODEL), lambda i: (0, 0))
    return pl.pallas_call(
        functools.partial(_ln2_kernel, alpha=alpha),
        grid=(n // tm,),
        in_specs=[pl.BlockSpec((tm * ROW_TILE, LANES), lambda i: (i, 0)),
                  pl.BlockSpec((tm * TOP_K * ROW_TILE, LANES), lambda i: (i, 0)),
                  pl.BlockSpec((tm, ROUTER_PAD), lambda i: (i, 0)), vec, vec],
        out_specs=[pl.BlockSpec((tm, D_MODEL), lambda i: (i, 0))] * 2,
        out_shape=[jax.ShapeDtypeStruct((n, D_MODEL), F32), jax.ShapeDtypeStruct((n, D_MODEL), BF16)],
        compiler_params=_cparams("parallel"),
        name="ln2",
    )(ht, ft, gate, g, b)


def _row(v):
    return v.reshape(1, -1)


def _layer(x, xb, p, conv_state, shift, wkv, caches, *, alpha, prompt, layer):
    bsz, t, d = x.shape
    n = bsz * t
    tm = 512 if prompt else n
    xb2 = xb.reshape(n, d)
    o1 = 2 * CONV_DIM
    o2 = o1 + RWKV_U
    o3 = o2 + ATT_QKV
    wide = dict(tn=D_MODEL, tm=min(n, 1024))
    narrow = dict(tn=256, tm=min(n, 2048))
    proj = functools.partial(_proj, xb2, p["w_in"], p["b_in"], layer=layer)
    z = proj(col0=0, n_cols=o1, epilogue="glu", **wide)
    ur = proj(col0=o1, n_cols=RWKV_U, epilogue="none", **narrow)
    ua = proj(col0=o2, n_cols=ATT_QKV, epilogue="none", **narrow)
    gates = proj(col0=o3, n_cols=N_BRANCH * D_MODEL, epilogue="sigmoid_bias", **wide)
    z = z.reshape(bsz, t, CONV_DIM)
    ur = ur.reshape(bsz, t, RWKV_U)
    ua = ua.reshape(bsz, t, ATT_QKV)

    hist = jnp.pad(conv_state, ((0, 0), (CONV_HIST - (CONV_K - 1), 0), (0, 0)))
    yc = _conv(z, hist, p["conv_dw"], p["conv_dw_b"], p["conv_ln_g"], p["conv_ln_b"], tt=256 if prompt else t)
    conv_new = jnp.concatenate([conv_state, z], axis=1)[:, -(CONV_K - 1):]

    shift8 = jnp.pad(shift[:, None, :], ((0, 0), (SUBLANES - 1, 0), (0, 0)))
    pre = _rwkv_pre(ur, shift8, p["rwkv_mu"], p["rwkv_w0"], p["rwkv_w2p"], p["rwkv_a0"], p["rwkv_a2p"],
                    p["rwkv_g2"], p["rwkv_kk"], p["rwkv_ka"], p["rwkv_rk"], p["ones_bd"],
                    tt=256 if prompt else t)
    chunk = 64 if prompt else SUBLANES
    if t % chunk:
        pre = [jnp.pad(a, ((0, 0), (0, chunk - t % chunk), (0, 0))) for a in pre]
    yr, wkv_new = _scan(*pre, wkv, p["rwkv_ln_g"], p["rwkv_ln_b"], chunk=chunk)
    yr = yr[:, :t]
    shift_new = ur[:, -1]

    os_, ls_, att_new = [], [], []
    for g, (window, dil) in enumerate(DILATIONS):
        if prompt:
            o, lse = _attn_prompt(ua, p["att_bias"][g], g=g, dil=dil)
            rows = min(window, t)
            kv = [ua[:, t - rows:, (w * N_DIL + g) * ATT_W:(w * N_DIL + g + 1) * ATT_W] for w in (1, 2)]
            att_new.append(jnp.stack([a.reshape(bsz, rows, ATT_HG, HEAD_DIM) for a in kv], axis=1))
        else:
            cache_t, prev = caches[g]
            o, lse, new = _attn_sample(ua, cache_t, prev, *p["att_bias"][g], g=g, layer=layer)
            att_new.append(new)
        os_.append(o.reshape(n, ATT_W))
        ls_.append(lse.reshape(n, ATT_W))

    ht, gate, eid = _merge(x.reshape(n, d), yc.reshape(n, CONV_DIM), yr.reshape(n, RWKV_C), os_, ls_, gates,
                           p["conv_proj"], p["rwkv_proj"], p["attn_proj"], p["w_out"], p["ln1_g"], p["ln1_b"],
                           p["router"], alpha=alpha, tm=256 if prompt else n)

    ft = _experts(ht, *_route(eid[:, :TOP_K]), p["moe_w_gate"], p["moe_w_up"], p["moe_w_down"], layer=layer)
    y, yb = _ln2(ht, ft, gate, p["ln2_g"], p["ln2_b"], alpha=alpha, tm=tm)
    return y.reshape(bsz, t, d), yb.reshape(bsz, t, d), conv_new, shift_new, wkv_new, att_new


def kernel(x_prompt, x_sample, state_conv, state_shift, state_wkv, cache_attn_w128, cache_attn_w512, cache_attn_w2048, rel_bias, w_in, b_gate, conv_dw, conv_dw_b, conv_ln_g, conv_ln_b, conv_proj, rwkv_mu, rwkv_w0, rwkv_w2, rwkv_a0, rwkv_a2, rwkv_g2, rwkv_kk, rwkv_ka, rwkv_rk, rwkv_ln_g, rwkv_ln_b, rwkv_proj, attn_proj, w_out, ln1_g, ln1_b, router_group, router_expert, moe_w_gate, moe_w_up, moe_w_down, ln2_g, ln2_b):
    depth = w_in.shape[0]
    alpha = (2 * depth) ** 0.25
    bp, tp, _ = x_prompt.shape
    ts = x_sample.shape[1]
    caches_t = [jnp.transpose(c, (0, 1, 2, 4, 5, 3)) for c in (cache_attn_w128, cache_attn_w512, cache_attn_w2048)]
    new_caches = [None] * N_DIL
    head = np.arange(RWKV_C) // RWKV_N
    ones_bd = jnp.asarray(head[:, None] == head[None, :], BF16)
    tabs = [rel_bias[:, g * ATT_HG:(g + 1) * ATT_HG] for g in range(N_DIL)]
    bias_p = [_prompt_bias(tabs[g], dil) for g, (_, dil) in enumerate(DILATIONS)]
    bias_s = [_sample_bias(tabs[g], dil, caches_t[g].shape[-1], ts) for g, (_, dil) in enumerate(DILATIONS)]

    w_in_b = w_in.astype(BF16)
    b_in = jnp.pad(b_gate, ((0, 0), (w_in.shape[2] - b_gate.shape[1], 0)))[:, None, :]
    moe_b = [w.astype(BF16) for w in (moe_w_gate, moe_w_up, moe_w_down)]

    xp, xs = x_prompt, x_sample
    xpb, xsb = xp.astype(BF16), xs.astype(BF16)
    outs_p, outs_s = [], []
    for l in range(depth):
        zeros_lora = jnp.zeros((LORA_W, RWKV_C), F32)
        router = jnp.concatenate(
            [router_group[l], router_expert[l],
             jnp.zeros((D_MODEL, ROUTER_PAD - N_GROUPS - N_EXPERTS), F32)], axis=1)
        p = dict(
            w_in=w_in_b, b_in=b_in,
            conv_dw=conv_dw[l], conv_dw_b=_row(conv_dw_b[l]), conv_ln_g=_row(conv_ln_g[l]),
            conv_ln_b=_row(conv_ln_b[l]), conv_proj=conv_proj[l].astype(BF16),
            rwkv_mu=_row(rwkv_mu[l]), rwkv_w0=_row(rwkv_w0[l]),
            rwkv_w2p=jnp.concatenate([rwkv_w2[l], zeros_lora], axis=0).astype(BF16),
            rwkv_a0=_row(rwkv_a0[l]),
            rwkv_a2p=jnp.concatenate([zeros_lora, rwkv_a2[l]], axis=0).astype(BF16),
            rwkv_g2=rwkv_g2[l].astype(BF16), rwkv_kk=_row(rwkv_kk[l]), rwkv_ka=_row(rwkv_ka[l]),
            rwkv_rk=_row(rwkv_rk[l]), rwkv_ln_g=_row(rwkv_ln_g[l]), rwkv_ln_b=_row(rwkv_ln_b[l]),
            rwkv_proj=rwkv_proj[l].astype(BF16), attn_proj=attn_proj[l].astype(BF16),
            w_out=w_out[l].astype(BF16), ln1_g=_row(ln1_g[l]), ln1_b=_row(ln1_b[l]), router=router,
            moe_w_gate=moe_b[0], moe_w_up=moe_b[1], moe_w_down=moe_b[2],
            ln2_g=_row(ln2_g[l]), ln2_b=_row(ln2_b[l]),
            ones_bd=ones_bd)
        xp, xpb, *new_p = _layer(
            xp, xpb, dict(p, att_bias=bias_p),
            jnp.zeros((bp, CONV_K - 1, CONV_DIM), F32), jnp.zeros((bp, RWKV_U), F32),
            jnp.zeros((bp, RWKV_H, RWKV_N, RWKV_N), F32), None, alpha=alpha, prompt=True, layer=l)
        xs, xsb, *new_s = _layer(
            xs, xsb, dict(p, att_bias=bias_s), state_conv[l], state_shift[l], state_wkv[l],
            list(zip(caches_t, new_caches)), alpha=alpha, prompt=False, layer=l)
        new_caches = new_s[3]
        outs_p.append(new_p)
        outs_s.append(new_s)

    def stack(outs, i):
        return jnp.stack([o[i] for o in outs])

    att_p = [jnp.stack([o[3][g] for o in outs_p]) for g in range(N_DIL)]
    att_s = [jnp.transpose(c, (0, 1, 2, 5, 3, 4)) for c in new_caches]
    return (xp, xs, stack(outs_p, 0), stack(outs_s, 0), stack(outs_p, 1), stack(outs_s, 1),
            stack(outs_p, 2), stack(outs_s, 2),
            att_p[0], att_s[0], att_p[1], att_s[1], att_p[2], att_s[2])
```

```python
import functools
import math

import jax
import jax.numpy as jnp
import numpy as np
from jax import lax
from jax.experimental import pallas as pl
from jax.experimental.pallas import tpu as pltpu

F32 = jnp.float32
BF16 = jnp.bfloat16

D_MODEL = 1024
CONV_DIM = D_MODEL // 2
CONV_K = 31
RWKV_N = 64
RWKV_H = D_MODEL // 128
RWKV_C = RWKV_H * RWKV_N
LORA_W, LORA_A, LORA_G = 64, 64, 128
RWKV_U = 3 * RWKV_C + LORA_W + LORA_A + LORA_G
RWKV_LN_EPS = 64e-5
HEAD_DIM = 64
ATT_HG = 4
ATT_W = ATT_HG * HEAD_DIM
DILATIONS = ((128, 1), (512, 4), (2048, 16))
N_DIL = len(DILATIONS)
ATT_QKV = 3 * N_DIL * ATT_W
N_BUCKETS = 32
MAX_DISTANCE = 2048
N_BRANCH = 3
N_GROUPS = 4
EXPERTS_PER_GROUP = 8
N_EXPERTS = N_GROUPS * EXPERTS_PER_GROUP
TOP_K = 2
D_EXPERT = D_MODEL // 2
MOE_BLOCK = 128
LN_EPS = 1e-5
NEG_INF = -1e30
ROUTER_PAD = 128

LANES = 128
SUBLANES = 8
VMEM_LIMIT = 48 * 1024 * 1024

NN = (((1,), (0,)), ((), ()))
NT = (((1,), (1,)), ((), ()))
TN = (((0,), (0,)), ((), ()))


def _cparams(*sem):
    return pltpu.CompilerParams(dimension_semantics=sem, vmem_limit_bytes=VMEM_LIMIT)


def _dot(a, b, dims=NN):
    return lax.dot_general(a, b, dims, preferred_element_type=F32)


def _pieces(x, n):
    if x.dtype == BF16:
        return [x]
    out, r = [], x
    for i in range(n):
        p = r.astype(BF16)
        out.append(p)
        if i + 1 < n:
            r = r - p.astype(F32)
    return out


def _mm(a, b, dims=NN, pa=1, pb=1):
    ap, bp = _pieces(a, pa), _pieces(b, pb)
    order = max(len(ap), len(bp))
    acc = None
    for i in reversed(range(len(ap))):
        for j in reversed(range(len(bp))):
            if i + j < order:
                t = _dot(ap[i], bp[j], dims)
                acc = t if acc is None else acc + t
    return acc


def _sigmoid(x):
    return 1.0 / (1.0 + jnp.exp(-x))


def _layer_norm(x, g, b, eps):
    mu = jnp.mean(x, axis=-1, keepdims=True)
    xc = x - mu
    var = jnp.mean(xc * xc, axis=-1, keepdims=True)
    return xc * lax.rsqrt(var + eps) * g + b


def _proj_kernel(x_ref, w_ref, b_ref, o_ref, *, epilogue):
    acc = _dot(x_ref[...], w_ref[...])
    if epilogue == "glu":
        half = acc.shape[1] // 2
        o_ref[...] = acc[:, :half] * _sigmoid(acc[:, half:])
    elif epilogue == "sigmoid_bias":
        o_ref[...] = _sigmoid(acc + b_ref[...])
    else:
        o_ref[...] = acc


def _proj(x, w_all, b_all, *, layer, col0, n_cols, epilogue, tn, tm):
    n, k = x.shape
    out_cols = n_cols // 2 if epilogue == "glu" else n_cols
    out_tn = tn // 2 if epilogue == "glu" else tn
    c0 = col0 // tn
    mt, nt = n // tm, n_cols // tn
    rows_outer = mt * n_cols + n <= nt * n + n_cols
    grid = (mt, nt) if rows_outer else (nt, mt)
    ij = (lambda a, b: (a, b)) if rows_outer else (lambda a, b: (b, a))
    return pl.pallas_call(
        functools.partial(_proj_kernel, epilogue=epilogue),
        grid=grid,
        in_specs=[pl.BlockSpec((tm, k), lambda a, b: (ij(a, b)[0], 0)),
                  pl.BlockSpec((None, k, tn), lambda a, b: (layer, 0, c0 + ij(a, b)[1])),
                  pl.BlockSpec((None, 1, tn), lambda a, b: (layer, 0, c0 + ij(a, b)[1]))],
        out_specs=pl.BlockSpec((tm, out_tn), lambda a, b: ij(a, b)),
        out_shape=jax.ShapeDtypeStruct((n, out_cols), F32),
        compiler_params=_cparams("parallel", "parallel"),
        name="in_proj_" + epilogue,
    )(x, w_all, b_all)


CONV_HIST = 32
CONV_ROWS = 64


def _conv_kernel(z_ref, hist_ref, dw_ref, dwb_ref, g_ref, b_ref, o_ref, zbuf):
    tt = z_ref.shape[1]
    win = CONV_HIST + tt

    @pl.when(pl.program_id(1) == 0)
    def _():
        zbuf[0, 0:CONV_HIST, :] = hist_ref[0]

    @pl.when(pl.program_id(1) != 0)
    def _():
        zbuf[0, 0:CONV_HIST, :] = zbuf[0, tt:tt + CONV_HIST, :]

    zbuf[0, CONV_HIST:win, :] = z_ref[0]
    if tt % SUBLANES == 0:
        for s in range(1, SUBLANES):
            zbuf[s, 0:win - SUBLANES, :] = zbuf[0, s:s + win - SUBLANES, :]
    pad = CONV_HIST - (CONV_K - 1)
    rb = min(tt, CONV_ROWS)
    for r0 in range(0, tt, rb):
        acc = jnp.zeros((rb, CONV_DIM), F32)
        for k in range(CONV_K):
            off = r0 + pad + k
            if tt % SUBLANES == 0:
                tap = zbuf[off % SUBLANES, off - off % SUBLANES:off - off % SUBLANES + rb, :]
            else:
                tap = zbuf[0, off:off + rb, :]
            acc = acc + tap * dw_ref[k:k + 1, :]
        y = _layer_norm(acc + dwb_ref[...], g_ref[...], b_ref[...], LN_EPS)
        o_ref[0, r0:r0 + rb, :] = (y * _sigmoid(y)).astype(BF16)


def _conv(z, hist, dw, dwb, g, b, *, tt):
    bsz, t, _ = z.shape
    vec = pl.BlockSpec((1, CONV_DIM), lambda i, j: (0, 0))
    return pl.pallas_call(
        _conv_kernel,
        grid=(bsz, t // tt),
        in_specs=[pl.BlockSpec((1, tt, CONV_DIM), lambda i, j: (i, j, 0)),
                  pl.BlockSpec((1, CONV_HIST, CONV_DIM), lambda i, j: (i, 0, 0)),
                  pl.BlockSpec((CONV_K, CONV_DIM), lambda i, j: (0, 0)),
                  vec, vec, vec],
        out_specs=pl.BlockSpec((1, tt, CONV_DIM), lambda i, j: (i, j, 0)),
        out_shape=jax.ShapeDtypeStruct((bsz, t, CONV_DIM), BF16),
        scratch_shapes=[pltpu.VMEM((SUBLANES, CONV_HIST + tt, CONV_DIM), F32)],
        compiler_params=_cparams("parallel", "arbitrary"),
        name="conv",
    )(z, hist, dw, dwb, g, b)


def _head_sum(x, ones_bd):
    return _mm(x, ones_bd, NN, pa=3, pb=1)


def _rwkv_pre_kernel(u_ref, shift_ref, mu_ref, w0_ref, w2_ref, a0_ref, a2_ref, g2_ref, kk_ref, ka_ref,
                     rk_ref, ones_ref,
                     r_ref, k_ref, v_ref, lw_ref, al_ref, be_ref, gate_ref, bonus_ref, ubuf):
    tt = u_ref.shape[1]
    c = RWKV_C

    @pl.when(pl.program_id(1) == 0)
    def _():
        ubuf[0:SUBLANES, :] = shift_ref[0]

    @pl.when(pl.program_id(1) != 0)
    def _():
        ubuf[0:SUBLANES, :] = ubuf[tt:tt + SUBLANES, :]

    u = u_ref[0]
    ubuf[SUBLANES:SUBLANES + tt, :] = u
    u_prev = ubuf[SUBLANES - 1:SUBLANES - 1 + tt, :]
    um = u + (u_prev - u) * mu_ref[...]
    r, k, v = um[:, 0:c], um[:, c:2 * c], um[:, 2 * c:3 * c]
    lo = um[:, 3 * c:3 * c + LORA_W + LORA_A]
    lane = lax.broadcasted_iota(jnp.int32, lo.shape, 1)
    lo = jnp.where(lane < LORA_W, jnp.tanh(lo), lo).astype(BF16)
    g_lo = _sigmoid(um[:, 3 * c + LORA_W + LORA_A:]).astype(BF16)
    xw = -(w0_ref[...] + _dot(lo, w2_ref[...]))
    softplus = jnp.maximum(xw, 0.0) + jnp.log(1.0 + jnp.exp(-jnp.abs(xw)))
    lw_ref[0] = -jnp.exp(-softplus - 0.5)
    a = _sigmoid(a0_ref[...] + _dot(lo, a2_ref[...]))
    gate_ref[0] = _dot(g_lo, g2_ref[...])
    ones_bd = ones_ref[...]
    kk = k * kk_ref[...]
    norm = jnp.sqrt(_head_sum(kk * kk, ones_bd))
    kk = kk / jnp.maximum(norm, 1e-12)
    k = k * (1.0 + (a - 1.0) * ka_ref[...])
    r_ref[0] = r
    k_ref[0] = k
    v_ref[0] = v
    al_ref[0] = -kk
    be_ref[0] = kk * a
    bonus_ref[0] = _head_sum(r * k * rk_ref[...], ones_bd) * v


def _rwkv_pre(u, shift8, mu, w0, w2p, a0, a2p, g2, kkp, ka, rk, ones_bd, *, tt):
    bsz, t, _ = u.shape
    full = lambda a: pl.BlockSpec(a.shape, lambda i, j: (0,) * a.ndim)
    seq = pl.BlockSpec((1, tt, RWKV_C), lambda i, j: (i, j, 0))
    return pl.pallas_call(
        _rwkv_pre_kernel,
        grid=(bsz, t // tt),
        in_specs=[pl.BlockSpec((1, tt, RWKV_U), lambda i, j: (i, j, 0)),
                  pl.BlockSpec((1, SUBLANES, RWKV_U), lambda i, j: (i, 0, 0)),
                  full(mu), full(w0), full(w2p), full(a0), full(a2p), full(g2), full(kkp), full(ka),
                  full(rk), full(ones_bd)],
        out_specs=[seq] * 8,
        out_shape=[jax.ShapeDtypeStruct((bsz, t, RWKV_C), F32)] * 8,
        scratch_shapes=[pltpu.VMEM((SUBLANES + tt, RWKV_U), F32)],
        compiler_params=_cparams("parallel", "arbitrary"),
        name="rwkv_pre",
    )(u, shift8, mu, w0, w2p, a0, a2p, g2, kkp, ka, rk, ones_bd)


def _scan_kernel(r_ref, k_ref, v_ref, lw_ref, al_ref, be_ref, gate_ref, bonus_ref, s0_ref, g_ref, b_ref,
                 y_ref, s_ref, state):
    c = r_ref.shape[1]
    n = RWKV_N

    @pl.when(pl.program_id(1) == 0)
    def _():
        state[...] = s0_ref[0]

    row = lax.broadcasted_iota(jnp.int32, (c, c), 0)
    col = lax.broadcasted_iota(jnp.int32, (c, c), 1)
    lower = (row >= col).astype(BF16)
    eye = (row == col).astype(F32)

    lw = lw_ref[0]
    cum = _mm(lower, lw, NN, pa=1, pb=3)
    e_p = jnp.exp(cum)
    e_m = jnp.exp(-cum)
    cum_end = cum[c - 1:c, :]
    e_end = jnp.exp(cum_end - cum)
    p_end = jnp.exp(cum_end)
    rq_all = r_ref[0] * e_p
    aq_all = al_ref[0] * jnp.exp(cum - lw)
    kd_all = k_ref[0] * e_m
    bd_all = be_ref[0] * e_m
    kend_all = k_ref[0] * e_end
    bend_all = be_ref[0] * e_end
    v_all = v_ref[0]

    heads = range(RWKV_H)
    sls = [slice(h * n, (h + 1) * n) for h in heads]
    row2 = lax.broadcasted_iota(jnp.int32, (2 * c, 2 * c), 0)
    col2 = lax.broadcasted_iota(jnp.int32, (2 * c, 2 * c), 1) & (c - 1)
    mask2 = col2 < jnp.where(row2 < c, row2, row2 - c + 1)
    zeros_v = jnp.zeros((c, n), F32)
    xq =[jnp.concatenate([aq_all[:, sl], rq_all[:, sl]], axis=0) for sl in sls]
    wd = [jnp.concatenate([bd_all[:, sl], kd_all[:, sl]], axis=0) for sl in sls]
    zv = [jnp.concatenate([zeros_v, v_all[:, sl]], axis=0) for sl in sls]
    s_old = [state[h] for h in heads]
    gm = [jnp.where(mask2, _mm(xq[h], wd[h], NT), 0.0) for h in heads]
    xs = [_mm(xq[h], s_old[h], NT) for h in heads]
    gv = [_mm(gm[h], zv[h], NN) for h in heads]
    pw = [gm[h][:c, :c] for h in heads]
    inv = [eye + a for a in pw]
    span = 2
    while span < c:
        pw = [_mm(a, a, NN) for a in pw]
        inv = [i + _mm(i, a, NN) for i, a in zip(inv, pw)]
        span *= 2
    us = [_mm(inv[h], xs[h][:c] + gv[h][:c], NN) for h in heads]
    ys = [xs[h][c:] + gv[h][c:] + _mm(gm[h][c:, :c], us[h], NN) for h in heads]
    for h in heads:
        sl = sls[h]
        uv = jnp.concatenate([us[h], v_all[:, sl]], axis=0)
        ends = jnp.concatenate([bend_all[:, sl], kend_all[:, sl]], axis=0)
        state[h] = s_old[h] * p_end[:, sl] + _mm(uv, ends, TN)
    for h in heads:
        sl = sls[h]
        y = ys[h]
        mu = jnp.mean(y, axis=-1, keepdims=True)
        yc = y - mu
        var = jnp.mean(yc * yc, axis=-1, keepdims=True)
        y = yc * lax.rsqrt(var + RWKV_LN_EPS) * g_ref[:, sl] + b_ref[:, sl]
        y_ref[0, :, sl] = ((y + bonus_ref[0, :, sl]) * gate_ref[0, :, sl]).astype(BF16)

    @pl.when(pl.program_id(1) == pl.num_programs(1) - 1)
    def _():
        s_ref[0] = state[...]


def _scan(r, k, v, lw, al, be, gate, bonus, s0, g, b, *, chunk):
    bsz, t, _ = r.shape
    seq = pl.BlockSpec((1, chunk, RWKV_C), lambda i, j: (i, j, 0))
    st = pl.BlockSpec((1, RWKV_H, RWKV_N, RWKV_N), lambda i, j: (i, 0, 0, 0))
    vec = pl.BlockSpec((1, RWKV_C), lambda i, j: (0, 0))
    return pl.pallas_call(
        _scan_kernel,
        grid=(bsz, t // chunk),
        in_specs=[seq] * 8 + [st, vec, vec],
        out_specs=[seq, st],
        out_shape=[jax.ShapeDtypeStruct((bsz, t, RWKV_C), BF16),
                   jax.ShapeDtypeStruct((bsz, RWKV_H, RWKV_N, RWKV_N), F32)],
        scratch_shapes=[pltpu.VMEM((RWKV_H, RWKV_N, RWKV_N), F32)],
        compiler_params=_cparams("parallel", "arbitrary"),
        name="rwkv_scan",
    )(r, k, v, lw, al, be, gate, bonus, s0, g, b)


ATT_SPAN = 128


HEADS_PER_TILE = LANES // HEAD_DIM


ATT_UNITS = 4


def _attn_prompt_kernel(q_ref, kc_ref, kp_ref, vc_ref, vp_ref, bias_ref, o_ref, lse_ref, *, dil, nq):
    scale = HEAD_DIM ** -0.5
    first = jnp.minimum(pl.program_id(2), 1)

    def rows_of(b, res):
        start = b * ATT_SPAN * dil + res
        return pl.ds(start, ATT_SPAN, stride=dil) if dil > 1 else pl.ds(start, ATT_SPAN)

    def run(units):
        loaded = []
        for b, res in units:
            cur = rows_of(b, res)
            q, kc, vc = (ref[0, cur, :] for ref in (q_ref, kc_ref, vc_ref))
            if b == 0:
                kp, vp = (ref[0, rows_of(0, res), :] for ref in (kp_ref, vp_ref))
            else:
                kp, vp = (ref[0, rows_of(b - 1, res), :] for ref in (kc_ref, vc_ref))
            loaded.append((cur, q, kc, kp, vc, vp, first if b == 0 else 1))
        heads = [(u, h) for u in range(len(units)) for h in range(HEADS_PER_TILE)]
        sls = [slice(h * HEAD_DIM, (h + 1) * HEAD_DIM) for h in range(HEADS_PER_TILE)]
        ss = []
        for u, h in heads:
            _, q, kc, kp, _, _, variant = loaded[u]
            qh = q[:, sls[h]].astype(BF16)
            s = jnp.concatenate([_dot(qh, kp[:, sls[h]].astype(BF16), NT),
                                 _dot(qh, kc[:, sls[h]].astype(BF16), NT)], axis=1)
            ss.append(s * scale + bias_ref[variant, h])
        ms = [jnp.max(s, axis=-1, keepdims=True) for s in ss]
        es = [jnp.exp(s - m) for s, m in zip(ss, ms)]
        dens = [jnp.sum(e, axis=-1, keepdims=True) for e in es]
        outs = []
        for (u, h), e, den in zip(heads, es, dens):
            _, _, _, _, vc, vp, _ = loaded[u]
            v2 = jnp.concatenate([vp[:, sls[h]], vc[:, sls[h]]], axis=0).astype(BF16)
            outs.append(_dot((e / den).astype(BF16), v2))
        for u in range(len(units)):
            cur = loaded[u][0]
            mine = [i for i, (uu, _) in enumerate(heads) if uu == u]
            o_ref[0, cur, :] = jnp.concatenate([outs[i] for i in mine], axis=1)
            lse_ref[0, cur, :] = jnp.concatenate(
                [jnp.broadcast_to(ms[i] + jnp.log(dens[i]), outs[i].shape) for i in mine], axis=1)

    if dil > ATT_UNITS:
        def group(gi, carry):
            run([(0, gi * ATT_UNITS + r) for r in range(ATT_UNITS)])
            return carry

        lax.fori_loop(0, dil // ATT_UNITS, group, 0)
    else:
        run([(b, r) for b in range(nq) for r in range(dil)])


def _attn_prompt(ua, bias, *, g, dil):
    bsz, s, _ = ua.shape
    nq = max(ATT_UNITS // dil, 1)
    prev_rows = ATT_SPAN * dil
    rows = nq * prev_rows
    tiles = ATT_W // LANES

    def col(which):
        return lambda b, hp, i: (b, i, (which * N_DIL + g) * tiles + hp)

    def col_prev(which):
        return lambda b, hp, i: (b, jnp.maximum(i * nq - 1, 0), (which * N_DIL + g) * tiles + hp)

    blk = (1, rows, LANES)
    prev = (1, prev_rows, LANES)
    return pl.pallas_call(
        functools.partial(_attn_prompt_kernel, dil=dil, nq=nq),
        grid=(bsz, tiles, s // rows),
        in_specs=[pl.BlockSpec(blk, col(0)), pl.BlockSpec(blk, col(1)), pl.BlockSpec(prev, col_prev(1)),
                  pl.BlockSpec(blk, col(2)), pl.BlockSpec(prev, col_prev(2)),
                  pl.BlockSpec((2, HEADS_PER_TILE, ATT_SPAN, 2 * ATT_SPAN), lambda b, hp, i: (0, hp, 0, 0))],
        out_specs=[pl.BlockSpec(blk, lambda b, hp, i: (b, i, hp))] * 2,
        out_shape=[jax.ShapeDtypeStruct((bsz, s, ATT_W), F32)] * 2,
        compiler_params=_cparams("parallel", "parallel", "parallel"),
        name="attn_prompt",
    )(ua, ua, ua, ua, ua, bias)


def _attn_sample_kernel(u_ref, cache_ref, bo_ref, bn_ref, *rest, g):
    o_ref, lse_ref, new_ref = rest[-3:]
    t = u_ref.shape[1]
    rows = cache_ref.shape[-1]
    scale = HEAD_DIM ** -0.5
    base = g * ATT_W
    lane = lax.broadcasted_iota(jnp.int32, (HEAD_DIM, LANES), 1)
    place = (lax.broadcasted_iota(jnp.int32, (t, LANES), 1)
             == lax.broadcasted_iota(jnp.int32, (t, LANES), 0) + (LANES - t)).astype(BF16)
    for h in range(ATT_HG):
        sl = slice(h * HEAD_DIM, (h + 1) * HEAD_DIM)
        col = base + h * HEAD_DIM
        q = u_ref[0, :, col:col + HEAD_DIM].astype(BF16)
        k_new = u_ref[0, :, N_DIL * ATT_W + col:N_DIL * ATT_W + col + HEAD_DIM]
        v_new = u_ref[0, :, 2 * N_DIL * ATT_W + col:2 * N_DIL * ATT_W + col + HEAD_DIM]
        k_t = cache_ref[0, 0, 0, h]
        v_t = cache_ref[0, 0, 1, h]
        s_old = _dot(q, k_t.astype(BF16)) * scale + bo_ref[h]
        s_new = _dot(q, k_new.astype(BF16), NT) * scale + bn_ref[h]
        m = jnp.maximum(jnp.max(s_old, axis=-1, keepdims=True), jnp.max(s_new, axis=-1, keepdims=True))
        e_old = jnp.exp(s_old - m)
        e_new = jnp.exp(s_new - m)
        den = jnp.sum(e_old, axis=-1, keepdims=True) + jnp.sum(e_new, axis=-1, keepdims=True)
        o = _dot((e_old / den).astype(BF16), v_t.astype(BF16), NT)
        p_new = e_new / den
        for j in range(t):
            o = o + p_new[:, j:j + 1] * v_new[j:j + 1, :]
        o_ref[0, :, sl] = o
        lse_ref[0, :, sl] = jnp.broadcast_to(m + jnp.log(den), o.shape)
        for kv, (old, new) in enumerate(((k_t, k_new), (v_t, v_new))):
            moved = pltpu.roll(old, rows - t, axis=1)
            tail = jnp.where(lane >= LANES - t, _mm(new, place, TN, 3, 1), moved[:, rows - LANES:])
            if rows > LANES:
                new_ref[0, 0, kv, h, :, 0:rows - LANES] = moved[:, 0:rows - LANES]
            new_ref[0, 0, kv, h, :, rows - LANES:rows] = tail


def _attn_sample(ua, cache_t, prev, bias_old, bias_new, *, g, layer):
    bsz, t, _ = ua.shape
    out = pl.BlockSpec((1, t, ATT_W), lambda b: (b, 0, 0))
    cb = pl.BlockSpec((1, 1) + cache_t.shape[2:], lambda b: (layer, b, 0, 0, 0, 0))
    full = lambda a: pl.BlockSpec(a.shape, lambda b: (0,) * a.ndim)
    in_specs = [pl.BlockSpec((1, t, ATT_QKV), lambda b: (b, 0, 0)), cb, full(bias_old), full(bias_new)]
    args = [ua, cache_t, bias_old, bias_new]
    aliases = {}
    if prev is not None:
        in_specs.append(pl.BlockSpec(memory_space=pl.ANY))
        args.append(prev)
        aliases = {len(args) - 1: 2}
    return pl.pallas_call(
        functools.partial(_attn_sample_kernel, g=g),
        grid=(bsz,),
        in_specs=in_specs,
        out_specs=[out, out, cb],
        out_shape=[jax.ShapeDtypeStruct((bsz, t, ATT_W), F32)] * 2
                  + [jax.ShapeDtypeStruct(cache_t.shape, F32)],
        input_output_aliases=aliases,
        compiler_params=_cparams("parallel"),
        name="attn_sample",
    )(*args)


def _rel_bucket(dist):
    max_exact = N_BUCKETS // 2
    large = max_exact + (jnp.log(jnp.maximum(dist, 1).astype(F32) / max_exact)
                         / math.log(MAX_DISTANCE / max_exact) * (N_BUCKETS - max_exact)).astype(jnp.int32)
    return jnp.where(dist < max_exact, dist, jnp.minimum(large, N_BUCKETS - 1))


def _bias_lookup(tab, dist):
    onehot = (_rel_bucket(jnp.asarray(dist))[..., None] == jnp.arange(N_BUCKETS)).astype(F32)
    return jnp.einsum("...b,bh->h...", onehot, tab, precision=lax.Precision.HIGHEST)


def _prompt_bias(tab, dil):
    span = ATT_SPAN
    qi = np.arange(span)[:, None]
    ki = np.arange(2 * span)[None, :]
    j = span + qi - ki
    band = (j >= 0) & (j <= span)
    bias = _bias_lookup(tab, dil * np.clip(j, 0, span))
    rest = jnp.where(band[None], bias, NEG_INF)
    first = jnp.where((band & (ki >= span))[None], bias, NEG_INF)
    return jnp.stack([first, rest])


def _sample_bias(tab, dil, rows, t):
    rho = np.arange(rows + t)[None, :]
    d = rows + np.arange(t)[:, None] - rho
    valid = (d >= 0) & (d % dil == 0) & (d // dil <= ATT_SPAN)
    bias = jnp.where(valid[None], _bias_lookup(tab, np.clip(d, 0, None)), NEG_INF)
    return bias[:, :, :rows], bias[:, :, rows:]


def _merge_kernel(x_ref, yc_ref, yr_ref, o0_ref, o1_ref, o2_ref, l0_ref, l1_ref, l2_ref, gates_ref,
                  wc_ref, wr_ref, wa_ref, wo_ref, g_ref, b_ref, rt_ref, h_ref, gate_ref, eid_ref, *, alpha):
    l0, l1, l2 = l0_ref[...], l1_ref[...], l2_ref[...]
    m = jnp.maximum(jnp.maximum(l0, l1), l2)
    e0, e1, e2 = jnp.exp(l0 - m), jnp.exp(l1 - m), jnp.exp(l2 - m)
    den = e0 + e1 + e2
    o = o0_ref[...] * (e0 / den) + o1_ref[...] * (e1 / den) + o2_ref[...] * (e2 / den)
    d = D_MODEL
    merged = (gates_ref[:, 0:d] * _dot(yc_ref[...], wc_ref[...])
              + gates_ref[:, d:2 * d] * _dot(yr_ref[...], wr_ref[...])
              + gates_ref[:, 2 * d:3 * d] * _dot(o.astype(BF16), wa_ref[...]))
    pre = alpha * x_ref[...] + _dot(merged.astype(BF16), wo_ref[...])
    h = _layer_norm(pre, g_ref[...], b_ref[...], LN_EPS)
    for c in range(ROW_TILE):
        h_ref[pl.ds(c, h.shape[0], stride=ROW_TILE), :] = h[:, c * LANES:(c + 1) * LANES]
    logits = _mm(h, rt_ref[...], NN, 3, 3)
    lane = lax.broadcasted_iota(jnp.int32, logits.shape, 1)
    big = jnp.int32(ROUTER_PAD)
    is_grp = lane < N_GROUPS
    lg = jnp.where(is_grp, logits, NEG_INF)
    m_g = jnp.max(lg, axis=-1, keepdims=True)
    grp = jnp.min(jnp.where(lg == m_g, lane, big), axis=-1, keepdims=True)
    p_grp = 1.0 / jnp.sum(jnp.where(is_grp, jnp.exp(logits - m_g), 0.0), axis=-1, keepdims=True)
    lo_lane = N_GROUPS + grp * EXPERTS_PER_GROUP
    in_grp = (lane >= lo_lane) & (lane < lo_lane + EXPERTS_PER_GROUP)
    le = jnp.where(in_grp, logits, NEG_INF)
    v1 = jnp.max(le, axis=-1, keepdims=True)
    i1 = jnp.min(jnp.where(le == v1, lane, big), axis=-1, keepdims=True)
    le = jnp.where(lane == i1, NEG_INF, le)
    v2 = jnp.max(le, axis=-1, keepdims=True)
    i2 = jnp.min(jnp.where(le == v2, lane, big), axis=-1, keepdims=True)
    e2 = jnp.exp(v2 - v1)
    den = 1.0 + e2
    gate_ref[...] = jnp.where(lane == 0, p_grp * (1.0 / den), jnp.where(lane == 1, p_grp * (e2 / den), 0.0))
    eid_ref[...] = jnp.where(lane == 0, i1 - N_GROUPS, jnp.where(lane == 1, i2 - N_GROUPS, 0))


def _merge(x, yc, yr, os_, ls_, gates, wc, wr, wa, wo, g, b, rt, *, alpha, tm):
    n = x.shape[0]
    row = lambda c: pl.BlockSpec((tm, c), lambda i: (i, 0))
    full = lambda a: pl.BlockSpec(a.shape, lambda i: (0,) * a.ndim)
    return pl.pallas_call(
        functools.partial(_merge_kernel, alpha=alpha),
        grid=(n // tm,),
        in_specs=[row(D_MODEL), row(CONV_DIM), row(RWKV_C)] + [row(ATT_W)] * 6 + [row(N_BRANCH * D_MODEL)]
                 + [full(a) for a in (wc, wr, wa, wo, g, b, rt)],
        out_specs=[pl.BlockSpec((tm * ROW_TILE, LANES), lambda i: (i, 0)), row(ROUTER_PAD), row(ROUTER_PAD)],
        out_shape=[jax.ShapeDtypeStruct((n * ROW_TILE, LANES), F32), jax.ShapeDtypeStruct((n, ROUTER_PAD), F32),
                   jax.ShapeDtypeStruct((n, ROUTER_PAD), jnp.int32)],
        compiler_params=_cparams("parallel"),
        name="merge",
    )(x, yc, yr, *os_, *ls_, gates, wc, wr, wa, wo, g, b, rt)


assert TOP_K & (TOP_K - 1) == 0
ROW_TILE = D_MODEL // LANES


def _expert_kernel(be_ref, off_ref, cnt_ref, order_ref, h_hbm, wg_ref, wu_ref, wd_ref, out_hbm,
                   xbuf, ybuf, gsem, ssem):
    i = pl.program_id(0)
    last = pl.num_programs(0) - 1
    slot = i & 1
    other = 1 - slot
    n_out = out_hbm.shape[0] // ROW_TILE - 2 * MOE_BLOCK
    block_rows = MOE_BLOCK * ROW_TILE

    def tile(row):
        return pl.ds(pl.multiple_of(row * ROW_TILE, ROW_TILE), ROW_TILE)

    def chunk(c):
        return pl.ds(c, MOE_BLOCK, stride=ROW_TILE)

    def token_of(assignment):
        return lax.shift_right_logical(assignment, TOP_K.bit_length() - 1)

    def gather_wait(s):
        pltpu.make_async_copy(h_hbm.at[pl.ds(0, block_rows)], xbuf.at[s], gsem.at[s]).wait()

    def scatter_wait(s):
        pltpu.make_async_copy(ybuf.at[s], out_hbm.at[pl.ds(0, block_rows)], ssem.at[s]).wait()

    active = cnt_ref[i] > 0
    prev = jnp.maximum(i - 1, 0)
    prev_active = (i > 0) & (cnt_ref[prev] > 0)

    @pl.when(i == 0)
    def _():
        ybuf[...] = jnp.zeros(ybuf.shape, F32)
        pltpu.make_async_copy(ybuf.at[0], out_hbm.at[pl.ds(n_out * ROW_TILE, block_rows)], ssem.at[0]).start()
        base = off_ref[0]
        for r in range(MOE_BLOCK):
            tok = token_of(order_ref[base + r])
            pltpu.make_async_copy(h_hbm.at[tile(tok)], xbuf.at[0, tile(r)], gsem.at[0]).start()

    @pl.when(active | prev_active)
    def _():
        gather_wait(slot)
        scatter_wait(slot)
        nxt_base = off_ref[jnp.minimum(i + 1, last)]
        prev_base = off_ref[prev]
        prev_cnt = jnp.where(i > 0, cnt_ref[prev], 0)
        for r in range(MOE_BLOCK):
            tok = token_of(order_ref[nxt_base + r])
            pltpu.make_async_copy(h_hbm.at[tile(tok)], xbuf.at[other, tile(r)], gsem.at[other]).start()
            dst = jnp.where(r < prev_cnt, order_ref[prev_base + r], n_out + other * MOE_BLOCK + r)
            pltpu.make_async_copy(ybuf.at[other, tile(r)], out_hbm.at[tile(dst)], ssem.at[other]).start()
        gate = jnp.zeros((MOE_BLOCK, D_EXPERT), F32)
        up = jnp.zeros((MOE_BLOCK, D_EXPERT), F32)
        for c in range(ROW_TILE):
            xc = xbuf[slot, chunk(c), :].astype(BF16)
            gate = gate + _dot(xc, wg_ref[c * LANES:(c + 1) * LANES, :])
            up = up + _dot(xc, wu_ref[c * LANES:(c + 1) * LANES, :])
        act = (gate * _sigmoid(gate) * up).astype(BF16)
        for c in range(ROW_TILE):
            ybuf[slot, chunk(c), :] = _dot(act, wd_ref[:, c * LANES:(c + 1) * LANES])

    @pl.when(prev_active & jnp.logical_not(active))
    def _():
        gather_wait(other)
        scatter_wait(other)


def _experts(ht, block_e, block_off, block_cnt, order, wg, wu, wd, *, layer):
    n = ht.shape[0] // ROW_TILE
    n_blocks = block_e.shape[0]
    wmap = lambda i, be, off, cnt, order: (layer, be[i], 0, 0)
    grid_spec = pltpu.PrefetchScalarGridSpec(
        num_scalar_prefetch=4,
        grid=(n_blocks,),
        in_specs=[pl.BlockSpec(memory_space=pl.ANY),
                  pl.BlockSpec((None, None, D_MODEL, D_EXPERT), wmap),
                  pl.BlockSpec((None, None, D_MODEL, D_EXPERT), wmap),
                  pl.BlockSpec((None, None, D_EXPERT, D_MODEL), wmap)],
        out_specs=pl.BlockSpec(memory_space=pl.ANY),
        scratch_shapes=[pltpu.VMEM((2, MOE_BLOCK * ROW_TILE, LANES), F32),
                        pltpu.VMEM((2, MOE_BLOCK * ROW_TILE, LANES), F32),
                        pltpu.SemaphoreType.DMA((2,)), pltpu.SemaphoreType.DMA((2,))],
    )
    return pl.pallas_call(
        _expert_kernel,
        grid_spec=grid_spec,
        out_shape=jax.ShapeDtypeStruct(((TOP_K * n + 2 * MOE_BLOCK) * ROW_TILE, LANES), F32),
        compiler_params=_cparams("arbitrary"),
        name="experts",
    )(block_e, block_off, block_cnt, order, ht, wg, wu, wd)


def _route(eid):
    n = eid.shape[0]
    a = n * TOP_K
    n_blocks = -(-a // MOE_BLOCK) + N_EXPERTS
    skey = jnp.sort(eid.reshape(-1) * a + jnp.arange(a, dtype=jnp.int32))
    order = skey % a
    experts = jnp.arange(N_EXPERTS, dtype=jnp.int32)
    start = jnp.sum(skey[None, :] < (experts * a)[:, None], axis=1, dtype=jnp.int32)
    counts = jnp.sum(skey[None, :] < ((experts + 1) * a)[:, None], axis=1, dtype=jnp.int32) - start
    padded = (counts + MOE_BLOCK - 1) // MOE_BLOCK * MOE_BLOCK
    pend = jnp.cumsum(padded)
    pstart = pend - padded
    blk_start = jnp.arange(n_blocks, dtype=jnp.int32) * MOE_BLOCK
    block_e = jnp.minimum(jnp.sum(pend[None, :] <= blk_start[:, None], axis=1, dtype=jnp.int32), N_EXPERTS - 1)
    pick = (block_e[:, None] == experts[None, :]).astype(jnp.int32)
    within = blk_start - jnp.sum(pick * pstart[None, :], axis=1)
    block_cnt = jnp.where(blk_start < pend[-1],
                          jnp.clip(jnp.sum(pick * counts[None, :], axis=1) - within, 0, MOE_BLOCK), 0)
    block_off = jnp.where(block_cnt > 0, jnp.sum(pick * start[None, :], axis=1) + within, 0)
    order = jnp.concatenate([order, jnp.zeros((MOE_BLOCK,), jnp.int32)])
    return block_e, block_off.astype(jnp.int32), block_cnt.astype(jnp.int32), order


def _ln2_kernel(h_ref, f_ref, gate_ref, g_ref, b_ref, y_ref, yb_ref, *, alpha):
    tm = y_ref.shape[0]
    g0, g1 = gate_ref[:, 0:1], gate_ref[:, 1:2]
    pre = []
    for c in range(ROW_TILE):
        f0 = f_ref[pl.ds(c, tm, stride=TOP_K * ROW_TILE), :]
        f1 = f_ref[pl.ds(ROW_TILE + c, tm, stride=TOP_K * ROW_TILE), :]
        pre.append(alpha * h_ref[pl.ds(c, tm, stride=ROW_TILE), :] + (f0 * g0 + f1 * g1))
    mu = sum(jnp.sum(p, axis=-1, keepdims=True) for p in pre) * (1.0 / D_MODEL)
    cen = [p - mu for p in pre]
    var = sum(jnp.sum(p * p, axis=-1, keepdims=True) for p in cen) * (1.0 / D_MODEL)
    rstd = lax.rsqrt(var + LN_EPS)
    for c in range(ROW_TILE):
        sl = slice(c * LANES, (c + 1) * LANES)
        y = cen[c] * rstd * g_ref[:, sl] + b_ref[:, sl]
        y_ref[:, sl] = y
        yb_ref[:, sl] = y.astype(BF16)


def _ln2(ht, ft, gate, g, b, *, alpha, tm):
    n = ht.shape[0] // ROW_TILE
    vec = pl.BlockSpec((1, D_MODEL), lambda i: (0, 0))
    return pl.pallas_call(
        functools.partial(_ln2_kernel, alpha=alpha),
        grid=(n // tm,),
        in_specs=[pl.BlockSpec((tm * ROW_TILE, LANES), lambda i: (i, 0)),
                  pl.BlockSpec((tm * TOP_K * ROW_TILE, LANES), lambda i: (i, 0)),
                  pl.BlockSpec((tm, ROUTER_PAD), lambda i: (i, 0)), vec, vec],
        out_specs=[pl.BlockSpec((tm, D_MODEL), lambda i: (i, 0))] * 2,
        out_shape=[jax.ShapeDtypeStruct((n, D_MODEL), F32), jax.ShapeDtypeStruct((n, D_MODEL), BF16)],
        compiler_params=_cparams("parallel"),
        name="ln2",
    )(ht, ft, gate, g, b)


def _row(v):
    return v.reshape(1, -1)


def _layer(x, xb, p, conv_state, shift, wkv, caches, *, alpha, prompt, layer):
    bsz, t, d = x.shape
    n = bsz * t
    tm = 512 if prompt else n
    xb2 = xb.reshape(n, d)
    o1 = 2 * CONV_DIM
    o2 = o1 + RWKV_U
    o3 = o2 + ATT_QKV
    tm_in = min(n, 1024)
    z = _proj(xb2, p["w_in"], p["b_in"], layer=layer, col0=0, n_cols=o1, epilogue="glu", tn=D_MODEL, tm=tm_in)
    gates = _proj(xb2, p["w_in"], p["b_in"], layer=layer, col0=o3, n_cols=N_BRANCH * D_MODEL,
                  epilogue="sigmoid_bias", tn=D_MODEL, tm=tm_in)
    ur = _proj(xb2, p["w_rwkv"], p["b_in"], layer=layer, col0=0, n_cols=RWKV_U, epilogue="none",
               tn=RWKV_U, tm=min(n, 512))
    ua = _proj(xb2, p["w_att"], p["b_in"], layer=layer, col0=0, n_cols=ATT_QKV, epilogue="none",
               tn=ATT_QKV // 2, tm=min(n, 512))
    z = z.reshape(bsz, t, CONV_DIM)
    ur = ur.reshape(bsz, t, RWKV_U)
    ua = ua.reshape(bsz, t, ATT_QKV)

    hist = jnp.pad(conv_state, ((0, 0), (CONV_HIST - (CONV_K - 1), 0), (0, 0)))
    yc = _conv(z, hist, p["conv_dw"], p["conv_dw_b"], p["conv_ln_g"], p["conv_ln_b"], tt=256 if prompt else t)
    conv_new = jnp.concatenate([conv_state, z], axis=1)[:, -(CONV_K - 1):]

    shift8 = jnp.pad(shift[:, None, :], ((0, 0), (SUBLANES - 1, 0), (0, 0)))
    pre = _rwkv_pre(ur, shift8, p["rwkv_mu"], p["rwkv_w0"], p["rwkv_w2p"], p["rwkv_a0"], p["rwkv_a2p"],
                    p["rwkv_g2"], p["rwkv_kk"], p["rwkv_ka"], p["rwkv_rk"], p["ones_bd"],
                    tt=256 if prompt else t)
    chunk = 64 if prompt else SUBLANES
    if t % chunk:
        pre = [jnp.pad(a, ((0, 0), (0, chunk - t % chunk), (0, 0))) for a in pre]
    yr, wkv_new = _scan(*pre, wkv, p["rwkv_ln_g"], p["rwkv_ln_b"], chunk=chunk)
    yr = yr[:, :t]
    shift_new = ur[:, -1]

    os_, ls_, att_new = [], [], []
    for g, (window, dil) in enumerate(DILATIONS):
        if prompt:
            o, lse = _attn_prompt(ua, p["att_bias"][g], g=g, dil=dil)
            rows = min(window, t)
            kv = [ua[:, t - rows:, (w * N_DIL + g) * ATT_W:(w * N_DIL + g + 1) * ATT_W] for w in (1, 2)]
            att_new.append(jnp.stack([a.reshape(bsz, rows, ATT_HG, HEAD_DIM) for a in kv], axis=1))
        else:
            cache_t, prev = caches[g]
            o, lse, new = _attn_sample(ua, cache_t, prev, *p["att_bias"][g], g=g, layer=layer)
            att_new.append(new)
        os_.append(o.reshape(n, ATT_W))
        ls_.append(lse.reshape(n, ATT_W))

    ht, gate, eid = _merge(x.reshape(n, d), yc.reshape(n, CONV_DIM), yr.reshape(n, RWKV_C), os_, ls_, gates,
                           p["conv_proj"], p["rwkv_proj"], p["attn_proj"], p["w_out"], p["ln1_g"], p["ln1_b"],
                           p["router"], alpha=alpha, tm=256 if prompt else n)

    ft = _experts(ht, *_route(eid[:, :TOP_K]), p["moe_w_gate"], p["moe_w_up"], p["moe_w_down"], layer=layer)
    y, yb = _ln2(ht, ft, gate, p["ln2_g"], p["ln2_b"], alpha=alpha, tm=tm)
    return y.reshape(bsz, t, d), yb.reshape(bsz, t, d), conv_new, shift_new, wkv_new, att_new


def kernel(x_prompt, x_sample, state_conv, state_shift, state_wkv, cache_attn_w128, cache_attn_w512, cache_attn_w2048, rel_bias, w_in, b_gate, conv_dw, conv_dw_b, conv_ln_g, conv_ln_b, conv_proj, rwkv_mu, rwkv_w0, rwkv_w2, rwkv_a0, rwkv_a2, rwkv_g2, rwkv_kk, rwkv_ka, rwkv_rk, rwkv_ln_g, rwkv_ln_b, rwkv_proj, attn_proj, w_out, ln1_g, ln1_b, router_group, router_expert, moe_w_gate, moe_w_up, moe_w_down, ln2_g, ln2_b):
    depth = w_in.shape[0]
    alpha = (2 * depth) ** 0.25
    bp, tp, _ = x_prompt.shape
    ts = x_sample.shape[1]
    caches_t = [jnp.transpose(c, (0, 1, 2, 4, 5, 3)) for c in (cache_attn_w128, cache_attn_w512, cache_attn_w2048)]
    new_caches = [None] * N_DIL
    head = np.arange(RWKV_C) // RWKV_N
    ones_bd = jnp.asarray(head[:, None] == head[None, :], BF16)
    tabs = [rel_bias[:, g * ATT_HG:(g + 1) * ATT_HG] for g in range(N_DIL)]
    bias_p = [_prompt_bias(tabs[g], dil) for g, (_, dil) in enumerate(DILATIONS)]
    bias_s = [_sample_bias(tabs[g], dil, caches_t[g].shape[-1], ts) for g, (_, dil) in enumerate(DILATIONS)]

    w_in_b = w_in.astype(BF16)
    o_rwkv = 2 * CONV_DIM
    o_att = o_rwkv + RWKV_U
    w_rwkv_b = w_in_b[:, :, o_rwkv:o_att]
    w_att_b = w_in_b[:, :, o_att:o_att + ATT_QKV]
    b_in =jnp.pad(b_gate, ((0, 0), (w_in.shape[2] - b_gate.shape[1], 0)))[:, None, :]
    moe_b = [w.astype(BF16) for w in (moe_w_gate, moe_w_up, moe_w_down)]

    xp, xs = x_prompt, x_sample
    xpb, xsb = xp.astype(BF16), xs.astype(BF16)
    outs_p, outs_s = [], []
    for l in range(depth):
        zeros_lora = jnp.zeros((LORA_W, RWKV_C), F32)
        router = jnp.concatenate(
            [router_group[l], router_expert[l],
             jnp.zeros((D_MODEL, ROUTER_PAD - N_GROUPS - N_EXPERTS), F32)], axis=1)
        p = dict(
            w_in=w_in_b, b_in=b_in, w_rwkv=w_rwkv_b, w_att=w_att_b,
            conv_dw=conv_dw[l], conv_dw_b=_row(conv_dw_b[l]), conv_ln_g=_row(conv_ln_g[l]),
            conv_ln_b=_row(conv_ln_b[l]), conv_proj=conv_proj[l].astype(BF16),
            rwkv_mu=_row(rwkv_mu[l]), rwkv_w0=_row(rwkv_w0[l]),
            rwkv_w2p=jnp.concatenate([rwkv_w2[l], zeros_lora], axis=0).astype(BF16),
            rwkv_a0=_row(rwkv_a0[l]),
            rwkv_a2p=jnp.concatenate([zeros_lora, rwkv_a2[l]], axis=0).astype(BF16),
            rwkv_g2=rwkv_g2[l].astype(BF16), rwkv_kk=_row(rwkv_kk[l]), rwkv_ka=_row(rwkv_ka[l]),
            rwkv_rk=_row(rwkv_rk[l]), rwkv_ln_g=_row(rwkv_ln_g[l]), rwkv_ln_b=_row(rwkv_ln_b[l]),
            rwkv_proj=rwkv_proj[l].astype(BF16), attn_proj=attn_proj[l].astype(BF16),
            w_out=w_out[l].astype(BF16), ln1_g=_row(ln1_g[l]), ln1_b=_row(ln1_b[l]), router=router,
            moe_w_gate=moe_b[0], moe_w_up=moe_b[1], moe_w_down=moe_b[2],
            ln2_g=_row(ln2_g[l]), ln2_b=_row(ln2_b[l]),
            ones_bd=ones_bd)
        xp, xpb, *new_p = _layer(
            xp, xpb, dict(p, att_bias=bias_p),
            jnp.zeros((bp, CONV_K - 1, CONV_DIM), F32), jnp.zeros((bp, RWKV_U), F32),
            jnp.zeros((bp, RWKV_H, RWKV_N, RWKV_N), F32), None, alpha=alpha, prompt=True, layer=l)
        xs, xsb, *new_s = _layer(
            xs, xsb, dict(p, att_bias=bias_s), state_conv[l], state_shift[l], state_wkv[l],
            list(zip(caches_t, new_caches)), alpha=alpha, prompt=False, layer=l)
        new_caches = new_s[3]
        outs_p.append(new_p)
        outs_s.append(new_s)

    def stack(outs, i):
        return jnp.stack([o[i] for o in outs])

    att_p = [jnp.stack([o[3][g] for o in outs_p]) for g in range(N_DIL)]
    att_s = [jnp.transpose(c, (0, 1, 2, 5, 3, 4)) for c in new_caches]
    return (xp, xs, stack(outs_p, 0), stack(outs_s, 0), stack(outs_p, 1), stack(outs_s, 1),
            stack(outs_p, 2), stack(outs_s, 2),
            att_p[0], att_s[0], att_p[1], att_s[1], att_p[2], att_s[2])
```

```python
import functools
import math

import jax
import jax.numpy as jnp
import numpy as np
from jax import lax
from jax.experimental import pallas as pl
from jax.experimental.pallas import tpu as pltpu

F32 = jnp.float32
BF16 = jnp.bfloat16

D_MODEL = 1024
CONV_DIM = D_MODEL // 2
CONV_K = 31
RWKV_N = 64
RWKV_H = D_MODEL // 128
RWKV_C = RWKV_H * RWKV_N
LORA_W, LORA_A, LORA_G = 64, 64, 128
RWKV_U = 3 * RWKV_C + LORA_W + LORA_A + LORA_G
RWKV_LN_EPS = 64e-5
HEAD_DIM = 64
ATT_HG = 4
ATT_W = ATT_HG * HEAD_DIM
DILATIONS = ((128, 1), (512, 4), (2048, 16))
N_DIL = len(DILATIONS)
ATT_QKV = 3 * N_DIL * ATT_W
N_BUCKETS = 32
MAX_DISTANCE = 2048
N_BRANCH = 3
N_GROUPS = 4
EXPERTS_PER_GROUP = 8
N_EXPERTS = N_GROUPS * EXPERTS_PER_GROUP
TOP_K = 2
D_EXPERT = D_MODEL // 2
MOE_BLOCK = 128
LN_EPS = 1e-5
NEG_INF = -1e30
ROUTER_PAD = 128

LANES = 128
SUBLANES = 8
VMEM_LIMIT = 48 * 1024 * 1024

NN = (((1,), (0,)), ((), ()))
NT = (((1,), (1,)), ((), ()))
TN = (((0,), (0,)), ((), ()))


def _cparams(*sem):
    return pltpu.CompilerParams(dimension_semantics=sem, vmem_limit_bytes=VMEM_LIMIT)


def _dot(a, b, dims=NN):
    return lax.dot_general(a, b, dims, preferred_element_type=F32)


def _pieces(x, n):
    if x.dtype == BF16:
        return [x]
    out, r = [], x
    for i in range(n):
        p = r.astype(BF16)
        out.append(p)
        if i + 1 < n:
            r = r - p.astype(F32)
    return out


def _mm(a, b, dims=NN, pa=1, pb=1):
    ap, bp = _pieces(a, pa), _pieces(b, pb)
    order = max(len(ap), len(bp))
    acc = None
    for i in reversed(range(len(ap))):
        for j in reversed(range(len(bp))):
            if i + j < order:
                t = _dot(ap[i], bp[j], dims)
                acc = t if acc is None else acc + t
    return acc


def _sigmoid(x):
    return 1.0 / (1.0 + jnp.exp(-x))


def _layer_norm(x, g, b, eps):
    mu = jnp.mean(x, axis=-1, keepdims=True)
    xc = x - mu
    var = jnp.mean(xc * xc, axis=-1, keepdims=True)
    return xc * lax.rsqrt(var + eps) * g + b


def _proj_kernel(x_ref, w_ref, b_ref, o_ref, *, epilogue):
    acc = _dot(x_ref[...], w_ref[...])
    if epilogue == "glu":
        half = acc.shape[1] // 2
        o_ref[...] = acc[:, :half] * _sigmoid(acc[:, half:])
    elif epilogue == "sigmoid_bias":
        o_ref[...] = _sigmoid(acc + b_ref[...])
    else:
        o_ref[...] = acc


def _proj(x, w_all, b_all, *, layer, col0, n_cols, epilogue, tn, tm):
    n, k = x.shape
    out_cols = n_cols // 2 if epilogue == "glu" else n_cols
    out_tn = tn // 2 if epilogue == "glu" else tn
    c0 = col0 // tn
    mt, nt = n // tm, n_cols // tn
    rows_outer = mt * n_cols + n <= nt * n + n_cols
    grid = (mt, nt) if rows_outer else (nt, mt)
    ij = (lambda a, b: (a, b)) if rows_outer else (lambda a, b: (b, a))
    return pl.pallas_call(
        functools.partial(_proj_kernel, epilogue=epilogue),
        grid=grid,
        in_specs=[pl.BlockSpec((tm, k), lambda a, b: (ij(a, b)[0], 0)),
                  pl.BlockSpec((None, k, tn), lambda a, b: (layer, 0, c0 + ij(a, b)[1])),
                  pl.BlockSpec((None, 1, tn), lambda a, b: (layer, 0, c0 + ij(a, b)[1]))],
        out_specs=pl.BlockSpec((tm, out_tn), lambda a, b: ij(a, b)),
        out_shape=jax.ShapeDtypeStruct((n, out_cols), F32),
        compiler_params=_cparams("parallel", "parallel"),
        name="in_proj_" + epilogue,
    )(x, w_all, b_all)


CONV_HIST = 32
CONV_ROWS = 64


def _conv_kernel(z_ref, hist_ref, dw_ref, dwb_ref, g_ref, b_ref, o_ref, zbuf):
    tt = z_ref.shape[1]
    win = CONV_HIST + tt

    @pl.when(pl.program_id(1) == 0)
    def _():
        zbuf[0, 0:CONV_HIST, :] = hist_ref[0]

    @pl.when(pl.program_id(1) != 0)
    def _():
        zbuf[0, 0:CONV_HIST, :] = zbuf[0, tt:tt + CONV_HIST, :]

    zbuf[0, CONV_HIST:win, :] = z_ref[0]
    if tt % SUBLANES == 0:
        for s in range(1, SUBLANES):
            zbuf[s, 0:win - SUBLANES, :] = zbuf[0, s:s + win - SUBLANES, :]
    pad = CONV_HIST - (CONV_K - 1)
    rb = min(tt, CONV_ROWS)
    for r0 in range(0, tt, rb):
        acc = jnp.zeros((rb, CONV_DIM), F32)
        for k in range(CONV_K):
            off = r0 + pad + k
            if tt % SUBLANES == 0:
                tap = zbuf[off % SUBLANES, off - off % SUBLANES:off - off % SUBLANES + rb, :]
            else:
                tap = zbuf[0, off:off + rb, :]
            acc = acc + tap * dw_ref[k:k + 1, :]
        y = _layer_norm(acc + dwb_ref[...], g_ref[...], b_ref[...], LN_EPS)
        o_ref[0, r0:r0 + rb, :] = (y * _sigmoid(y)).astype(BF16)


def _conv(z, hist, dw, dwb, g, b, *, tt):
    bsz, t, _ = z.shape
    vec = pl.BlockSpec((1, CONV_DIM), lambda i, j: (0, 0))
    return pl.pallas_call(
        _conv_kernel,
        grid=(bsz, t // tt),
        in_specs=[pl.BlockSpec((1, tt, CONV_DIM), lambda i, j: (i, j, 0)),
                  pl.BlockSpec((1, CONV_HIST, CONV_DIM), lambda i, j: (i, 0, 0)),
                  pl.BlockSpec((CONV_K, CONV_DIM), lambda i, j: (0, 0)),
                  vec, vec, vec],
        out_specs=pl.BlockSpec((1, tt, CONV_DIM), lambda i, j: (i, j, 0)),
        out_shape=jax.ShapeDtypeStruct((bsz, t, CONV_DIM), BF16),
        scratch_shapes=[pltpu.VMEM((SUBLANES, CONV_HIST + tt, CONV_DIM), F32)],
        compiler_params=_cparams("parallel", "arbitrary"),
        name="conv",
    )(z, hist, dw, dwb, g, b)


def _head_sum(x, ones_bd):
    return _mm(x, ones_bd, NN, pa=3, pb=1)


def _rwkv_pre_kernel(u_ref, shift_ref, mu_ref, w0_ref, w2_ref, a0_ref, a2_ref, g2_ref, kk_ref, ka_ref,
                     rk_ref, ones_ref,
                     r_ref, k_ref, v_ref, lw_ref, al_ref, be_ref, gate_ref, bonus_ref, ubuf):
    tt = u_ref.shape[1]
    c = RWKV_C

    @pl.when(pl.program_id(1) == 0)
    def _():
        ubuf[0:SUBLANES, :] = shift_ref[0]

    @pl.when(pl.program_id(1) != 0)
    def _():
        ubuf[0:SUBLANES, :] = ubuf[tt:tt + SUBLANES, :]

    u = u_ref[0]
    ubuf[SUBLANES:SUBLANES + tt, :] = u
    u_prev = ubuf[SUBLANES - 1:SUBLANES - 1 + tt, :]
    um = u + (u_prev - u) * mu_ref[...]
    r, k, v = um[:, 0:c], um[:, c:2 * c], um[:, 2 * c:3 * c]
    lo = um[:, 3 * c:3 * c + LORA_W + LORA_A]
    lane = lax.broadcasted_iota(jnp.int32, lo.shape, 1)
    lo = jnp.where(lane < LORA_W, jnp.tanh(lo), lo).astype(BF16)
    g_lo = _sigmoid(um[:, 3 * c + LORA_W + LORA_A:]).astype(BF16)
    xw = -(w0_ref[...] + _dot(lo, w2_ref[...]))
    softplus = jnp.maximum(xw, 0.0) + jnp.log(1.0 + jnp.exp(-jnp.abs(xw)))
    lw_ref[0] = -jnp.exp(-softplus - 0.5)
    a = _sigmoid(a0_ref[...] + _dot(lo, a2_ref[...]))
    gate_ref[0] = _dot(g_lo, g2_ref[...])
    ones_bd = ones_ref[...]
    kk = k * kk_ref[...]
    norm = jnp.sqrt(_head_sum(kk * kk, ones_bd))
    kk = kk / jnp.maximum(norm, 1e-12)
    k = k * (1.0 + (a - 1.0) * ka_ref[...])
    r_ref[0] = r
    k_ref[0] = k
    v_ref[0] = v
    al_ref[0] = -kk
    be_ref[0] = kk * a
    bonus_ref[0] = _head_sum(r * k * rk_ref[...], ones_bd) * v


def _rwkv_pre(u, shift8, mu, w0, w2p, a0, a2p, g2, kkp, ka, rk, ones_bd, *, tt):
    bsz, t, _ = u.shape
    full = lambda a: pl.BlockSpec(a.shape, lambda i, j: (0,) * a.ndim)
    seq = pl.BlockSpec((1, tt, RWKV_C), lambda i, j: (i, j, 0))
    return pl.pallas_call(
        _rwkv_pre_kernel,
        grid=(bsz, t // tt),
        in_specs=[pl.BlockSpec((1, tt, RWKV_U), lambda i, j: (i, j, 0)),
                  pl.BlockSpec((1, SUBLANES, RWKV_U), lambda i, j: (i, 0, 0)),
                  full(mu), full(w0), full(w2p), full(a0), full(a2p), full(g2), full(kkp), full(ka),
                  full(rk), full(ones_bd)],
        out_specs=[seq] * 8,
        out_shape=[jax.ShapeDtypeStruct((bsz, t, RWKV_C), F32)] * 8,
        scratch_shapes=[pltpu.VMEM((SUBLANES + tt, RWKV_U), F32)],
        compiler_params=_cparams("parallel", "arbitrary"),
        name="rwkv_pre",
    )(u, shift8, mu, w0, w2p, a0, a2p, g2, kkp, ka, rk, ones_bd)


def _scan_kernel(r_ref, k_ref, v_ref, lw_ref, al_ref, be_ref, gate_ref, bonus_ref, s0_ref, g_ref, b_ref,
                 y_ref, s_ref, state):
    nbt, c = r_ref.shape[0], r_ref.shape[1]
    n = RWKV_N

    @pl.when(pl.program_id(1) == 0)
    def _():
        state[...] = s0_ref[...]

    row = lax.broadcasted_iota(jnp.int32, (c, c), 0)
    col = lax.broadcasted_iota(jnp.int32, (c, c), 1)
    lower = (row >= col).astype(BF16)
    eye = (row == col).astype(F32)
    row2 = lax.broadcasted_iota(jnp.int32, (2 * c, 2 * c), 0)
    col2 = lax.broadcasted_iota(jnp.int32, (2 * c, 2 * c), 1) & (c - 1)
    mask2 = col2 < jnp.where(row2 < c, row2, row2 - c + 1)
    zeros_v = jnp.zeros((c, n), F32)

    units = [(bi, h) for bi in range(nbt) for h in range(RWKV_H)]
    sl = lambda h: slice(h * n, (h + 1) * n)
    wide = []
    for bi in range(nbt):
        lw = lw_ref[bi]
        cum = _mm(lower, lw, NN, pa=1, pb=3)
        e_p = jnp.exp(cum)
        e_m = jnp.exp(-cum)
        cum_end = cum[c - 1:c, :]
        e_end = jnp.exp(cum_end - cum)
        wide.append(dict(
            rq=r_ref[bi] * e_p, aq=al_ref[bi] * jnp.exp(cum - lw), kd=k_ref[bi] * e_m, bd=be_ref[bi] * e_m,
            kend=k_ref[bi] * e_end, bend=be_ref[bi] * e_end, v=v_ref[bi], p_end=jnp.exp(cum_end)))
    xq = [jnp.concatenate([wide[bi]["aq"][:, sl(h)], wide[bi]["rq"][:, sl(h)]], axis=0) for bi, h in units]
    wd = [jnp.concatenate([wide[bi]["bd"][:, sl(h)], wide[bi]["kd"][:, sl(h)]], axis=0) for bi, h in units]
    zv = [jnp.concatenate([zeros_v, wide[bi]["v"][:, sl(h)]], axis=0) for bi, h in units]
    s_old = [state[bi, h] for bi, h in units]
    idx = range(len(units))
    gm = [jnp.where(mask2, _mm(xq[u], wd[u], NT), 0.0) for u in idx]
    xs = [_mm(xq[u], s_old[u], NT) for u in idx]
    gv = [_mm(gm[u], zv[u], NN) for u in idx]
    pw = [gm[u][:c, :c] for u in idx]
    inv = [eye + a for a in pw]
    span = 2
    while span < c:
        pw = [_mm(a, a, NN) for a in pw]
        inv = [i + _mm(i, a, NN) for i, a in zip(inv, pw)]
        span *= 2
    us = [_mm(inv[u], xs[u][:c] + gv[u][:c], NN) for u in idx]
    ys = [xs[u][c:] + gv[u][c:] + _mm(gm[u][c:, :c], us[u], NN) for u in idx]
    for u, (bi, h) in enumerate(units):
        uv = jnp.concatenate([us[u], wide[bi]["v"][:, sl(h)]], axis=0)
        ends = jnp.concatenate([wide[bi]["bend"][:, sl(h)], wide[bi]["kend"][:, sl(h)]], axis=0)
        state[bi, h] = s_old[u] * wide[bi]["p_end"][:, sl(h)] + _mm(uv, ends, TN)
    for u, (bi, h) in enumerate(units):
        y = ys[u]
        mu = jnp.mean(y, axis=-1, keepdims=True)
        yc = y - mu
        var = jnp.mean(yc * yc, axis=-1, keepdims=True)
        y = yc * lax.rsqrt(var + RWKV_LN_EPS) * g_ref[:, sl(h)] + b_ref[:, sl(h)]
        y_ref[bi, :, sl(h)] = ((y + bonus_ref[bi, :, sl(h)]) * gate_ref[bi, :, sl(h)]).astype(BF16)

    @pl.when(pl.program_id(1) == pl.num_programs(1) - 1)
    def _():
        s_ref[...] = state[...]


SCAN_ROWS = 2


def _scan(r, k, v, lw, al, be, gate, bonus, s0, g, b, *, chunk):
    bsz, t, _ = r.shape
    seq = pl.BlockSpec((SCAN_ROWS, chunk, RWKV_C), lambda i, j: (i, j, 0))
    st = pl.BlockSpec((SCAN_ROWS, RWKV_H, RWKV_N, RWKV_N), lambda i, j: (i, 0, 0, 0))
    vec = pl.BlockSpec((1, RWKV_C), lambda i, j: (0, 0))
    return pl.pallas_call(
        _scan_kernel,
        grid=(bsz // SCAN_ROWS, t // chunk),
        in_specs=[seq] * 8 + [st, vec, vec],
        out_specs=[seq, st],
        out_shape=[jax.ShapeDtypeStruct((bsz, t, RWKV_C), BF16),
                   jax.ShapeDtypeStruct((bsz, RWKV_H, RWKV_N, RWKV_N), F32)],
        scratch_shapes=[pltpu.VMEM((SCAN_ROWS, RWKV_H, RWKV_N, RWKV_N), F32)],
        compiler_params=_cparams("parallel", "arbitrary"),
        name="rwkv_scan",
    )(r, k, v, lw, al, be, gate, bonus, s0, g, b)


ATT_SPAN = 128


HEADS_PER_TILE = LANES // HEAD_DIM


ATT_UNITS = 4


def _attn_prompt_kernel(q_ref, kc_ref, kp_ref, vc_ref, vp_ref, bias_ref, o_ref, lse_ref, *, dil, nq):
    scale = HEAD_DIM ** -0.5
    first = jnp.minimum(pl.program_id(2), 1)

    def rows_of(b, res):
        start = b * ATT_SPAN * dil + res
        return pl.ds(start, ATT_SPAN, stride=dil) if dil > 1 else pl.ds(start, ATT_SPAN)

    def run(units):
        loaded = []
        for b, res in units:
            cur = rows_of(b, res)
            q, kc, vc = (ref[0, cur, :] for ref in (q_ref, kc_ref, vc_ref))
            if b == 0:
                kp, vp = (ref[0, rows_of(0, res), :] for ref in (kp_ref, vp_ref))
            else:
                kp, vp = (ref[0, rows_of(b - 1, res), :] for ref in (kc_ref, vc_ref))
            loaded.append((cur, q, kc, kp, vc, vp, first if b == 0 else 1))
        heads = [(u, h) for u in range(len(units)) for h in range(HEADS_PER_TILE)]
        sls = [slice(h * HEAD_DIM, (h + 1) * HEAD_DIM) for h in range(HEADS_PER_TILE)]
        ss = []
        for u, h in heads:
            _, q, kc, kp, _, _, variant = loaded[u]
            qh = q[:, sls[h]].astype(BF16)
            s = jnp.concatenate([_dot(qh, kp[:, sls[h]].astype(BF16), NT),
                                 _dot(qh, kc[:, sls[h]].astype(BF16), NT)], axis=1)
            ss.append(s * scale + bias_ref[variant, h])
        ms = [jnp.max(s, axis=-1, keepdims=True) for s in ss]
        es = [jnp.exp(s - m) for s, m in zip(ss, ms)]
        dens = [jnp.sum(e, axis=-1, keepdims=True) for e in es]
        outs = []
        for (u, h), e, den in zip(heads, es, dens):
            _, _, _, _, vc, vp, _ = loaded[u]
            v2 = jnp.concatenate([vp[:, sls[h]], vc[:, sls[h]]], axis=0).astype(BF16)
            outs.append(_dot((e / den).astype(BF16), v2))
        for u in range(len(units)):
            cur = loaded[u][0]
            mine = [i for i, (uu, _) in enumerate(heads) if uu == u]
            o_ref[0, cur, :] = jnp.concatenate([outs[i] for i in mine], axis=1)
            lse_ref[0, cur, :] = jnp.concatenate(
                [jnp.broadcast_to(ms[i] + jnp.log(dens[i]), outs[i].shape) for i in mine], axis=1)

    if dil > ATT_UNITS:
        def group(gi, carry):
            run([(0, gi * ATT_UNITS + r) for r in range(ATT_UNITS)])
            return carry

        lax.fori_loop(0, dil // ATT_UNITS, group, 0)
    else:
        run([(b, r) for b in range(nq) for r in range(dil)])


def _attn_prompt(ua, bias, *, g, dil):
    bsz, s, _ = ua.shape
    nq = max(ATT_UNITS // dil, 1)
    prev_rows = ATT_SPAN * dil
    rows = nq * prev_rows
    tiles = ATT_W // LANES

    def col(which):
        return lambda b, hp, i: (b, i, (which * N_DIL + g) * tiles + hp)

    def col_prev(which):
        return lambda b, hp, i: (b, jnp.maximum(i * nq - 1, 0), (which * N_DIL + g) * tiles + hp)

    blk = (1, rows, LANES)
    prev = (1, prev_rows, LANES)
    return pl.pallas_call(
        functools.partial(_attn_prompt_kernel, dil=dil, nq=nq),
        grid=(bsz, tiles, s // rows),
        in_specs=[pl.BlockSpec(blk, col(0)), pl.BlockSpec(blk, col(1)), pl.BlockSpec(prev, col_prev(1)),
                  pl.BlockSpec(blk, col(2)), pl.BlockSpec(prev, col_prev(2)),
                  pl.BlockSpec((2, HEADS_PER_TILE, ATT_SPAN, 2 * ATT_SPAN), lambda b, hp, i: (0, hp, 0, 0))],
        out_specs=[pl.BlockSpec(blk, lambda b, hp, i: (b, i, hp))] * 2,
        out_shape=[jax.ShapeDtypeStruct((bsz, s, ATT_W), F32)] * 2,
        compiler_params=_cparams("parallel", "parallel", "parallel"),
        name="attn_prompt",
    )(ua, ua, ua, ua, ua, bias)


def _attn_sample_kernel(u_ref, cache_ref, bo_ref, bn_ref, *rest, g):
    o_ref, lse_ref, new_ref = rest[-3:]
    t = u_ref.shape[1]
    rows = cache_ref.shape[-1]
    scale = HEAD_DIM ** -0.5
    base = g * ATT_W
    lane = lax.broadcasted_iota(jnp.int32, (HEAD_DIM, LANES), 1)
    place = (lax.broadcasted_iota(jnp.int32, (t, LANES), 1)
             == lax.broadcasted_iota(jnp.int32, (t, LANES), 0) + (LANES - t)).astype(BF16)
    for h in range(ATT_HG):
        sl = slice(h * HEAD_DIM, (h + 1) * HEAD_DIM)
        col = base + h * HEAD_DIM
        q = u_ref[0, :, col:col + HEAD_DIM].astype(BF16)
        k_new = u_ref[0, :, N_DIL * ATT_W + col:N_DIL * ATT_W + col + HEAD_DIM]
        v_new = u_ref[0, :, 2 * N_DIL * ATT_W + col:2 * N_DIL * ATT_W + col + HEAD_DIM]
        k_t = cache_ref[0, 0, 0, h]
        v_t = cache_ref[0, 0, 1, h]
        s_old = _dot(q, k_t.astype(BF16)) * scale + bo_ref[h]
        s_new = _dot(q, k_new.astype(BF16), NT) * scale + bn_ref[h]
        m = jnp.maximum(jnp.max(s_old, axis=-1, keepdims=True), jnp.max(s_new, axis=-1, keepdims=True))
        e_old = jnp.exp(s_old - m)
        e_new = jnp.exp(s_new - m)
        den = jnp.sum(e_old, axis=-1, keepdims=True) + jnp.sum(e_new, axis=-1, keepdims=True)
        o = _dot((e_old / den).astype(BF16), v_t.astype(BF16), NT)
        p_new = e_new / den
        for j in range(t):
            o = o + p_new[:, j:j + 1] * v_new[j:j + 1, :]
        o_ref[0, :, sl] = o
        lse_ref[0, :, sl] = jnp.broadcast_to(m + jnp.log(den), o.shape)
        for kv, (old, new) in enumerate(((k_t, k_new), (v_t, v_new))):
            moved = pltpu.roll(old, rows - t, axis=1)
            tail = jnp.where(lane >= LANES - t, _mm(new, place, TN, 3, 1), moved[:, rows - LANES:])
            if rows > LANES:
                new_ref[0, 0, kv, h, :, 0:rows - LANES] = moved[:, 0:rows - LANES]
            new_ref[0, 0, kv, h, :, rows - LANES:rows] = tail


def _attn_sample(ua, cache_t, prev, bias_old, bias_new, *, g, layer):
    bsz, t, _ = ua.shape
    out = pl.BlockSpec((1, t, ATT_W), lambda b: (b, 0, 0))
    cb = pl.BlockSpec((1, 1) + cache_t.shape[2:], lambda b: (layer, b, 0, 0, 0, 0))
    full = lambda a: pl.BlockSpec(a.shape, lambda b: (0,) * a.ndim)
    in_specs = [pl.BlockSpec((1, t, ATT_QKV), lambda b: (b, 0, 0)), cb, full(bias_old), full(bias_new)]
    args = [ua, cache_t, bias_old, bias_new]
    aliases = {}
    if prev is not None:
        in_specs.append(pl.BlockSpec(memory_space=pl.ANY))
        args.append(prev)
        aliases = {len(args) - 1: 2}
    return pl.pallas_call(
        functools.partial(_attn_sample_kernel, g=g),
        grid=(bsz,),
        in_specs=in_specs,
        out_specs=[out, out, cb],
        out_shape=[jax.ShapeDtypeStruct((bsz, t, ATT_W), F32)] * 2
                  + [jax.ShapeDtypeStruct(cache_t.shape, F32)],
        input_output_aliases=aliases,
        compiler_params=_cparams("parallel"),
        name="attn_sample",
    )(*args)


def _rel_bucket(dist):
    max_exact = N_BUCKETS // 2
    large = max_exact + (jnp.log(jnp.maximum(dist, 1).astype(F32) / max_exact)
                         / math.log(MAX_DISTANCE / max_exact) * (N_BUCKETS - max_exact)).astype(jnp.int32)
    return jnp.where(dist < max_exact, dist, jnp.minimum(large, N_BUCKETS - 1))


def _bias_lookup(tab, dist):
    onehot = (_rel_bucket(jnp.asarray(dist))[..., None] == jnp.arange(N_BUCKETS)).astype(F32)
    return jnp.einsum("...b,bh->h...", onehot, tab, precision=lax.Precision.HIGHEST)


def _prompt_bias(tab, dil):
    span = ATT_SPAN
    qi = np.arange(span)[:, None]
    ki = np.arange(2 * span)[None, :]
    j = span + qi - ki
    band = (j >= 0) & (j <= span)
    bias = _bias_lookup(tab, dil * np.clip(j, 0, span))
    rest = jnp.where(band[None], bias, NEG_INF)
    first = jnp.where((band & (ki >= span))[None], bias, NEG_INF)
    return jnp.stack([first, rest])


def _sample_bias(tab, dil, rows, t):
    rho = np.arange(rows + t)[None, :]
    d = rows + np.arange(t)[:, None] - rho
    valid = (d >= 0) & (d % dil == 0) & (d // dil <= ATT_SPAN)
    bias = jnp.where(valid[None], _bias_lookup(tab, np.clip(d, 0, None)), NEG_INF)
    return bias[:, :, :rows], bias[:, :, rows:]


def _merge_kernel(x_ref, yc_ref, yr_ref, o0_ref, o1_ref, o2_ref, l0_ref, l1_ref, l2_ref, gates_ref,
                  wc_ref, wr_ref, wa_ref, wo_ref, g_ref, b_ref, rt_ref, *rest, alpha):
    h_ref, gate_ref, eid_ref = rest[-3:]
    l0, l1, l2 = l0_ref[...], l1_ref[...], l2_ref[...]
    m = jnp.maximum(jnp.maximum(l0, l1), l2)
    e0, e1, e2 = jnp.exp(l0 - m), jnp.exp(l1 - m), jnp.exp(l2 - m)
    den = e0 + e1 + e2
    o = o0_ref[...] * (e0 / den) + o1_ref[...] * (e1 / den) + o2_ref[...] * (e2 / den)
    d = D_MODEL
    merged = (gates_ref[:, 0:d] * _dot(yc_ref[...], wc_ref[...])
              + gates_ref[:, d:2 * d] * _dot(yr_ref[...], wr_ref[...])
              + gates_ref[:, 2 * d:3 * d] * _dot(o.astype(BF16), wa_ref[...]))
    pre = alpha * x_ref[...] + _dot(merged.astype(BF16), wo_ref[...])
    h = _layer_norm(pre, g_ref[...], b_ref[...], LN_EPS)
    for c in range(ROW_TILE):
        h_ref[pl.ds(c, h.shape[0], stride=ROW_TILE), :] = h[:, c * LANES:(c + 1) * LANES]
    logits = _mm(h, rt_ref[...], NN, 3, 3)
    lane = lax.broadcasted_iota(jnp.int32, logits.shape, 1)
    big = jnp.int32(ROUTER_PAD)
    is_grp = lane < N_GROUPS
    lg = jnp.where(is_grp, logits, NEG_INF)
    m_g = jnp.max(lg, axis=-1, keepdims=True)
    grp = jnp.min(jnp.where(lg == m_g, lane, big), axis=-1, keepdims=True)
    p_grp = 1.0 / jnp.sum(jnp.where(is_grp, jnp.exp(logits - m_g), 0.0), axis=-1, keepdims=True)
    lo_lane = N_GROUPS + grp * EXPERTS_PER_GROUP
    in_grp = (lane >= lo_lane) & (lane < lo_lane + EXPERTS_PER_GROUP)
    le = jnp.where(in_grp, logits, NEG_INF)
    v1 = jnp.max(le, axis=-1, keepdims=True)
    i1 = jnp.min(jnp.where(le == v1, lane, big), axis=-1, keepdims=True)
    le = jnp.where(lane == i1, NEG_INF, le)
    v2 = jnp.max(le, axis=-1, keepdims=True)
    i2 = jnp.min(jnp.where(le == v2, lane, big), axis=-1, keepdims=True)
    e2 = jnp.exp(v2 - v1)
    den = 1.0 + e2
    gate_ref[...] = jnp.where(lane == 0, p_grp * (1.0 / den), jnp.where(lane == 1, p_grp * (e2 / den), 0.0))
    eid_ref[...] = jnp.where(lane == 0, i1 - N_GROUPS, jnp.where(lane == 1, i2 - N_GROUPS, 0))


def _merge(x, yc, yr, os_, ls_, gates, wc, wr, wa, wo, g, b, rt, *, alpha, tm, row0, n_total, prev):
    n = x.shape[0]
    row = lambda c: pl.BlockSpec((tm, c), lambda i: (i, 0))
    full = lambda a: pl.BlockSpec(a.shape, lambda i: (0,) * a.ndim)
    b0 = row0 // tm
    in_specs = ([row(D_MODEL), row(CONV_DIM), row(RWKV_C)] + [row(ATT_W)] * 6 + [row(N_BRANCH * D_MODEL)]
                + [full(a) for a in (wc, wr, wa, wo, g, b, rt)])
    args = [x, yc, yr, *os_, *ls_, gates, wc, wr, wa, wo, g, b, rt]
    aliases = {}
    if prev is not None:
        aliases = {len(args) + k: k for k in range(len(prev))}
        in_specs += [pl.BlockSpec(memory_space=pl.ANY)] * len(prev)
        args += list(prev)
    return pl.pallas_call(
        functools.partial(_merge_kernel, alpha=alpha),
        grid=(n // tm,),
        in_specs=in_specs,
        out_specs=[pl.BlockSpec((tm * ROW_TILE, LANES), lambda i: (b0 + i, 0)),
                   pl.BlockSpec((tm, ROUTER_PAD), lambda i: (b0 + i, 0)),
                   pl.BlockSpec((tm, ROUTER_PAD), lambda i: (b0 + i, 0))],
        out_shape=[jax.ShapeDtypeStruct((n_total * ROW_TILE, LANES), F32),
                   jax.ShapeDtypeStruct((n_total, ROUTER_PAD), F32),
                   jax.ShapeDtypeStruct((n_total, ROUTER_PAD), jnp.int32)],
        input_output_aliases=aliases,
        compiler_params=_cparams("parallel"),
        name="merge",
    )(*args)


assert TOP_K & (TOP_K - 1) == 0
ROW_TILE = D_MODEL // LANES


def _expert_kernel(be_ref, off_ref, cnt_ref, order_ref, h_hbm, wg_ref, wu_ref, wd_ref, out_hbm,
                   xbuf, ybuf, gsem, ssem):
    i = pl.program_id(0)
    last = pl.num_programs(0) - 1
    slot = i & 1
    other = 1 - slot
    n_out = out_hbm.shape[0] // ROW_TILE - 2 * MOE_BLOCK
    block_rows = MOE_BLOCK * ROW_TILE

    def tile(row):
        return pl.ds(pl.multiple_of(row * ROW_TILE, ROW_TILE), ROW_TILE)

    def chunk(c):
        return pl.ds(c, MOE_BLOCK, stride=ROW_TILE)

    def token_of(assignment):
        return lax.shift_right_logical(assignment, TOP_K.bit_length() - 1)

    def gather_wait(s):
        pltpu.make_async_copy(h_hbm.at[pl.ds(0, block_rows)], xbuf.at[s], gsem.at[s]).wait()

    def scatter_wait(s):
        pltpu.make_async_copy(ybuf.at[s], out_hbm.at[pl.ds(0, block_rows)], ssem.at[s]).wait()

    active = cnt_ref[i] > 0
    prev = jnp.maximum(i - 1, 0)
    prev_active = (i > 0) & (cnt_ref[prev] > 0)

    @pl.when(i == 0)
    def _():
        ybuf[...] = jnp.zeros(ybuf.shape, F32)
        pltpu.make_async_copy(ybuf.at[0], out_hbm.at[pl.ds(n_out * ROW_TILE, block_rows)], ssem.at[0]).start()
        base = off_ref[0]
        for r in range(MOE_BLOCK):
            tok = token_of(order_ref[base + r])
            pltpu.make_async_copy(h_hbm.at[tile(tok)], xbuf.at[0, tile(r)], gsem.at[0]).start()

    @pl.when(active | prev_active)
    def _():
        gather_wait(slot)
        scatter_wait(slot)
        nxt_base = off_ref[jnp.minimum(i + 1, last)]
        prev_base = off_ref[prev]
        prev_cnt = jnp.where(i > 0, cnt_ref[prev], 0)
        for r in range(MOE_BLOCK):
            tok = token_of(order_ref[nxt_base + r])
            pltpu.make_async_copy(h_hbm.at[tile(tok)], xbuf.at[other, tile(r)], gsem.at[other]).start()
            dst = jnp.where(r < prev_cnt, order_ref[prev_base + r], n_out + other * MOE_BLOCK + r)
            pltpu.make_async_copy(ybuf.at[other, tile(r)], out_hbm.at[tile(dst)], ssem.at[other]).start()
        gate = jnp.zeros((MOE_BLOCK, D_EXPERT), F32)
        up = jnp.zeros((MOE_BLOCK, D_EXPERT), F32)
        for c in range(ROW_TILE):
            xc = xbuf[slot, chunk(c), :].astype(BF16)
            gate = gate + _dot(xc, wg_ref[c * LANES:(c + 1) * LANES, :])
            up = up + _dot(xc, wu_ref[c * LANES:(c + 1) * LANES, :])
        act = (gate * _sigmoid(gate) * up).astype(BF16)
        for c in range(ROW_TILE):
            ybuf[slot, chunk(c), :] = _dot(act, wd_ref[:, c * LANES:(c + 1) * LANES])

    @pl.when(prev_active & jnp.logical_not(active))
    def _():
        gather_wait(other)
        scatter_wait(other)


def _experts(ht, block_e, block_off, block_cnt, order, wg, wu, wd, *, layer):
    n = ht.shape[0] // ROW_TILE
    n_blocks = block_e.shape[0]
    wmap = lambda i, be, off, cnt, order: (layer, be[i], 0, 0)
    grid_spec = pltpu.PrefetchScalarGridSpec(
        num_scalar_prefetch=4,
        grid=(n_blocks,),
        in_specs=[pl.BlockSpec(memory_space=pl.ANY),
                  pl.BlockSpec((None, None, D_MODEL, D_EXPERT), wmap),
                  pl.BlockSpec((None, None, D_MODEL, D_EXPERT), wmap),
                  pl.BlockSpec((None, None, D_EXPERT, D_MODEL), wmap)],
        out_specs=pl.BlockSpec(memory_space=pl.ANY),
        scratch_shapes=[pltpu.VMEM((2, MOE_BLOCK * ROW_TILE, LANES), F32),
                        pltpu.VMEM((2, MOE_BLOCK * ROW_TILE, LANES), F32),
                        pltpu.SemaphoreType.DMA((2,)), pltpu.SemaphoreType.DMA((2,))],
    )
    return pl.pallas_call(
        _expert_kernel,
        grid_spec=grid_spec,
        out_shape=jax.ShapeDtypeStruct(((TOP_K * n + 2 * MOE_BLOCK) * ROW_TILE, LANES), F32),
        compiler_params=_cparams("arbitrary"),
        name="experts",
    )(block_e, block_off, block_cnt, order, ht, wg, wu, wd)


def _route(eid):
    n = eid.shape[0]
    a = n * TOP_K
    n_blocks = -(-a // MOE_BLOCK) + N_EXPERTS
    skey = jnp.sort(eid.reshape(-1) * a + jnp.arange(a, dtype=jnp.int32))
    order = skey % a
    experts = jnp.arange(N_EXPERTS, dtype=jnp.int32)
    start = jnp.sum(skey[None, :] < (experts * a)[:, None], axis=1, dtype=jnp.int32)
    counts = jnp.sum(skey[None, :] < ((experts + 1) * a)[:, None], axis=1, dtype=jnp.int32) - start
    padded = (counts + MOE_BLOCK - 1) // MOE_BLOCK * MOE_BLOCK
    pend = jnp.cumsum(padded)
    pstart = pend - padded
    blk_start = jnp.arange(n_blocks, dtype=jnp.int32) * MOE_BLOCK
    block_e = jnp.minimum(jnp.sum(pend[None, :] <= blk_start[:, None], axis=1, dtype=jnp.int32), N_EXPERTS - 1)
    pick = (block_e[:, None] == experts[None, :]).astype(jnp.int32)
    within = blk_start - jnp.sum(pick * pstart[None, :], axis=1)
    block_cnt = jnp.where(blk_start < pend[-1],
                          jnp.clip(jnp.sum(pick * counts[None, :], axis=1) - within, 0, MOE_BLOCK), 0)
    block_off = jnp.where(block_cnt > 0, jnp.sum(pick * start[None, :], axis=1) + within, 0)
    order = jnp.concatenate([order, jnp.zeros((MOE_BLOCK,), jnp.int32)])
    return block_e, block_off.astype(jnp.int32), block_cnt.astype(jnp.int32), order


def _ln2_kernel(h_ref, f_ref, gate_ref, g_ref, b_ref, y_ref, yb_ref, *, alpha):
    tm = y_ref.shape[0]
    g0, g1 = gate_ref[:, 0:1], gate_ref[:, 1:2]
    pre = []
    for c in range(ROW_TILE):
        f0 = f_ref[pl.ds(c, tm, stride=TOP_K * ROW_TILE), :]
        f1 = f_ref[pl.ds(ROW_TILE + c, tm, stride=TOP_K * ROW_TILE), :]
        pre.append(alpha * h_ref[pl.ds(c, tm, stride=ROW_TILE), :] + (f0 * g0 + f1 * g1))
    mu = sum(jnp.sum(p, axis=-1, keepdims=True) for p in pre) * (1.0 / D_MODEL)
    cen = [p - mu for p in pre]
    var = sum(jnp.sum(p * p, axis=-1, keepdims=True) for p in cen) * (1.0 / D_MODEL)
    rstd = lax.rsqrt(var + LN_EPS)
    for c in range(ROW_TILE):
        sl = slice(c * LANES, (c + 1) * LANES)
        y = cen[c] * rstd * g_ref[:, sl] + b_ref[:, sl]
        y_ref[:, sl] = y
        yb_ref[:, sl] = y.astype(BF16)


def _ln2(ht, ft, gate, g, b, *, alpha, tm, row0, n):
    vec = pl.BlockSpec((1, D_MODEL), lambda i: (0, 0))
    b0 = row0 // tm
    return pl.pallas_call(
        functools.partial(_ln2_kernel, alpha=alpha),
        grid=(n // tm,),
        in_specs=[pl.BlockSpec((tm * ROW_TILE, LANES), lambda i: (b0 + i, 0)),
                  pl.BlockSpec((tm * TOP_K * ROW_TILE, LANES), lambda i: (b0 + i, 0)),
                  pl.BlockSpec((tm, ROUTER_PAD), lambda i: (b0 + i, 0)), vec, vec],
        out_specs=[pl.BlockSpec((tm, D_MODEL), lambda i: (i, 0))] * 2,
        out_shape=[jax.ShapeDtypeStruct((n, D_MODEL), F32), jax.ShapeDtypeStruct((n, D_MODEL), BF16)],
        compiler_params=_cparams("parallel"),
        name="ln2",
    )(ht, ft, gate, g, b)


def _row(v):
    return v.reshape(1, -1)


def _mixers(x, xb, p, conv_state, shift, wkv, caches, *, alpha, prompt, layer, row0, n_total, moe_prev):
    bsz, t, d = x.shape
    n = bsz * t
    xb2 = xb.reshape(n, d)
    o1 = 2 * CONV_DIM
    o2 = o1 + RWKV_U
    o3 = o2 + ATT_QKV
    tm_in = min(n, 1024)
    z = _proj(xb2, p["w_in"], p["b_in"], layer=layer, col0=0, n_cols=o1, epilogue="glu", tn=D_MODEL, tm=tm_in)
    gates = _proj(xb2, p["w_in"], p["b_in"], layer=layer, col0=o3, n_cols=N_BRANCH * D_MODEL,
                  epilogue="sigmoid_bias", tn=D_MODEL, tm=tm_in)
    ur = _proj(xb2, p["w_rwkv"], p["b_in"], layer=layer, col0=0, n_cols=RWKV_U, epilogue="none",
               tn=RWKV_U, tm=min(n, 512))
    ua = _proj(xb2, p["w_att"], p["b_in"], layer=layer, col0=0, n_cols=ATT_QKV, epilogue="none",
               tn=ATT_QKV // 2, tm=min(n, 512))
    z = z.reshape(bsz, t, CONV_DIM)
    ur = ur.reshape(bsz, t, RWKV_U)
    ua = ua.reshape(bsz, t, ATT_QKV)

    hist = jnp.pad(conv_state, ((0, 0), (CONV_HIST - (CONV_K - 1), 0), (0, 0)))
    yc = _conv(z, hist, p["conv_dw"], p["conv_dw_b"], p["conv_ln_g"], p["conv_ln_b"], tt=256 if prompt else t)
    conv_new = jnp.concatenate([conv_state, z], axis=1)[:, -(CONV_K - 1):]

    shift8 = jnp.pad(shift[:, None, :], ((0, 0), (SUBLANES - 1, 0), (0, 0)))
    pre = _rwkv_pre(ur, shift8, p["rwkv_mu"], p["rwkv_w0"], p["rwkv_w2p"], p["rwkv_a0"], p["rwkv_a2p"],
                    p["rwkv_g2"], p["rwkv_kk"], p["rwkv_ka"], p["rwkv_rk"], p["ones_bd"],
                    tt=256 if prompt else t)
    chunk = 64 if prompt else SUBLANES
    if t % chunk:
        pre = [jnp.pad(a, ((0, 0), (0, chunk - t % chunk), (0, 0))) for a in pre]
    yr, wkv_new = _scan(*pre, wkv, p["rwkv_ln_g"], p["rwkv_ln_b"], chunk=chunk)
    yr = yr[:, :t]
    shift_new = ur[:, -1]

    os_, ls_, att_new = [], [], []
    for g, (window, dil) in enumerate(DILATIONS):
        if prompt:
            o, lse = _attn_prompt(ua, p["att_bias"][g], g=g, dil=dil)
            rows = min(window, t)
            kv = [ua[:, t - rows:, (w * N_DIL + g) * ATT_W:(w * N_DIL + g + 1) * ATT_W] for w in (1, 2)]
            att_new.append(jnp.stack([a.reshape(bsz, rows, ATT_HG, HEAD_DIM) for a in kv], axis=1))
        else:
            cache_t, prev = caches[g]
            o, lse, new = _attn_sample(ua, cache_t, prev, *p["att_bias"][g], g=g, layer=layer)
            att_new.append(new)
        os_.append(o.reshape(n, ATT_W))
        ls_.append(lse.reshape(n, ATT_W))

    moe_in = _merge(x.reshape(n, d), yc.reshape(n, CONV_DIM), yr.reshape(n, RWKV_C), os_, ls_, gates,
                    p["conv_proj"], p["rwkv_proj"], p["attn_proj"], p["w_out"], p["ln1_g"], p["ln1_b"],
                    p["router"], alpha=alpha, tm=min(n, 256), row0=row0, n_total=n_total, prev=moe_prev)
    return moe_in, conv_new, shift_new, wkv_new, att_new


def kernel(x_prompt, x_sample, state_conv, state_shift, state_wkv, cache_attn_w128, cache_attn_w512, cache_attn_w2048, rel_bias, w_in, b_gate, conv_dw, conv_dw_b, conv_ln_g, conv_ln_b, conv_proj, rwkv_mu, rwkv_w0, rwkv_w2, rwkv_a0, rwkv_a2, rwkv_g2, rwkv_kk, rwkv_ka, rwkv_rk, rwkv_ln_g, rwkv_ln_b, rwkv_proj, attn_proj, w_out, ln1_g, ln1_b, router_group, router_expert, moe_w_gate, moe_w_up, moe_w_down, ln2_g, ln2_b):
    depth = w_in.shape[0]
    alpha = (2 * depth) ** 0.25
    bp, tp, _ = x_prompt.shape
    ts = x_sample.shape[1]
    n_p, n_s = bp * tp, x_sample.shape[0] * ts
    caches_t = [jnp.transpose(c, (0, 1, 2, 4, 5, 3)) for c in (cache_attn_w128, cache_attn_w512, cache_attn_w2048)]
    new_caches = [None] * N_DIL
    head = np.arange(RWKV_C) // RWKV_N
    ones_bd = jnp.asarray(head[:, None] == head[None, :], BF16)
    tabs = [rel_bias[:, g * ATT_HG:(g + 1) * ATT_HG] for g in range(N_DIL)]
    bias_p = [_prompt_bias(tabs[g], dil) for g, (_, dil) in enumerate(DILATIONS)]
    bias_s = [_sample_bias(tabs[g], dil, caches_t[g].shape[-1], ts) for g, (_, dil) in enumerate(DILATIONS)]

    w_in_b = w_in.astype(BF16)
    o_rwkv = 2 * CONV_DIM
    o_att = o_rwkv + RWKV_U
    w_rwkv_b = w_in_b[:, :, o_rwkv:o_att]
    w_att_b = w_in_b[:, :, o_att:o_att + ATT_QKV]
    b_in =jnp.pad(b_gate, ((0, 0), (w_in.shape[2] - b_gate.shape[1], 0)))[:, None, :]
    moe_b = [w.astype(BF16) for w in (moe_w_gate, moe_w_up, moe_w_down)]

    xp, xs = x_prompt, x_sample
    xpb, xsb = xp.astype(BF16), xs.astype(BF16)
    outs_p, outs_s = [], []
    for l in range(depth):
        zeros_lora = jnp.zeros((LORA_W, RWKV_C), F32)
        router = jnp.concatenate(
            [router_group[l], router_expert[l],
             jnp.zeros((D_MODEL, ROUTER_PAD - N_GROUPS - N_EXPERTS), F32)], axis=1)
        p = dict(
            w_in=w_in_b, b_in=b_in, w_rwkv=w_rwkv_b, w_att=w_att_b,
            conv_dw=conv_dw[l], conv_dw_b=_row(conv_dw_b[l]), conv_ln_g=_row(conv_ln_g[l]),
            conv_ln_b=_row(conv_ln_b[l]), conv_proj=conv_proj[l].astype(BF16),
            rwkv_mu=_row(rwkv_mu[l]), rwkv_w0=_row(rwkv_w0[l]),
            rwkv_w2p=jnp.concatenate([rwkv_w2[l], zeros_lora], axis=0).astype(BF16),
            rwkv_a0=_row(rwkv_a0[l]),
            rwkv_a2p=jnp.concatenate([zeros_lora, rwkv_a2[l]], axis=0).astype(BF16),
            rwkv_g2=rwkv_g2[l].astype(BF16), rwkv_kk=_row(rwkv_kk[l]), rwkv_ka=_row(rwkv_ka[l]),
            rwkv_rk=_row(rwkv_rk[l]), rwkv_ln_g=_row(rwkv_ln_g[l]), rwkv_ln_b=_row(rwkv_ln_b[l]),
            rwkv_proj=rwkv_proj[l].astype(BF16), attn_proj=attn_proj[l].astype(BF16),
            w_out=w_out[l].astype(BF16), ln1_g=_row(ln1_g[l]), ln1_b=_row(ln1_b[l]), router=router,
            ln2_g=_row(ln2_g[l]), ln2_b=_row(ln2_b[l]),
            ones_bd=ones_bd)
        moe_in, *new_p = _mixers(
            xp, xpb, dict(p, att_bias=bias_p),
            jnp.zeros((bp, CONV_K - 1, CONV_DIM), F32), jnp.zeros((bp, RWKV_U), F32),
            jnp.zeros((bp, RWKV_H, RWKV_N, RWKV_N), F32), None, alpha=alpha, prompt=True, layer=l,
            row0=0, n_total=n_p + n_s, moe_prev=None)
        moe_in, *new_s = _mixers(
            xs, xsb, dict(p, att_bias=bias_s), state_conv[l], state_shift[l], state_wkv[l],
            list(zip(caches_t, new_caches)), alpha=alpha, prompt=False, layer=l,
            row0=n_p, n_total=n_p + n_s, moe_prev=moe_in)
        ht, gate, eid = moe_in
        ft = _experts(ht, *_route(eid[:, :TOP_K]), *moe_b, layer=l)
        xp, xpb = _ln2(ht, ft, gate, p["ln2_g"], p["ln2_b"], alpha=alpha, tm=512, row0=0, n=n_p)
        xs, xsb = _ln2(ht, ft, gate, p["ln2_g"], p["ln2_b"], alpha=alpha, tm=n_s, row0=n_p, n=n_s)
        xp, xpb = xp.reshape(x_prompt.shape), xpb.reshape(x_prompt.shape)
        xs, xsb = xs.reshape(x_sample.shape), xsb.reshape(x_sample.shape)
        new_caches = new_s[3]
        outs_p.append(new_p)
        outs_s.append(new_s)

    def stack(outs, i):
        return jnp.stack([o[i] for o in outs])

    att_p = [jnp.stack([o[3][g] for o in outs_p]) for g in range(N_DIL)]
    att_s = [jnp.transpose(c, (0, 1, 2, 5, 3, 4)) for c in new_caches]
    return (xp, xs, stack(outs_p, 0), stack(outs_s, 0), stack(outs_p, 1), stack(outs_s, 1),
            stack(outs_p, 2), stack(outs_s, 2),
            att_p[0], att_s[0], att_p[1], att_s[1], att_p[2], att_s[2])
```

```python
import functools
import math

import jax
import jax.numpy as jnp
import numpy as np
from jax import lax
from jax.experimental import pallas as pl
from jax.experimental.pallas import tpu as pltpu

F32 = jnp.float32
BF16 = jnp.bfloat16

D_MODEL = 1024
CONV_DIM = D_MODEL // 2
CONV_K = 31
RWKV_N = 64
RWKV_H = D_MODEL // 128
RWKV_C = RWKV_H * RWKV_N
LORA_W, LORA_A, LORA_G = 64, 64, 128
RWKV_U = 3 * RWKV_C + LORA_W + LORA_A + LORA_G
RWKV_LN_EPS = 64e-5
HEAD_DIM = 64
ATT_HG = 4
ATT_W = ATT_HG * HEAD_DIM
DILATIONS = ((128, 1), (512, 4), (2048, 16))
N_DIL = len(DILATIONS)
ATT_QKV = 3 * N_DIL * ATT_W
N_BUCKETS = 32
MAX_DISTANCE = 2048
N_BRANCH = 3
N_GROUPS = 4
EXPERTS_PER_GROUP = 8
N_EXPERTS = N_GROUPS * EXPERTS_PER_GROUP
TOP_K = 2
D_EXPERT = D_MODEL // 2
MOE_BLOCK = 128
LN_EPS = 1e-5
NEG_INF = -1e30
ROUTER_PAD = 128

LANES = 128
SUBLANES = 8
VMEM_LIMIT = 48 * 1024 * 1024

NN = (((1,), (0,)), ((), ()))
NT = (((1,), (1,)), ((), ()))
TN = (((0,), (0,)), ((), ()))


def _cparams(*sem):
    return pltpu.CompilerParams(dimension_semantics=sem, vmem_limit_bytes=VMEM_LIMIT)


def _dot(a, b, dims=NN):
    return lax.dot_general(a, b, dims, preferred_element_type=F32)


def _pieces(x, n):
    if x.dtype == BF16:
        return [x]
    out, r = [], x
    for i in range(n):
        p = r.astype(BF16)
        out.append(p)
        if i + 1 < n:
            r = r - p.astype(F32)
    return out


def _mm(a, b, dims=NN, pa=1, pb=1):
    ap, bp = _pieces(a, pa), _pieces(b, pb)
    order = max(len(ap), len(bp))
    acc = None
    for i in reversed(range(len(ap))):
        for j in reversed(range(len(bp))):
            if i + j < order:
                t = _dot(ap[i], bp[j], dims)
                acc = t if acc is None else acc + t
    return acc


def _sigmoid(x):
    return 1.0 / (1.0 + jnp.exp(-x))


def _layer_norm(x, g, b, eps):
    mu = jnp.mean(x, axis=-1, keepdims=True)
    xc = x - mu
    var = jnp.mean(xc * xc, axis=-1, keepdims=True)
    return xc * lax.rsqrt(var + eps) * g + b


def _proj_kernel(x_ref, w_ref, b_ref, o_ref, *, epilogue):
    acc = _dot(x_ref[...], w_ref[...])
    if epilogue == "glu":
        half = acc.shape[1] // 2
        o_ref[...] = acc[:, :half] * _sigmoid(acc[:, half:])
    elif epilogue == "sigmoid_bias":
        o_ref[...] = _sigmoid(acc + b_ref[...])
    else:
        o_ref[...] = acc


def _proj(x, w_all, b_all, *, layer, col0, n_cols, epilogue, tn, tm):
    n, k = x.shape
    out_cols = n_cols // 2 if epilogue == "glu" else n_cols
    out_tn = tn // 2 if epilogue == "glu" else tn
    c0 = col0 // tn
    mt, nt = n // tm, n_cols // tn
    rows_outer = mt * n_cols + n <= nt * n + n_cols
    grid = (mt, nt) if rows_outer else (nt, mt)
    ij = (lambda a, b: (a, b)) if rows_outer else (lambda a, b: (b, a))
    return pl.pallas_call(
        functools.partial(_proj_kernel, epilogue=epilogue),
        grid=grid,
        in_specs=[pl.BlockSpec((tm, k), lambda a, b: (ij(a, b)[0], 0)),
                  pl.BlockSpec((None, k, tn), lambda a, b: (layer, 0, c0 + ij(a, b)[1])),
                  pl.BlockSpec((None, 1, tn), lambda a, b: (layer, 0, c0 + ij(a, b)[1]))],
        out_specs=pl.BlockSpec((tm, out_tn), lambda a, b: ij(a, b)),
        out_shape=jax.ShapeDtypeStruct((n, out_cols), F32),
        compiler_params=_cparams("parallel", "parallel"),
        name="in_proj_" + epilogue,
    )(x, w_all, b_all)


CONV_HIST = 32
CONV_ROWS = 64


def _conv_kernel(z_ref, hist_ref, dw_ref, dwb_ref, g_ref, b_ref, o_ref, zbuf):
    tt = z_ref.shape[1]
    win = CONV_HIST + tt

    @pl.when(pl.program_id(1) == 0)
    def _():
        zbuf[0, 0:CONV_HIST, :] = hist_ref[0]

    @pl.when(pl.program_id(1) != 0)
    def _():
        zbuf[0, 0:CONV_HIST, :] = zbuf[0, tt:tt + CONV_HIST, :]

    zbuf[0, CONV_HIST:win, :] = z_ref[0]
    if tt % SUBLANES == 0:
        for s in range(1, SUBLANES):
            zbuf[s, 0:win - SUBLANES, :] = zbuf[0, s:s + win - SUBLANES, :]
    pad = CONV_HIST - (CONV_K - 1)
    rb = min(tt, CONV_ROWS)
    for r0 in range(0, tt, rb):
        acc = jnp.zeros((rb, CONV_DIM), F32)
        for k in range(CONV_K):
            off = r0 + pad + k
            if tt % SUBLANES == 0:
                tap = zbuf[off % SUBLANES, off - off % SUBLANES:off - off % SUBLANES + rb, :]
            else:
                tap = zbuf[0, off:off + rb, :]
            acc = acc + tap * dw_ref[k:k + 1, :]
        y = _layer_norm(acc + dwb_ref[...], g_ref[...], b_ref[...], LN_EPS)
        o_ref[0, r0:r0 + rb, :] = (y * _sigmoid(y)).astype(BF16)


def _conv(z, hist, dw, dwb, g, b, *, tt):
    bsz, t, _ = z.shape
    vec = pl.BlockSpec((1, CONV_DIM), lambda i, j: (0, 0))
    return pl.pallas_call(
        _conv_kernel,
        grid=(bsz, t // tt),
        in_specs=[pl.BlockSpec((1, tt, CONV_DIM), lambda i, j: (i, j, 0)),
                  pl.BlockSpec((1, CONV_HIST, CONV_DIM), lambda i, j: (i, 0, 0)),
                  pl.BlockSpec((CONV_K, CONV_DIM), lambda i, j: (0, 0)),
                  vec, vec, vec],
        out_specs=pl.BlockSpec((1, tt, CONV_DIM), lambda i, j: (i, j, 0)),
        out_shape=jax.ShapeDtypeStruct((bsz, t, CONV_DIM), BF16),
        scratch_shapes=[pltpu.VMEM((SUBLANES, CONV_HIST + tt, CONV_DIM), F32)],
        compiler_params=_cparams("parallel", "arbitrary"),
        name="conv",
    )(z, hist, dw, dwb, g, b)


def _head_sum(x, ones_bd):
    return _mm(x, ones_bd, NN, pa=3, pb=1)


def _rwkv_pre_kernel(u_ref, shift_ref, mu_ref, w0_ref, w2_ref, a0_ref, a2_ref, g2_ref, kk_ref, ka_ref,
                     rk_ref, ones_ref,
                     r_ref, k_ref, v_ref, lw_ref, al_ref, be_ref, gate_ref, bonus_ref, ubuf):
    tt = u_ref.shape[1]
    c = RWKV_C

    @pl.when(pl.program_id(1) == 0)
    def _():
        ubuf[0:SUBLANES, :] = shift_ref[0]

    @pl.when(pl.program_id(1) != 0)
    def _():
        ubuf[0:SUBLANES, :] = ubuf[tt:tt + SUBLANES, :]

    u = u_ref[0]
    ubuf[SUBLANES:SUBLANES + tt, :] = u
    u_prev = ubuf[SUBLANES - 1:SUBLANES - 1 + tt, :]
    um = u + (u_prev - u) * mu_ref[...]
    r, k, v = um[:, 0:c], um[:, c:2 * c], um[:, 2 * c:3 * c]
    lo = um[:, 3 * c:3 * c + LORA_W + LORA_A]
    lane = lax.broadcasted_iota(jnp.int32, lo.shape, 1)
    lo = jnp.where(lane < LORA_W, jnp.tanh(lo), lo).astype(BF16)
    g_lo = _sigmoid(um[:, 3 * c + LORA_W + LORA_A:]).astype(BF16)
    xw = -(w0_ref[...] + _dot(lo, w2_ref[...]))
    softplus = jnp.maximum(xw, 0.0) + jnp.log(1.0 + jnp.exp(-jnp.abs(xw)))
    lw_ref[0] = -jnp.exp(-softplus - 0.5)
    a = _sigmoid(a0_ref[...] + _dot(lo, a2_ref[...]))
    gate_ref[0] = _dot(g_lo, g2_ref[...])
    ones_bd = ones_ref[...]
    kk = k * kk_ref[...]
    norm = jnp.sqrt(_head_sum(kk * kk, ones_bd))
    kk = kk / jnp.maximum(norm, 1e-12)
    k = k * (1.0 + (a - 1.0) * ka_ref[...])
    r_ref[0] = r
    k_ref[0] = k
    v_ref[0] = v
    al_ref[0] = -kk
    be_ref[0] = kk * a
    bonus_ref[0] = _head_sum(r * k * rk_ref[...], ones_bd) * v


def _rwkv_pre(u, shift8, mu, w0, w2p, a0, a2p, g2, kkp, ka, rk, ones_bd, *, tt):
    bsz, t, _ = u.shape
    full = lambda a: pl.BlockSpec(a.shape, lambda i, j: (0,) * a.ndim)
    seq = pl.BlockSpec((1, tt, RWKV_C), lambda i, j: (i, j, 0))
    return pl.pallas_call(
        _rwkv_pre_kernel,
        grid=(bsz, t // tt),
        in_specs=[pl.BlockSpec((1, tt, RWKV_U), lambda i, j: (i, j, 0)),
                  pl.BlockSpec((1, SUBLANES, RWKV_U), lambda i, j: (i, 0, 0)),
                  full(mu), full(w0), full(w2p), full(a0), full(a2p), full(g2), full(kkp), full(ka),
                  full(rk), full(ones_bd)],
        out_specs=[seq] * 8,
        out_shape=[jax.ShapeDtypeStruct((bsz, t, RWKV_C), F32)] * 8,
        scratch_shapes=[pltpu.VMEM((SUBLANES + tt, RWKV_U), F32)],
        compiler_params=_cparams("parallel", "arbitrary"),
        name="rwkv_pre",
    )(u, shift8, mu, w0, w2p, a0, a2p, g2, kkp, ka, rk, ones_bd)


def _scan_kernel(r_ref, k_ref, v_ref, lw_ref, al_ref, be_ref, gate_ref, bonus_ref, s0_ref, g_ref, b_ref,
                 y_ref, s_ref, state):
    nbt, c = r_ref.shape[0], r_ref.shape[1]
    n = RWKV_N

    @pl.when(pl.program_id(1) == 0)
    def _():
        state[...] = s0_ref[...]

    row = lax.broadcasted_iota(jnp.int32, (c, c), 0)
    col = lax.broadcasted_iota(jnp.int32, (c, c), 1)
    lower = (row >= col).astype(BF16)
    eye = (row == col).astype(F32)
    row2 = lax.broadcasted_iota(jnp.int32, (2 * c, 2 * c), 0)
    col2 = lax.broadcasted_iota(jnp.int32, (2 * c, 2 * c), 1) & (c - 1)
    mask2 = col2 < jnp.where(row2 < c, row2, row2 - c + 1)
    zeros_v = jnp.zeros((c, n), F32)

    units = [(bi, h) for bi in range(nbt) for h in range(RWKV_H)]
    sl = lambda h: slice(h * n, (h + 1) * n)
    wide = []
    for bi in range(nbt):
        lw = lw_ref[bi]
        cum = _mm(lower, lw, NN, pa=1, pb=3)
        e_p = jnp.exp(cum)
        e_m = jnp.exp(-cum)
        cum_end = cum[c - 1:c, :]
        e_end = jnp.exp(cum_end - cum)
        wide.append(dict(
            rq=r_ref[bi] * e_p, aq=al_ref[bi] * jnp.exp(cum - lw), kd=k_ref[bi] * e_m, bd=be_ref[bi] * e_m,
            kend=k_ref[bi] * e_end, bend=be_ref[bi] * e_end, v=v_ref[bi], p_end=jnp.exp(cum_end)))
    xq = [jnp.concatenate([wide[bi]["aq"][:, sl(h)], wide[bi]["rq"][:, sl(h)]], axis=0) for bi, h in units]
    wd = [jnp.concatenate([wide[bi]["bd"][:, sl(h)], wide[bi]["kd"][:, sl(h)]], axis=0) for bi, h in units]
    zv = [jnp.concatenate([zeros_v, wide[bi]["v"][:, sl(h)]], axis=0) for bi, h in units]
    s_old = [state[bi, h] for bi, h in units]
    idx = range(len(units))
    gm = [jnp.where(mask2, _mm(xq[u], wd[u], NT), 0.0) for u in idx]
    xs = [_mm(xq[u], s_old[u], NT) for u in idx]
    gv = [_mm(gm[u], zv[u], NN) for u in idx]
    pw = [gm[u][:c, :c] for u in idx]
    inv = [eye + a for a in pw]
    span = 2
    while span < c:
        pw = [_mm(a, a, NN) for a in pw]
        inv = [i + _mm(i, a, NN) for i, a in zip(inv, pw)]
        span *= 2
    us = [_mm(inv[u], xs[u][:c] + gv[u][:c], NN) for u in idx]
    ys = [xs[u][c:] + gv[u][c:] + _mm(gm[u][c:, :c], us[u], NN) for u in idx]
    for u, (bi, h) in enumerate(units):
        uv = jnp.concatenate([us[u], wide[bi]["v"][:, sl(h)]], axis=0)
        ends = jnp.concatenate([wide[bi]["bend"][:, sl(h)], wide[bi]["kend"][:, sl(h)]], axis=0)
        state[bi, h] = s_old[u] * wide[bi]["p_end"][:, sl(h)] + _mm(uv, ends, TN)
    for u, (bi, h) in enumerate(units):
        y = ys[u]
        mu = jnp.mean(y, axis=-1, keepdims=True)
        yc = y - mu
        var = jnp.mean(yc * yc, axis=-1, keepdims=True)
        y = yc * lax.rsqrt(var + RWKV_LN_EPS) * g_ref[:, sl(h)] + b_ref[:, sl(h)]
        y_ref[bi, :, sl(h)] = ((y + bonus_ref[bi, :, sl(h)]) * gate_ref[bi, :, sl(h)]).astype(BF16)

    @pl.when(pl.program_id(1) == pl.num_programs(1) - 1)
    def _():
        s_ref[...] = state[...]


SCAN_ROWS = 4


def _scan(r, k, v, lw, al, be, gate, bonus, s0, g, b, *, chunk):
    bsz, t, _ = r.shape
    seq = pl.BlockSpec((SCAN_ROWS, chunk, RWKV_C), lambda i, j: (i, j, 0))
    st = pl.BlockSpec((SCAN_ROWS, RWKV_H, RWKV_N, RWKV_N), lambda i, j: (i, 0, 0, 0))
    vec = pl.BlockSpec((1, RWKV_C), lambda i, j: (0, 0))
    return pl.pallas_call(
        _scan_kernel,
        grid=(bsz // SCAN_ROWS, t // chunk),
        in_specs=[seq] * 8 + [st, vec, vec],
        out_specs=[seq, st],
        out_shape=[jax.ShapeDtypeStruct((bsz, t, RWKV_C), BF16),
                   jax.ShapeDtypeStruct((bsz, RWKV_H, RWKV_N, RWKV_N), F32)],
        scratch_shapes=[pltpu.VMEM((SCAN_ROWS, RWKV_H, RWKV_N, RWKV_N), F32)],
        compiler_params=_cparams("parallel", "arbitrary"),
        name="rwkv_scan",
    )(r, k, v, lw, al, be, gate, bonus, s0, g, b)


ATT_SPAN = 128


HEADS_PER_TILE = LANES // HEAD_DIM


ATT_UNITS = 4


def _attn_prompt_kernel(q_ref, kc_ref, kp_ref, vc_ref, vp_ref, bias_ref, o_ref, lse_ref, *, dil, nq):
    scale = HEAD_DIM ** -0.5
    first = jnp.minimum(pl.program_id(2), 1)

    def rows_of(b, res):
        start = b * ATT_SPAN * dil + res
        return pl.ds(start, ATT_SPAN, stride=dil) if dil > 1 else pl.ds(start, ATT_SPAN)

    def run(units):
        loaded = []
        for b, res in units:
            cur = rows_of(b, res)
            q, kc, vc = (ref[0, cur, :] for ref in (q_ref, kc_ref, vc_ref))
            if b == 0:
                kp, vp = (ref[0, rows_of(0, res), :] for ref in (kp_ref, vp_ref))
            else:
                kp, vp = (ref[0, rows_of(b - 1, res), :] for ref in (kc_ref, vc_ref))
            loaded.append((cur, q, kc, kp, vc, vp, first if b == 0 else 1))
        heads = [(u, h) for u in range(len(units)) for h in range(HEADS_PER_TILE)]
        sls = [slice(h * HEAD_DIM, (h + 1) * HEAD_DIM) for h in range(HEADS_PER_TILE)]
        ss = []
        for u, h in heads:
            _, q, kc, kp, _, _, variant = loaded[u]
            qh = q[:, sls[h]].astype(BF16)
            s = jnp.concatenate([_dot(qh, kp[:, sls[h]].astype(BF16), NT),
                                 _dot(qh, kc[:, sls[h]].astype(BF16), NT)], axis=1)
            ss.append(s * scale + bias_ref[variant, h])
        ms = [jnp.max(s, axis=-1, keepdims=True) for s in ss]
        es = [jnp.exp(s - m) for s, m in zip(ss, ms)]
        dens = [jnp.sum(e, axis=-1, keepdims=True) for e in es]
        outs = []
        for (u, h), e, den in zip(heads, es, dens):
            _, _, _, _, vc, vp, _ = loaded[u]
            v2 = jnp.concatenate([vp[:, sls[h]], vc[:, sls[h]]], axis=0).astype(BF16)
            outs.append(_dot((e / den).astype(BF16), v2))
        for u in range(len(units)):
            cur = loaded[u][0]
            mine = [i for i, (uu, _) in enumerate(heads) if uu == u]
            o_ref[0, cur, :] = jnp.concatenate([outs[i] for i in mine], axis=1)
            lse_ref[0, cur, :] = jnp.concatenate(
                [jnp.broadcast_to(ms[i] + jnp.log(dens[i]), outs[i].shape) for i in mine], axis=1)

    if dil > ATT_UNITS:
        def group(gi, carry):
            run([(0, gi * ATT_UNITS + r) for r in range(ATT_UNITS)])
            return carry

        lax.fori_loop(0, dil // ATT_UNITS, group, 0)
    else:
        run([(b, r) for b in range(nq) for r in range(dil)])


def _attn_prompt(ua, bias, *, g, dil):
    bsz, s, _ = ua.shape
    nq = max(ATT_UNITS // dil, 1)
    prev_rows = ATT_SPAN * dil
    rows = nq * prev_rows
    tiles = ATT_W // LANES

    def col(which):
        return lambda b, hp, i: (b, i, (which * N_DIL + g) * tiles + hp)

    def col_prev(which):
        return lambda b, hp, i: (b, jnp.maximum(i * nq - 1, 0), (which * N_DIL + g) * tiles + hp)

    blk = (1, rows, LANES)
    prev = (1, prev_rows, LANES)
    return pl.pallas_call(
        functools.partial(_attn_prompt_kernel, dil=dil, nq=nq),
        grid=(bsz, tiles, s // rows),
        in_specs=[pl.BlockSpec(blk, col(0)), pl.BlockSpec(blk, col(1)), pl.BlockSpec(prev, col_prev(1)),
                  pl.BlockSpec(blk, col(2)), pl.BlockSpec(prev, col_prev(2)),
                  pl.BlockSpec((2, HEADS_PER_TILE, ATT_SPAN, 2 * ATT_SPAN), lambda b, hp, i: (0, hp, 0, 0))],
        out_specs=[pl.BlockSpec(blk, lambda b, hp, i: (b, i, hp))] * 2,
        out_shape=[jax.ShapeDtypeStruct((bsz, s, ATT_W), F32)] * 2,
        compiler_params=_cparams("parallel", "parallel", "parallel"),
        name="attn_prompt",
    )(ua, ua, ua, ua, ua, bias)


def _attn_sample_kernel(u_ref, cache_ref, bo_ref, bn_ref, *rest, g):
    o_ref, lse_ref, new_ref = rest[-3:]
    t = u_ref.shape[1]
    rows = cache_ref.shape[-1]
    scale = HEAD_DIM ** -0.5
    base = g * ATT_W
    lane = lax.broadcasted_iota(jnp.int32, (HEAD_DIM, LANES), 1)
    place = (lax.broadcasted_iota(jnp.int32, (t, LANES), 1)
             == lax.broadcasted_iota(jnp.int32, (t, LANES), 0) + (LANES - t)).astype(BF16)
    for h in range(ATT_HG):
        sl = slice(h * HEAD_DIM, (h + 1) * HEAD_DIM)
        col = base + h * HEAD_DIM
        q = u_ref[0, :, col:col + HEAD_DIM].astype(BF16)
        k_new = u_ref[0, :, N_DIL * ATT_W + col:N_DIL * ATT_W + col + HEAD_DIM]
        v_new = u_ref[0, :, 2 * N_DIL * ATT_W + col:2 * N_DIL * ATT_W + col + HEAD_DIM]
        k_t = cache_ref[0, 0, 0, h]
        v_t = cache_ref[0, 0, 1, h]
        s_old = _dot(q, k_t.astype(BF16)) * scale + bo_ref[h]
        s_new = _dot(q, k_new.astype(BF16), NT) * scale + bn_ref[h]
        m = jnp.maximum(jnp.max(s_old, axis=-1, keepdims=True), jnp.max(s_new, axis=-1, keepdims=True))
        e_old = jnp.exp(s_old - m)
        e_new = jnp.exp(s_new - m)
        den = jnp.sum(e_old, axis=-1, keepdims=True) + jnp.sum(e_new, axis=-1, keepdims=True)
        o = _dot((e_old / den).astype(BF16), v_t.astype(BF16), NT)
        p_new = e_new / den
        for j in range(t):
            o = o + p_new[:, j:j + 1] * v_new[j:j + 1, :]
        o_ref[0, :, sl] = o
        lse_ref[0, :, sl] = jnp.broadcast_to(m + jnp.log(den), o.shape)
        for kv, (old, new) in enumerate(((k_t, k_new), (v_t, v_new))):
            moved = pltpu.roll(old, rows - t, axis=1)
            tail = jnp.where(lane >= LANES - t, _mm(new, place, TN, 3, 1), moved[:, rows - LANES:])
            if rows > LANES:
                new_ref[0, 0, kv, h, :, 0:rows - LANES] = moved[:, 0:rows - LANES]
            new_ref[0, 0, kv, h, :, rows - LANES:rows] = tail


def _attn_sample(ua, cache_t, prev, bias_old, bias_new, *, g, layer):
    bsz, t, _ = ua.shape
    out = pl.BlockSpec((1, t, ATT_W), lambda b: (b, 0, 0))
    cb = pl.BlockSpec((1, 1) + cache_t.shape[2:], lambda b: (layer, b, 0, 0, 0, 0))
    full = lambda a: pl.BlockSpec(a.shape, lambda b: (0,) * a.ndim)
    in_specs = [pl.BlockSpec((1, t, ATT_QKV), lambda b: (b, 0, 0)), cb, full(bias_old), full(bias_new)]
    args = [ua, cache_t, bias_old, bias_new]
    aliases = {}
    if prev is not None:
        in_specs.append(pl.BlockSpec(memory_space=pl.ANY))
        args.append(prev)
        aliases = {len(args) - 1: 2}
    return pl.pallas_call(
        functools.partial(_attn_sample_kernel, g=g),
        grid=(bsz,),
        in_specs=in_specs,
        out_specs=[out, out, cb],
        out_shape=[jax.ShapeDtypeStruct((bsz, t, ATT_W), F32)] * 2
                  + [jax.ShapeDtypeStruct(cache_t.shape, F32)],
        input_output_aliases=aliases,
        compiler_params=_cparams("parallel"),
        name="attn_sample",
    )(*args)


def _rel_bucket(dist):
    max_exact = N_BUCKETS // 2
    large = max_exact + (jnp.log(jnp.maximum(dist, 1).astype(F32) / max_exact)
                         / math.log(MAX_DISTANCE / max_exact) * (N_BUCKETS - max_exact)).astype(jnp.int32)
    return jnp.where(dist < max_exact, dist, jnp.minimum(large, N_BUCKETS - 1))


def _bias_lookup(tab, dist):
    onehot = (_rel_bucket(jnp.asarray(dist))[..., None] == jnp.arange(N_BUCKETS)).astype(F32)
    return jnp.einsum("...b,bh->h...", onehot, tab, precision=lax.Precision.HIGHEST)


def _prompt_bias(tab, dil):
    span = ATT_SPAN
    qi = np.arange(span)[:, None]
    ki = np.arange(2 * span)[None, :]
    j = span + qi - ki
    band = (j >= 0) & (j <= span)
    bias = _bias_lookup(tab, dil * np.clip(j, 0, span))
    rest = jnp.where(band[None], bias, NEG_INF)
    first = jnp.where((band & (ki >= span))[None], bias, NEG_INF)
    return jnp.stack([first, rest])


def _sample_bias(tab, dil, rows, t):
    rho = np.arange(rows + t)[None, :]
    d = rows + np.arange(t)[:, None] - rho
    valid = (d >= 0) & (d % dil == 0) & (d // dil <= ATT_SPAN)
    bias = jnp.where(valid[None], _bias_lookup(tab, np.clip(d, 0, None)), NEG_INF)
    return bias[:, :, :rows], bias[:, :, rows:]


def _merge_kernel(x_ref, yc_ref, yr_ref, o0_ref, o1_ref, o2_ref, l0_ref, l1_ref, l2_ref, gates_ref,
                  wc_ref, wr_ref, wa_ref, wo_ref, g_ref, b_ref, rt_ref, *rest, alpha):
    h_ref, gate_ref, eid_ref = rest[-3:]
    l0, l1, l2 = l0_ref[...], l1_ref[...], l2_ref[...]
    m = jnp.maximum(jnp.maximum(l0, l1), l2)
    e0, e1, e2 = jnp.exp(l0 - m), jnp.exp(l1 - m), jnp.exp(l2 - m)
    den = e0 + e1 + e2
    o = o0_ref[...] * (e0 / den) + o1_ref[...] * (e1 / den) + o2_ref[...] * (e2 / den)
    d = D_MODEL
    merged = (gates_ref[:, 0:d] * _dot(yc_ref[...], wc_ref[...])
              + gates_ref[:, d:2 * d] * _dot(yr_ref[...], wr_ref[...])
              + gates_ref[:, 2 * d:3 * d] * _dot(o.astype(BF16), wa_ref[...]))
    pre = alpha * x_ref[...] + _dot(merged.astype(BF16), wo_ref[...])
    h = _layer_norm(pre, g_ref[...], b_ref[...], LN_EPS)
    for c in range(ROW_TILE):
        h_ref[pl.ds(c, h.shape[0], stride=ROW_TILE), :] = h[:, c * LANES:(c + 1) * LANES]
    logits = _mm(h, rt_ref[...], NN, 2, 2)
    lane = lax.broadcasted_iota(jnp.int32, logits.shape, 1)
    big = jnp.int32(ROUTER_PAD)
    is_grp = lane < N_GROUPS
    lg = jnp.where(is_grp, logits, NEG_INF)
    m_g = jnp.max(lg, axis=-1, keepdims=True)
    grp = jnp.min(jnp.where(lg == m_g, lane, big), axis=-1, keepdims=True)
    p_grp = 1.0 / jnp.sum(jnp.where(is_grp, jnp.exp(logits - m_g), 0.0), axis=-1, keepdims=True)
    lo_lane = N_GROUPS + grp * EXPERTS_PER_GROUP
    in_grp = (lane >= lo_lane) & (lane < lo_lane + EXPERTS_PER_GROUP)
    le = jnp.where(in_grp, logits, NEG_INF)
    v1 = jnp.max(le, axis=-1, keepdims=True)
    i1 = jnp.min(jnp.where(le == v1, lane, big), axis=-1, keepdims=True)
    le = jnp.where(lane == i1, NEG_INF, le)
    v2 = jnp.max(le, axis=-1, keepdims=True)
    i2 = jnp.min(jnp.where(le == v2, lane, big), axis=-1, keepdims=True)
    e2 = jnp.exp(v2 - v1)
    den = 1.0 + e2
    gate_ref[...] = jnp.where(lane == 0, p_grp * (1.0 / den), jnp.where(lane == 1, p_grp * (e2 / den), 0.0))
    eid_ref[...] = jnp.where(lane == 0, i1 - N_GROUPS, jnp.where(lane == 1, i2 - N_GROUPS, 0))


def _merge(x, yc, yr, os_, ls_, gates, wc, wr, wa, wo, g, b, rt, *, alpha, tm, row0, n_total, prev):
    n = x.shape[0]
    row = lambda c: pl.BlockSpec((tm, c), lambda i: (i, 0))
    full = lambda a: pl.BlockSpec(a.shape, lambda i: (0,) * a.ndim)
    b0 = row0 // tm
    in_specs = ([row(D_MODEL), row(CONV_DIM), row(RWKV_C)] + [row(ATT_W)] * 6 + [row(N_BRANCH * D_MODEL)]
                + [full(a) for a in (wc, wr, wa, wo, g, b, rt)])
    args = [x, yc, yr, *os_, *ls_, gates, wc, wr, wa, wo, g, b, rt]
    aliases = {}
    if prev is not None:
        aliases = {len(args) + k: k for k in range(len(prev))}
        in_specs += [pl.BlockSpec(memory_space=pl.ANY)] * len(prev)
        args += list(prev)
    return pl.pallas_call(
        functools.partial(_merge_kernel, alpha=alpha),
        grid=(n // tm,),
        in_specs=in_specs,
        out_specs=[pl.BlockSpec((tm * ROW_TILE, LANES), lambda i: (b0 + i, 0)),
                   pl.BlockSpec((tm, ROUTER_PAD), lambda i: (b0 + i, 0)),
                   pl.BlockSpec((tm, ROUTER_PAD), lambda i: (b0 + i, 0))],
        out_shape=[jax.ShapeDtypeStruct((n_total * ROW_TILE, LANES), F32),
                   jax.ShapeDtypeStruct((n_total, ROUTER_PAD), F32),
                   jax.ShapeDtypeStruct((n_total, ROUTER_PAD), jnp.int32)],
        input_output_aliases=aliases,
        compiler_params=_cparams("parallel"),
        name="merge",
    )(*args)


assert TOP_K & (TOP_K - 1) == 0
ROW_TILE = D_MODEL // LANES


def _expert_kernel(be_ref, off_ref, cnt_ref, order_ref, h_hbm, wg_ref, wu_ref, wd_ref, out_hbm,
                   xbuf, ybuf, gsem, ssem):
    i = pl.program_id(0)
    last = pl.num_programs(0) - 1
    slot = i & 1
    other = 1 - slot
    n_out = out_hbm.shape[0] // ROW_TILE - 2 * MOE_BLOCK
    block_rows = MOE_BLOCK * ROW_TILE

    def tile(row):
        return pl.ds(pl.multiple_of(row * ROW_TILE, ROW_TILE), ROW_TILE)

    def chunk(c):
        return pl.ds(c, MOE_BLOCK, stride=ROW_TILE)

    def token_of(assignment):
        return lax.shift_right_logical(assignment, TOP_K.bit_length() - 1)

    def gather_wait(s):
        pltpu.make_async_copy(h_hbm.at[pl.ds(0, block_rows)], xbuf.at[s], gsem.at[s]).wait()

    def scatter_wait(s):
        pltpu.make_async_copy(ybuf.at[s], out_hbm.at[pl.ds(0, block_rows)], ssem.at[s]).wait()

    active = cnt_ref[i] > 0
    prev = jnp.maximum(i - 1, 0)
    prev_active = (i > 0) & (cnt_ref[prev] > 0)

    @pl.when(i == 0)
    def _():
        ybuf[...] = jnp.zeros(ybuf.shape, F32)
        pltpu.make_async_copy(ybuf.at[0], out_hbm.at[pl.ds(n_out * ROW_TILE, block_rows)], ssem.at[0]).start()
        base = off_ref[0]
        for r in range(MOE_BLOCK):
            tok = token_of(order_ref[base + r])
            pltpu.make_async_copy(h_hbm.at[tile(tok)], xbuf.at[0, tile(r)], gsem.at[0]).start()

    @pl.when(active | prev_active)
    def _():
        gather_wait(slot)
        scatter_wait(slot)
        nxt_base = off_ref[jnp.minimum(i + 1, last)]
        prev_base = off_ref[prev]
        prev_cnt = jnp.where(i > 0, cnt_ref[prev], 0)
        for r in range(MOE_BLOCK):
            tok = token_of(order_ref[nxt_base + r])
            pltpu.make_async_copy(h_hbm.at[tile(tok)], xbuf.at[other, tile(r)], gsem.at[other]).start()
            dst = jnp.where(r < prev_cnt, order_ref[prev_base + r], n_out + other * MOE_BLOCK + r)
            pltpu.make_async_copy(ybuf.at[other, tile(r)], out_hbm.at[tile(dst)], ssem.at[other]).start()
        gate = jnp.zeros((MOE_BLOCK, D_EXPERT), F32)
        up = jnp.zeros((MOE_BLOCK, D_EXPERT), F32)
        for c in range(ROW_TILE):
            xc = xbuf[slot, chunk(c), :].astype(BF16)
            gate = gate + _dot(xc, wg_ref[c * LANES:(c + 1) * LANES, :])
            up = up + _dot(xc, wu_ref[c * LANES:(c + 1) * LANES, :])
        act = (gate * _sigmoid(gate) * up).astype(BF16)
        for c in range(ROW_TILE):
            ybuf[slot, chunk(c), :] = _dot(act, wd_ref[:, c * LANES:(c + 1) * LANES])

    @pl.when(prev_active & jnp.logical_not(active))
    def _():
        gather_wait(other)
        scatter_wait(other)


def _experts(ht, block_e, block_off, block_cnt, order, wg, wu, wd, *, layer):
    n = ht.shape[0] // ROW_TILE
    n_blocks = block_e.shape[0]
    wmap = lambda i, be, off, cnt, order: (layer, be[i], 0, 0)
    grid_spec = pltpu.PrefetchScalarGridSpec(
        num_scalar_prefetch=4,
        grid=(n_blocks,),
        in_specs=[pl.BlockSpec(memory_space=pl.ANY),
                  pl.BlockSpec((None, None, D_MODEL, D_EXPERT), wmap),
                  pl.BlockSpec((None, None, D_MODEL, D_EXPERT), wmap),
                  pl.BlockSpec((None, None, D_EXPERT, D_MODEL), wmap)],
        out_specs=pl.BlockSpec(memory_space=pl.ANY),
        scratch_shapes=[pltpu.VMEM((2, MOE_BLOCK * ROW_TILE, LANES), F32),
                        pltpu.VMEM((2, MOE_BLOCK * ROW_TILE, LANES), F32),
                        pltpu.SemaphoreType.DMA((2,)), pltpu.SemaphoreType.DMA((2,))],
    )
    return pl.pallas_call(
        _expert_kernel,
        grid_spec=grid_spec,
        out_shape=jax.ShapeDtypeStruct(((TOP_K * n + 2 * MOE_BLOCK) * ROW_TILE, LANES), F32),
        compiler_params=_cparams("arbitrary"),
        name="experts",
    )(block_e, block_off, block_cnt, order, ht, wg, wu, wd)


def _route(eid):
    n = eid.shape[0]
    a = n * TOP_K
    n_blocks = -(-a // MOE_BLOCK) + N_EXPERTS
    skey = jnp.sort(eid.reshape(-1) * a + jnp.arange(a, dtype=jnp.int32))
    order = skey % a
    experts = jnp.arange(N_EXPERTS, dtype=jnp.int32)
    start = jnp.sum(skey[None, :] < (experts * a)[:, None], axis=1, dtype=jnp.int32)
    counts = jnp.sum(skey[None, :] < ((experts + 1) * a)[:, None], axis=1, dtype=jnp.int32) - start
    padded = (counts + MOE_BLOCK - 1) // MOE_BLOCK * MOE_BLOCK
    pend = jnp.cumsum(padded)
    pstart = pend - padded
    blk_start = jnp.arange(n_blocks, dtype=jnp.int32) * MOE_BLOCK
    block_e = jnp.minimum(jnp.sum(pend[None, :] <= blk_start[:, None], axis=1, dtype=jnp.int32), N_EXPERTS - 1)
    pick = (block_e[:, None] == experts[None, :]).astype(jnp.int32)
    within = blk_start - jnp.sum(pick * pstart[None, :], axis=1)
    block_cnt = jnp.where(blk_start < pend[-1],
                          jnp.clip(jnp.sum(pick * counts[None, :], axis=1) - within, 0, MOE_BLOCK), 0)
    block_off = jnp.where(block_cnt > 0, jnp.sum(pick * start[None, :], axis=1) + within, 0)
    order = jnp.concatenate([order, jnp.zeros((MOE_BLOCK,), jnp.int32)])
    return block_e, block_off.astype(jnp.int32), block_cnt.astype(jnp.int32), order


def _ln2_kernel(h_ref, f_ref, gate_ref, g_ref, b_ref, y_ref, yb_ref, *, alpha):
    tm = y_ref.shape[0]
    g0, g1 = gate_ref[:, 0:1], gate_ref[:, 1:2]
    pre = []
    for c in range(ROW_TILE):
        f0 = f_ref[pl.ds(c, tm, stride=TOP_K * ROW_TILE), :]
        f1 = f_ref[pl.ds(ROW_TILE + c, tm, stride=TOP_K * ROW_TILE), :]
        pre.append(alpha * h_ref[pl.ds(c, tm, stride=ROW_TILE), :] + (f0 * g0 + f1 * g1))
    mu = sum(jnp.sum(p, axis=-1, keepdims=True) for p in pre) * (1.0 / D_MODEL)
    cen = [p - mu for p in pre]
    var = sum(jnp.sum(p * p, axis=-1, keepdims=True) for p in cen) * (1.0 / D_MODEL)
    rstd = lax.rsqrt(var + LN_EPS)
    for c in range(ROW_TILE):
        sl = slice(c * LANES, (c + 1) * LANES)
        y = cen[c] * rstd * g_ref[:, sl] + b_ref[:, sl]
        y_ref[:, sl] = y
        yb_ref[:, sl] = y.astype(BF16)


def _ln2(ht, ft, gate, g, b, *, alpha, tm, row0, n):
    vec = pl.BlockSpec((1, D_MODEL), lambda i: (0, 0))
    b0 = row0 // tm
    return pl.pallas_call(
        functools.partial(_ln2_kernel, alpha=alpha),
        grid=(n // tm,),
        in_specs=[pl.BlockSpec((tm * ROW_TILE, LANES), lambda i: (b0 + i, 0)),
                  pl.BlockSpec((tm * TOP_K * ROW_TILE, LANES), lambda i: (b0 + i, 0)),
                  pl.BlockSpec((tm, ROUTER_PAD), lambda i: (b0 + i, 0)), vec, vec],
        out_specs=[pl.BlockSpec((tm, D_MODEL), lambda i: (i, 0))] * 2,
        out_shape=[jax.ShapeDtypeStruct((n, D_MODEL), F32), jax.ShapeDtypeStruct((n, D_MODEL), BF16)],
        compiler_params=_cparams("parallel"),
        name="ln2",
    )(ht, ft, gate, g, b)


def _row(v):
    return v.reshape(1, -1)


def _mixers(x, xb, p, conv_state, shift, wkv, caches, *, alpha, prompt, layer, row0, n_total, moe_prev):
    bsz, t, d = x.shape
    n = bsz * t
    xb2 = xb.reshape(n, d)
    o1 = 2 * CONV_DIM
    o2 = o1 + RWKV_U
    o3 = o2 + ATT_QKV
    tm_in = min(n, 1024)
    z = _proj(xb2, p["w_in"], p["b_in"], layer=layer, col0=0, n_cols=o1, epilogue="glu", tn=D_MODEL, tm=tm_in)
    gates = _proj(xb2, p["w_in"], p["b_in"], layer=layer, col0=o3, n_cols=N_BRANCH * D_MODEL,
                  epilogue="sigmoid_bias", tn=D_MODEL, tm=tm_in)
    ur = _proj(xb2, p["w_rwkv"], p["b_in"], layer=layer, col0=0, n_cols=RWKV_U, epilogue="none",
               tn=RWKV_U, tm=tm_in)
    ua = _proj(xb2, p["w_att"], p["b_in"], layer=layer, col0=0, n_cols=ATT_QKV, epilogue="none",
               tn=ATT_QKV // 2, tm=tm_in)
    z = z.reshape(bsz, t, CONV_DIM)
    ur = ur.reshape(bsz, t, RWKV_U)
    ua = ua.reshape(bsz, t, ATT_QKV)

    hist = jnp.pad(conv_state, ((0, 0), (CONV_HIST - (CONV_K - 1), 0), (0, 0)))
    yc = _conv(z, hist, p["conv_dw"], p["conv_dw_b"], p["conv_ln_g"], p["conv_ln_b"], tt=256 if prompt else t)
    conv_new = jnp.concatenate([conv_state, z], axis=1)[:, -(CONV_K - 1):]

    shift8 = jnp.pad(shift[:, None, :], ((0, 0), (SUBLANES - 1, 0), (0, 0)))
    pre = _rwkv_pre(ur, shift8, p["rwkv_mu"], p["rwkv_w0"], p["rwkv_w2p"], p["rwkv_a0"], p["rwkv_a2p"],
                    p["rwkv_g2"], p["rwkv_kk"], p["rwkv_ka"], p["rwkv_rk"], p["ones_bd"],
                    tt=256 if prompt else t)
    chunk = 64 if prompt else SUBLANES
    if t % chunk:
        pre = [jnp.pad(a, ((0, 0), (0, chunk - t % chunk), (0, 0))) for a in pre]
    yr, wkv_new = _scan(*pre, wkv, p["rwkv_ln_g"], p["rwkv_ln_b"], chunk=chunk)
    yr = yr[:, :t]
    shift_new = ur[:, -1]

    os_, ls_, att_new = [], [], []
    for g, (window, dil) in enumerate(DILATIONS):
        if prompt:
            o, lse = _attn_prompt(ua, p["att_bias"][g], g=g, dil=dil)
            rows = min(window, t)
            kv = [ua[:, t - rows:, (w * N_DIL + g) * ATT_W:(w * N_DIL + g + 1) * ATT_W] for w in (1, 2)]
            att_new.append(jnp.stack([a.reshape(bsz, rows, ATT_HG, HEAD_DIM) for a in kv], axis=1))
        else:
            cache_t, prev = caches[g]
            o, lse, new = _attn_sample(ua, cache_t, prev, *p["att_bias"][g], g=g, layer=layer)
            att_new.append(new)
        os_.append(o.reshape(n, ATT_W))
        ls_.append(lse.reshape(n, ATT_W))

    moe_in = _merge(x.reshape(n, d), yc.reshape(n, CONV_DIM), yr.reshape(n, RWKV_C), os_, ls_, gates,
                    p["conv_proj"], p["rwkv_proj"], p["attn_proj"], p["w_out"], p["ln1_g"], p["ln1_b"],
                    p["router"], alpha=alpha, tm=min(n, 256), row0=row0, n_total=n_total, prev=moe_prev)
    return moe_in, conv_new, shift_new, wkv_new, att_new


def kernel(x_prompt, x_sample, state_conv, state_shift, state_wkv, cache_attn_w128, cache_attn_w512, cache_attn_w2048, rel_bias, w_in, b_gate, conv_dw, conv_dw_b, conv_ln_g, conv_ln_b, conv_proj, rwkv_mu, rwkv_w0, rwkv_w2, rwkv_a0, rwkv_a2, rwkv_g2, rwkv_kk, rwkv_ka, rwkv_rk, rwkv_ln_g, rwkv_ln_b, rwkv_proj, attn_proj, w_out, ln1_g, ln1_b, router_group, router_expert, moe_w_gate, moe_w_up, moe_w_down, ln2_g, ln2_b):
    depth = w_in.shape[0]
    alpha = (2 * depth) ** 0.25
    bp, tp, _ = x_prompt.shape
    ts = x_sample.shape[1]
    n_p, n_s = bp * tp, x_sample.shape[0] * ts
    caches_t = [jnp.transpose(c, (0, 1, 2, 4, 5, 3)) for c in (cache_attn_w128, cache_attn_w512, cache_attn_w2048)]
    new_caches = [None] * N_DIL
    head = np.arange(RWKV_C) // RWKV_N
    ones_bd = jnp.asarray(head[:, None] == head[None, :], BF16)
    tabs = [rel_bias[:, g * ATT_HG:(g + 1) * ATT_HG] for g in range(N_DIL)]
    bias_p = [_prompt_bias(tabs[g], dil) for g, (_, dil) in enumerate(DILATIONS)]
    bias_s = [_sample_bias(tabs[g], dil, caches_t[g].shape[-1], ts) for g, (_, dil) in enumerate(DILATIONS)]

    w_in_b = w_in.astype(BF16)
    o_rwkv = 2 * CONV_DIM
    o_att = o_rwkv + RWKV_U
    w_rwkv_b = w_in_b[:, :, o_rwkv:o_att]
    w_att_b = w_in_b[:, :, o_att:o_att + ATT_QKV]
    b_in =jnp.pad(b_gate, ((0, 0), (w_in.shape[2] - b_gate.shape[1], 0)))[:, None, :]
    moe_b = [w.astype(BF16) for w in (moe_w_gate, moe_w_up, moe_w_down)]

    xp, xs = x_prompt, x_sample
    xpb, xsb = xp.astype(BF16), xs.astype(BF16)
    outs_p, outs_s = [], []
    for l in range(depth):
        zeros_lora = jnp.zeros((LORA_W, RWKV_C), F32)
        router = jnp.concatenate(
            [router_group[l], router_expert[l],
             jnp.zeros((D_MODEL, ROUTER_PAD - N_GROUPS - N_EXPERTS), F32)], axis=1)
        p = dict(
            w_in=w_in_b, b_in=b_in, w_rwkv=w_rwkv_b, w_att=w_att_b,
            conv_dw=conv_dw[l], conv_dw_b=_row(conv_dw_b[l]), conv_ln_g=_row(conv_ln_g[l]),
            conv_ln_b=_row(conv_ln_b[l]), conv_proj=conv_proj[l].astype(BF16),
            rwkv_mu=_row(rwkv_mu[l]), rwkv_w0=_row(rwkv_w0[l]),
            rwkv_w2p=jnp.concatenate([rwkv_w2[l], zeros_lora], axis=0).astype(BF16),
            rwkv_a0=_row(rwkv_a0[l]),
            rwkv_a2p=jnp.concatenate([zeros_lora, rwkv_a2[l]], axis=0).astype(BF16),
            rwkv_g2=rwkv_g2[l].astype(BF16), rwkv_kk=_row(rwkv_kk[l]), rwkv_ka=_row(rwkv_ka[l]),
            rwkv_rk=_row(rwkv_rk[l]), rwkv_ln_g=_row(rwkv_ln_g[l]), rwkv_ln_b=_row(rwkv_ln_b[l]),
            rwkv_proj=rwkv_proj[l].astype(BF16), attn_proj=attn_proj[l].astype(BF16),
            w_out=w_out[l].astype(BF16), ln1_g=_row(ln1_g[l]), ln1_b=_row(ln1_b[l]), router=router,
            ln2_g=_row(ln2_g[l]), ln2_b=_row(ln2_b[l]),
            ones_bd=ones_bd)
        moe_in, *new_p = _mixers(
            xp, xpb, dict(p, att_bias=bias_p),
            jnp.zeros((bp, CONV_K - 1, CONV_DIM), F32), jnp.zeros((bp, RWKV_U), F32),
            jnp.zeros((bp, RWKV_H, RWKV_N, RWKV_N), F32), None, alpha=alpha, prompt=True, layer=l,
            row0=0, n_total=n_p + n_s, moe_prev=None)
        moe_in, *new_s = _mixers(
            xs, xsb, dict(p, att_bias=bias_s), state_conv[l], state_shift[l], state_wkv[l],
            list(zip(caches_t, new_caches)), alpha=alpha, prompt=False, layer=l,
            row0=n_p, n_total=n_p + n_s, moe_prev=moe_in)
        ht, gate, eid = moe_in
        ft = _experts(ht, *_route(eid[:, :TOP_K]), *moe_b, layer=l)
        xp, xpb = _ln2(ht, ft, gate, p["ln2_g"], p["ln2_b"], alpha=alpha, tm=512, row0=0, n=n_p)
        xs, xsb = _ln2(ht, ft, gate, p["ln2_g"], p["ln2_b"], alpha=alpha, tm=n_s, row0=n_p, n=n_s)
        xp, xpb = xp.reshape(x_prompt.shape), xpb.reshape(x_prompt.shape)
        xs, xsb = xs.reshape(x_sample.shape), xsb.reshape(x_sample.shape)
        new_caches = new_s[3]
        outs_p.append(new_p)
        outs_s.append(new_s)

    def stack(outs, i):
        return jnp.stack([o[i] for o in outs])

    att_p = [jnp.stack([o[3][g] for o in outs_p]) for g in range(N_DIL)]
    att_s = [jnp.transpose(c, (0, 1, 2, 5, 3, 4)) for c in new_caches]
    return (xp, xs, stack(outs_p, 0), stack(outs_s, 0), stack(outs_p, 1), stack(outs_s, 1),
            stack(outs_p, 2), stack(outs_s, 2),
            att_p[0], att_s[0], att_p[1], att_s[1], att_p[2], att_s[2])
```

```python
import functools
import math

import jax
import jax.numpy as jnp
import numpy as np
from jax import lax
from jax.experimental import pallas as pl
from jax.experimental.pallas import tpu as pltpu

F32 = jnp.float32
BF16 = jnp.bfloat16

D_MODEL = 1024
CONV_DIM = D_MODEL // 2
CONV_K = 31
RWKV_N = 64
RWKV_H = D_MODEL // 128
RWKV_C = RWKV_H * RWKV_N
LORA_W, LORA_A, LORA_G = 64, 64, 128
RWKV_U = 3 * RWKV_C + LORA_W + LORA_A + LORA_G
RWKV_LN_EPS = 64e-5
HEAD_DIM = 64
ATT_HG = 4
ATT_W = ATT_HG * HEAD_DIM
DILATIONS = ((128, 1), (512, 4), (2048, 16))
N_DIL = len(DILATIONS)
ATT_QKV = 3 * N_DIL * ATT_W
N_BUCKETS = 32
MAX_DISTANCE = 2048
N_BRANCH = 3
N_GROUPS = 4
EXPERTS_PER_GROUP = 8
N_EXPERTS = N_GROUPS * EXPERTS_PER_GROUP
TOP_K = 2
D_EXPERT = D_MODEL // 2
MOE_BLOCK = 128
LN_EPS = 1e-5
NEG_INF = -1e30
ROUTER_PAD = 128

LANES = 128
SUBLANES = 8
VMEM_LIMIT = 48 * 1024 * 1024

NN = (((1,), (0,)), ((), ()))
NT = (((1,), (1,)), ((), ()))
TN = (((0,), (0,)), ((), ()))


def _cparams(*sem):
    return pltpu.CompilerParams(dimension_semantics=sem, vmem_limit_bytes=VMEM_LIMIT)


def _dot(a, b, dims=NN):
    return lax.dot_general(a, b, dims, preferred_element_type=F32)


def _pieces(x, n):
    if x.dtype == BF16:
        return [x]
    out, r = [], x
    for i in range(n):
        p = r.astype(BF16)
        out.append(p)
        if i + 1 < n:
            r = r - p.astype(F32)
    return out


def _mm(a, b, dims=NN, pa=1, pb=1):
    ap, bp = _pieces(a, pa), _pieces(b, pb)
    order = max(len(ap), len(bp))
    acc = None
    for i in reversed(range(len(ap))):
        for j in reversed(range(len(bp))):
            if i + j < order:
                t = _dot(ap[i], bp[j], dims)
                acc = t if acc is None else acc + t
    return acc


def _sigmoid(x):
    return 1.0 / (1.0 + jnp.exp(-x))


def _layer_norm(x, g, b, eps):
    mu = jnp.mean(x, axis=-1, keepdims=True)
    xc = x - mu
    var = jnp.mean(xc * xc, axis=-1, keepdims=True)
    return xc * lax.rsqrt(var + eps) * g + b


def _proj_kernel(x_ref, w_ref, b_ref, o_ref, *, epilogue):
    acc = _dot(x_ref[...], w_ref[...])
    if epilogue == "glu":
        half = acc.shape[1] // 2
        o_ref[...] = acc[:, :half] * _sigmoid(acc[:, half:])
    elif epilogue == "sigmoid_bias":
        o_ref[...] = _sigmoid(acc + b_ref[...])
    else:
        o_ref[...] = acc


def _proj(x, w_all, b_all, *, layer, col0, n_cols, epilogue, tn, tm):
    n, k = x.shape
    out_cols = n_cols // 2 if epilogue == "glu" else n_cols
    out_tn = tn // 2 if epilogue == "glu" else tn
    c0 = col0 // tn
    mt, nt = n // tm, n_cols // tn
    rows_outer = mt * n_cols + n <= nt * n + n_cols
    grid = (mt, nt) if rows_outer else (nt, mt)
    ij = (lambda a, b: (a, b)) if rows_outer else (lambda a, b: (b, a))
    return pl.pallas_call(
        functools.partial(_proj_kernel, epilogue=epilogue),
        grid=grid,
        in_specs=[pl.BlockSpec((tm, k), lambda a, b: (ij(a, b)[0], 0)),
                  pl.BlockSpec((None, k, tn), lambda a, b: (layer, 0, c0 + ij(a, b)[1])),
                  pl.BlockSpec((None, 1, tn), lambda a, b: (layer, 0, c0 + ij(a, b)[1]))],
        out_specs=pl.BlockSpec((tm, out_tn), lambda a, b: ij(a, b)),
        out_shape=jax.ShapeDtypeStruct((n, out_cols), F32),
        compiler_params=_cparams("parallel", "parallel"),
        name="in_proj_" + epilogue,
    )(x, w_all, b_all)


CONV_HIST = 32
CONV_ROWS = 64


def _conv_kernel(z_ref, hist_ref, dw_ref, dwb_ref, g_ref, b_ref, o_ref, zbuf):
    tt = z_ref.shape[1]
    win = CONV_HIST + tt

    @pl.when(pl.program_id(1) == 0)
    def _():
        zbuf[0, 0:CONV_HIST, :] = hist_ref[0]

    @pl.when(pl.program_id(1) != 0)
    def _():
        zbuf[0, 0:CONV_HIST, :] = zbuf[0, tt:tt + CONV_HIST, :]

    zbuf[0, CONV_HIST:win, :] = z_ref[0]
    if tt % SUBLANES == 0:
        for s in range(1, SUBLANES):
            zbuf[s, 0:win - SUBLANES, :] = zbuf[0, s:s + win - SUBLANES, :]
    pad = CONV_HIST - (CONV_K - 1)
    rb = min(tt, CONV_ROWS)
    for r0 in range(0, tt, rb):
        acc = jnp.zeros((rb, CONV_DIM), F32)
        for k in range(CONV_K):
            off = r0 + pad + k
            if tt % SUBLANES == 0:
                tap = zbuf[off % SUBLANES, off - off % SUBLANES:off - off % SUBLANES + rb, :]
            else:
                tap = zbuf[0, off:off + rb, :]
            acc = acc + tap * dw_ref[k:k + 1, :]
        y = _layer_norm(acc + dwb_ref[...], g_ref[...], b_ref[...], LN_EPS)
        o_ref[0, r0:r0 + rb, :] = (y * _sigmoid(y)).astype(BF16)


def _conv(z, hist, dw, dwb, g, b, *, tt):
    bsz, t, _ = z.shape
    vec = pl.BlockSpec((1, CONV_DIM), lambda i, j: (0, 0))
    return pl.pallas_call(
        _conv_kernel,
        grid=(bsz, t // tt),
        in_specs=[pl.BlockSpec((1, tt, CONV_DIM), lambda i, j: (i, j, 0)),
                  pl.BlockSpec((1, CONV_HIST, CONV_DIM), lambda i, j: (i, 0, 0)),
                  pl.BlockSpec((CONV_K, CONV_DIM), lambda i, j: (0, 0)),
                  vec, vec, vec],
        out_specs=pl.BlockSpec((1, tt, CONV_DIM), lambda i, j: (i, j, 0)),
        out_shape=jax.ShapeDtypeStruct((bsz, t, CONV_DIM), BF16),
        scratch_shapes=[pltpu.VMEM((SUBLANES, CONV_HIST + tt, CONV_DIM), F32)],
        compiler_params=_cparams("parallel", "arbitrary"),
        name="conv",
    )(z, hist, dw, dwb, g, b)


def _head_sum(x, ones_bd):
    return _mm(x, ones_bd, NN, pa=3, pb=1)


def _rwkv_pre_kernel(u_ref, shift_ref, mu_ref, w0_ref, w2_ref, a0_ref, a2_ref, g2_ref, kk_ref, ka_ref,
                     rk_ref, ones_ref,
                     r_ref, k_ref, v_ref, lw_ref, al_ref, be_ref, gate_ref, bonus_ref, ubuf):
    tt = u_ref.shape[1]
    c = RWKV_C

    @pl.when(pl.program_id(1) == 0)
    def _():
        ubuf[0:SUBLANES, :] = shift_ref[0]

    @pl.when(pl.program_id(1) != 0)
    def _():
        ubuf[0:SUBLANES, :] = ubuf[tt:tt + SUBLANES, :]

    u = u_ref[0]
    ubuf[SUBLANES:SUBLANES + tt, :] = u
    u_prev = ubuf[SUBLANES - 1:SUBLANES - 1 + tt, :]
    um = u + (u_prev - u) * mu_ref[...]
    r, k, v = um[:, 0:c], um[:, c:2 * c], um[:, 2 * c:3 * c]
    lo = um[:, 3 * c:3 * c + LORA_W + LORA_A]
    lane = lax.broadcasted_iota(jnp.int32, lo.shape, 1)
    lo = jnp.where(lane < LORA_W, jnp.tanh(lo), lo).astype(BF16)
    g_lo = _sigmoid(um[:, 3 * c + LORA_W + LORA_A:]).astype(BF16)
    xw = -(w0_ref[...] + _dot(lo, w2_ref[...]))
    softplus = jnp.maximum(xw, 0.0) + jnp.log(1.0 + jnp.exp(-jnp.abs(xw)))
    lw_ref[0] = -jnp.exp(-softplus - 0.5)
    a = _sigmoid(a0_ref[...] + _dot(lo, a2_ref[...]))
    gate_ref[0] = _dot(g_lo, g2_ref[...])
    ones_bd = ones_ref[...]
    kk = k * kk_ref[...]
    norm = jnp.sqrt(_head_sum(kk * kk, ones_bd))
    kk = kk / jnp.maximum(norm, 1e-12)
    k = k * (1.0 + (a - 1.0) * ka_ref[...])
    r_ref[0] = r
    k_ref[0] = k
    v_ref[0] = v
    al_ref[0] = -kk
    be_ref[0] = kk * a
    bonus_ref[0] = _head_sum(r * k * rk_ref[...], ones_bd) * v


def _rwkv_pre(u, shift8, mu, w0, w2p, a0, a2p, g2, kkp, ka, rk, ones_bd, *, tt):
    bsz, t, _ = u.shape
    full = lambda a: pl.BlockSpec(a.shape, lambda i, j: (0,) * a.ndim)
    seq = pl.BlockSpec((1, tt, RWKV_C), lambda i, j: (i, j, 0))
    return pl.pallas_call(
        _rwkv_pre_kernel,
        grid=(bsz, t // tt),
        in_specs=[pl.BlockSpec((1, tt, RWKV_U), lambda i, j: (i, j, 0)),
                  pl.BlockSpec((1, SUBLANES, RWKV_U), lambda i, j: (i, 0, 0)),
                  full(mu), full(w0), full(w2p), full(a0), full(a2p), full(g2), full(kkp), full(ka),
                  full(rk), full(ones_bd)],
        out_specs=[seq] * 8,
        out_shape=[jax.ShapeDtypeStruct((bsz, t, RWKV_C), F32)] * 8,
        scratch_shapes=[pltpu.VMEM((SUBLANES + tt, RWKV_U), F32)],
        compiler_params=_cparams("parallel", "arbitrary"),
        name="rwkv_pre",
    )(u, shift8, mu, w0, w2p, a0, a2p, g2, kkp, ka, rk, ones_bd)


def _scan_kernel(r_ref, k_ref, v_ref, lw_ref, al_ref, be_ref, gate_ref, bonus_ref, s0_ref, g_ref, b_ref,
                 y_ref, s_ref, state):
    nbt, c = r_ref.shape[0], r_ref.shape[1]
    n = RWKV_N

    @pl.when(pl.program_id(1) == 0)
    def _():
        state[...] = s0_ref[...]

    row = lax.broadcasted_iota(jnp.int32, (c, c), 0)
    col = lax.broadcasted_iota(jnp.int32, (c, c), 1)
    lower = (row >= col).astype(BF16)
    eye = (row == col).astype(F32)
    row2 = lax.broadcasted_iota(jnp.int32, (2 * c, 2 * c), 0)
    col2 = lax.broadcasted_iota(jnp.int32, (2 * c, 2 * c), 1) & (c - 1)
    mask2 = col2 < jnp.where(row2 < c, row2, row2 - c + 1)
    zeros_v = jnp.zeros((c, n), F32)

    units = [(bi, h) for bi in range(nbt) for h in range(RWKV_H)]
    sl = lambda h: slice(h * n, (h + 1) * n)
    wide = []
    for bi in range(nbt):
        lw = lw_ref[bi]
        cum = _mm(lower, lw, NN, pa=1, pb=3)
        e_p = jnp.exp(cum)
        e_m = jnp.exp(-cum)
        cum_end = cum[c - 1:c, :]
        e_end = jnp.exp(cum_end - cum)
        wide.append(dict(
            rq=r_ref[bi] * e_p, aq=al_ref[bi] * jnp.exp(cum - lw), kd=k_ref[bi] * e_m, bd=be_ref[bi] * e_m,
            kend=k_ref[bi] * e_end, bend=be_ref[bi] * e_end, v=v_ref[bi], p_end=jnp.exp(cum_end)))
    xq = [jnp.concatenate([wide[bi]["aq"][:, sl(h)], wide[bi]["rq"][:, sl(h)]], axis=0) for bi, h in units]
    wd = [jnp.concatenate([wide[bi]["bd"][:, sl(h)], wide[bi]["kd"][:, sl(h)]], axis=0) for bi, h in units]
    zv = [jnp.concatenate([zeros_v, wide[bi]["v"][:, sl(h)]], axis=0) for bi, h in units]
    s_old = [state[bi, h] for bi, h in units]
    idx = range(len(units))
    gm = [jnp.where(mask2, _mm(xq[u], wd[u], NT), 0.0) for u in idx]
    xs = [_mm(xq[u], s_old[u], NT) for u in idx]
    gv = [_mm(gm[u], zv[u], NN) for u in idx]
    pw = [gm[u][:c, :c] for u in idx]
    inv = [eye + a for a in pw]
    span = 2
    while span < c:
        pw = [_mm(a, a, NN) for a in pw]
        inv = [i + _mm(i, a, NN) for i, a in zip(inv, pw)]
        span *= 2
    us = [_mm(inv[u], xs[u][:c] + gv[u][:c], NN) for u in idx]
    ys = [xs[u][c:] + gv[u][c:] + _mm(gm[u][c:, :c], us[u], NN) for u in idx]
    for u, (bi, h) in enumerate(units):
        uv = jnp.concatenate([us[u], wide[bi]["v"][:, sl(h)]], axis=0)
        ends = jnp.concatenate([wide[bi]["bend"][:, sl(h)], wide[bi]["kend"][:, sl(h)]], axis=0)
        state[bi, h] = s_old[u] * wide[bi]["p_end"][:, sl(h)] + _mm(uv, ends, TN)
    for u, (bi, h) in enumerate(units):
        y = ys[u]
        mu = jnp.mean(y, axis=-1, keepdims=True)
        yc = y - mu
        var = jnp.mean(yc * yc, axis=-1, keepdims=True)
        y = yc * lax.rsqrt(var + RWKV_LN_EPS) * g_ref[:, sl(h)] + b_ref[:, sl(h)]
        y_ref[bi, :, sl(h)] = ((y + bonus_ref[bi, :, sl(h)]) * gate_ref[bi, :, sl(h)]).astype(BF16)

    @pl.when(pl.program_id(1) == pl.num_programs(1) - 1)
    def _():
        s_ref[...] = state[...]


SCAN_ROWS = 4


def _scan(r, k, v, lw, al, be, gate, bonus, s0, g, b, *, chunk):
    bsz, t, _ = r.shape
    seq = pl.BlockSpec((SCAN_ROWS, chunk, RWKV_C), lambda i, j: (i, j, 0))
    st = pl.BlockSpec((SCAN_ROWS, RWKV_H, RWKV_N, RWKV_N), lambda i, j: (i, 0, 0, 0))
    vec = pl.BlockSpec((1, RWKV_C), lambda i, j: (0, 0))
    return pl.pallas_call(
        _scan_kernel,
        grid=(bsz // SCAN_ROWS, t // chunk),
        in_specs=[seq] * 8 + [st, vec, vec],
        out_specs=[seq, st],
        out_shape=[jax.ShapeDtypeStruct((bsz, t, RWKV_C), BF16),
                   jax.ShapeDtypeStruct((bsz, RWKV_H, RWKV_N, RWKV_N), F32)],
        scratch_shapes=[pltpu.VMEM((SCAN_ROWS, RWKV_H, RWKV_N, RWKV_N), F32)],
        compiler_params=_cparams("parallel", "arbitrary"),
        name="rwkv_scan",
    )(r, k, v, lw, al, be, gate, bonus, s0, g, b)


ATT_SPAN = 128


HEADS_PER_TILE = LANES // HEAD_DIM


ATT_UNITS = 8


def _attn_prompt_kernel(q_ref, kc_ref, kp_ref, vc_ref, vp_ref, bias_ref, o_ref, lse_ref, *, dil, nq):
    scale = HEAD_DIM ** -0.5
    first = jnp.minimum(pl.program_id(2), 1)

    def rows_of(b, res):
        start = b * ATT_SPAN * dil + res
        return pl.ds(start, ATT_SPAN, stride=dil) if dil > 1 else pl.ds(start, ATT_SPAN)

    def run(units):
        loaded = []
        for b, res in units:
            cur = rows_of(b, res)
            q, kc, vc = (ref[0, cur, :] for ref in (q_ref, kc_ref, vc_ref))
            if b == 0:
                kp, vp = (ref[0, rows_of(0, res), :] for ref in (kp_ref, vp_ref))
            else:
                kp, vp = (ref[0, rows_of(b - 1, res), :] for ref in (kc_ref, vc_ref))
            loaded.append((cur, q, kc, kp, vc, vp, first if b == 0 else 1))
        heads = [(u, h) for u in range(len(units)) for h in range(HEADS_PER_TILE)]
        sls = [slice(h * HEAD_DIM, (h + 1) * HEAD_DIM) for h in range(HEADS_PER_TILE)]
        ss = []
        for u, h in heads:
            _, q, kc, kp, _, _, variant = loaded[u]
            qh = q[:, sls[h]].astype(BF16)
            s = jnp.concatenate([_dot(qh, kp[:, sls[h]].astype(BF16), NT),
                                 _dot(qh, kc[:, sls[h]].astype(BF16), NT)], axis=1)
            ss.append(s * scale + bias_ref[variant, h])
        ms = [jnp.max(s, axis=-1, keepdims=True) for s in ss]
        es = [jnp.exp(s - m) for s, m in zip(ss, ms)]
        dens = [jnp.sum(e, axis=-1, keepdims=True) for e in es]
        outs = []
        for (u, h), e, den in zip(heads, es, dens):
            _, _, _, _, vc, vp, _ = loaded[u]
            v2 = jnp.concatenate([vp[:, sls[h]], vc[:, sls[h]]], axis=0).astype(BF16)
            outs.append(_dot((e / den).astype(BF16), v2))
        for u in range(len(units)):
            cur = loaded[u][0]
            mine = [i for i, (uu, _) in enumerate(heads) if uu == u]
            o_ref[0, cur, :] = jnp.concatenate([outs[i] for i in mine], axis=1)
            lse_ref[0, cur, :] = jnp.concatenate(
                [jnp.broadcast_to(ms[i] + jnp.log(dens[i]), outs[i].shape) for i in mine], axis=1)

    if dil > ATT_UNITS:
        def group(gi, carry):
            run([(0, gi * ATT_UNITS + r) for r in range(ATT_UNITS)])
            return carry

        lax.fori_loop(0, dil // ATT_UNITS, group, 0)
    else:
        run([(b, r) for b in range(nq) for r in range(dil)])


def _attn_prompt(ua, bias, *, g, dil):
    bsz, s, _ = ua.shape
    nq = max(ATT_UNITS // dil, 1)
    prev_rows = ATT_SPAN * dil
    rows = nq * prev_rows
    tiles = ATT_W // LANES

    def col(which):
        return lambda b, hp, i: (b, i, (which * N_DIL + g) * tiles + hp)

    def col_prev(which):
        return lambda b, hp, i: (b, jnp.maximum(i * nq - 1, 0), (which * N_DIL + g) * tiles + hp)

    blk = (1, rows, LANES)
    prev = (1, prev_rows, LANES)
    return pl.pallas_call(
        functools.partial(_attn_prompt_kernel, dil=dil, nq=nq),
        grid=(bsz, tiles, s // rows),
        in_specs=[pl.BlockSpec(blk, col(0)), pl.BlockSpec(blk, col(1)), pl.BlockSpec(prev, col_prev(1)),
                  pl.BlockSpec(blk, col(2)), pl.BlockSpec(prev, col_prev(2)),
                  pl.BlockSpec((2, HEADS_PER_TILE, ATT_SPAN, 2 * ATT_SPAN), lambda b, hp, i: (0, hp, 0, 0))],
        out_specs=[pl.BlockSpec(blk, lambda b, hp, i: (b, i, hp))] * 2,
        out_shape=[jax.ShapeDtypeStruct((bsz, s, ATT_W), F32)] * 2,
        compiler_params=_cparams("parallel", "parallel", "parallel"),
        name="attn_prompt",
    )(ua, ua, ua, ua, ua, bias)


def _attn_sample_kernel(u_ref, cache_ref, bo_ref, bn_ref, *rest, g):
    o_ref, lse_ref, new_ref = rest[-3:]
    t = u_ref.shape[1]
    rows = cache_ref.shape[-1]
    scale = HEAD_DIM ** -0.5
    base = g * ATT_W
    lane = lax.broadcasted_iota(jnp.int32, (HEAD_DIM, LANES), 1)
    place = (lax.broadcasted_iota(jnp.int32, (t, LANES), 1)
             == lax.broadcasted_iota(jnp.int32, (t, LANES), 0) + (LANES - t)).astype(BF16)
    for h in range(ATT_HG):
        sl = slice(h * HEAD_DIM, (h + 1) * HEAD_DIM)
        col = base + h * HEAD_DIM
        q = u_ref[0, :, col:col + HEAD_DIM].astype(BF16)
        k_new = u_ref[0, :, N_DIL * ATT_W + col:N_DIL * ATT_W + col + HEAD_DIM]
        v_new = u_ref[0, :, 2 * N_DIL * ATT_W + col:2 * N_DIL * ATT_W + col + HEAD_DIM]
        k_t = cache_ref[0, 0, 0, h]
        v_t = cache_ref[0, 0, 1, h]
        s_old = _dot(q, k_t.astype(BF16)) * scale + bo_ref[h]
        s_new = _dot(q, k_new.astype(BF16), NT) * scale + bn_ref[h]
        m = jnp.maximum(jnp.max(s_old, axis=-1, keepdims=True), jnp.max(s_new, axis=-1, keepdims=True))
        e_old = jnp.exp(s_old - m)
        e_new = jnp.exp(s_new - m)
        den = jnp.sum(e_old, axis=-1, keepdims=True) + jnp.sum(e_new, axis=-1, keepdims=True)
        o = _dot((e_old / den).astype(BF16), v_t.astype(BF16), NT)
        p_new = e_new / den
        for j in range(t):
            o = o + p_new[:, j:j + 1] * v_new[j:j + 1, :]
        o_ref[0, :, sl] = o
        lse_ref[0, :, sl] = jnp.broadcast_to(m + jnp.log(den), o.shape)
        for kv, (old, new) in enumerate(((k_t, k_new), (v_t, v_new))):
            moved = pltpu.roll(old, rows - t, axis=1)
            tail = jnp.where(lane >= LANES - t, _mm(new, place, TN, 3, 1), moved[:, rows - LANES:])
            if rows > LANES:
                new_ref[0, 0, kv, h, :, 0:rows - LANES] = moved[:, 0:rows - LANES]
            new_ref[0, 0, kv, h, :, rows - LANES:rows] = tail


def _attn_sample(ua, cache_t, prev, bias_old, bias_new, *, g, layer):
    bsz, t, _ = ua.shape
    out = pl.BlockSpec((1, t, ATT_W), lambda b: (b, 0, 0))
    cb = pl.BlockSpec((1, 1) + cache_t.shape[2:], lambda b: (layer, b, 0, 0, 0, 0))
    full = lambda a: pl.BlockSpec(a.shape, lambda b: (0,) * a.ndim)
    in_specs = [pl.BlockSpec((1, t, ATT_QKV), lambda b: (b, 0, 0)), cb, full(bias_old), full(bias_new)]
    args = [ua, cache_t, bias_old, bias_new]
    aliases = {}
    if prev is not None:
        in_specs.append(pl.BlockSpec(memory_space=pl.ANY))
        args.append(prev)
        aliases = {len(args) - 1: 2}
    return pl.pallas_call(
        functools.partial(_attn_sample_kernel, g=g),
        grid=(bsz,),
        in_specs=in_specs,
        out_specs=[out, out, cb],
        out_shape=[jax.ShapeDtypeStruct((bsz, t, ATT_W), F32)] * 2
                  + [jax.ShapeDtypeStruct(cache_t.shape, F32)],
        input_output_aliases=aliases,
        compiler_params=_cparams("parallel"),
        name="attn_sample",
    )(*args)


def _rel_bucket(dist):
    max_exact = N_BUCKETS // 2
    large = max_exact + (jnp.log(jnp.maximum(dist, 1).astype(F32) / max_exact)
                         / math.log(MAX_DISTANCE / max_exact) * (N_BUCKETS - max_exact)).astype(jnp.int32)
    return jnp.where(dist < max_exact, dist, jnp.minimum(large, N_BUCKETS - 1))


def _bias_lookup(tab, dist):
    onehot = (_rel_bucket(jnp.asarray(dist))[..., None] == jnp.arange(N_BUCKETS)).astype(F32)
    return jnp.einsum("...b,bh->h...", onehot, tab, precision=lax.Precision.HIGHEST)


def _prompt_bias(tab, dil):
    span = ATT_SPAN
    qi = np.arange(span)[:, None]
    ki = np.arange(2 * span)[None, :]
    j = span + qi - ki
    band = (j >= 0) & (j <= span)
    bias = _bias_lookup(tab, dil * np.clip(j, 0, span))
    rest = jnp.where(band[None], bias, NEG_INF)
    first = jnp.where((band & (ki >= span))[None], bias, NEG_INF)
    return jnp.stack([first, rest])


def _sample_bias(tab, dil, rows, t):
    rho = np.arange(rows + t)[None, :]
    d = rows + np.arange(t)[:, None] - rho
    valid = (d >= 0) & (d % dil == 0) & (d // dil <= ATT_SPAN)
    bias = jnp.where(valid[None], _bias_lookup(tab, np.clip(d, 0, None)), NEG_INF)
    return bias[:, :, :rows], bias[:, :, rows:]


def _merge_kernel(x_ref, yc_ref, yr_ref, o0_ref, o1_ref, o2_ref, l0_ref, l1_ref, l2_ref, gates_ref,
                  wc_ref, wr_ref, wa_ref, wo_ref, g_ref, b_ref, rt_ref, *rest, alpha):
    h_ref, gate_ref, eid_ref = rest[-3:]
    l0, l1, l2 = l0_ref[...], l1_ref[...], l2_ref[...]
    m = jnp.maximum(jnp.maximum(l0, l1), l2)
    e0, e1, e2 = jnp.exp(l0 - m), jnp.exp(l1 - m), jnp.exp(l2 - m)
    den = e0 + e1 + e2
    o = o0_ref[...] * (e0 / den) + o1_ref[...] * (e1 / den) + o2_ref[...] * (e2 / den)
    d = D_MODEL
    merged = (gates_ref[:, 0:d] * _dot(yc_ref[...], wc_ref[...])
              + gates_ref[:, d:2 * d] * _dot(yr_ref[...], wr_ref[...])
              + gates_ref[:, 2 * d:3 * d] * _dot(o.astype(BF16), wa_ref[...]))
    pre = alpha * x_ref[...] + _dot(merged.astype(BF16), wo_ref[...])
    h = _layer_norm(pre, g_ref[...], b_ref[...], LN_EPS)
    for c in range(ROW_TILE):
        h_ref[pl.ds(c, h.shape[0], stride=ROW_TILE), :] = h[:, c * LANES:(c + 1) * LANES]
    logits = _mm(h, rt_ref[...], NN, 2, 2)
    lane = lax.broadcasted_iota(jnp.int32, logits.shape, 1)
    big = jnp.int32(ROUTER_PAD)
    is_grp = lane < N_GROUPS
    lg = jnp.where(is_grp, logits, NEG_INF)
    m_g = jnp.max(lg, axis=-1, keepdims=True)
    grp = jnp.min(jnp.where(lg == m_g, lane, big), axis=-1, keepdims=True)
    p_grp = 1.0 / jnp.sum(jnp.where(is_grp, jnp.exp(logits - m_g), 0.0), axis=-1, keepdims=True)
    lo_lane = N_GROUPS + grp * EXPERTS_PER_GROUP
    in_grp = (lane >= lo_lane) & (lane < lo_lane + EXPERTS_PER_GROUP)
    le = jnp.where(in_grp, logits, NEG_INF)
    v1 = jnp.max(le, axis=-1, keepdims=True)
    i1 = jnp.min(jnp.where(le == v1, lane, big), axis=-1, keepdims=True)
    le = jnp.where(lane == i1, NEG_INF, le)
    v2 = jnp.max(le, axis=-1, keepdims=True)
    i2 = jnp.min(jnp.where(le == v2, lane, big), axis=-1, keepdims=True)
    e2 = jnp.exp(v2 - v1)
    den = 1.0 + e2
    gate_ref[...] = jnp.where(lane == 0, p_grp * (1.0 / den), jnp.where(lane == 1, p_grp * (e2 / den), 0.0))
    eid_ref[...] = jnp.where(lane == 0, i1 - N_GROUPS, jnp.where(lane == 1, i2 - N_GROUPS, 0))


def _merge(x, yc, yr, os_, ls_, gates, wc, wr, wa, wo, g, b, rt, *, alpha, tm, row0, n_total, prev):
    n = x.shape[0]
    row = lambda c: pl.BlockSpec((tm, c), lambda i: (i, 0))
    full = lambda a: pl.BlockSpec(a.shape, lambda i: (0,) * a.ndim)
    b0 = row0 // tm
    in_specs = ([row(D_MODEL), row(CONV_DIM), row(RWKV_C)] + [row(ATT_W)] * 6 + [row(N_BRANCH * D_MODEL)]
                + [full(a) for a in (wc, wr, wa, wo, g, b, rt)])
    args = [x, yc, yr, *os_, *ls_, gates, wc, wr, wa, wo, g, b, rt]
    aliases = {}
    if prev is not None:
        aliases = {len(args) + k: k for k in range(len(prev))}
        in_specs += [pl.BlockSpec(memory_space=pl.ANY)] * len(prev)
        args += list(prev)
    return pl.pallas_call(
        functools.partial(_merge_kernel, alpha=alpha),
        grid=(n // tm,),
        in_specs=in_specs,
        out_specs=[pl.BlockSpec((tm * ROW_TILE, LANES), lambda i: (b0 + i, 0)),
                   pl.BlockSpec((tm, ROUTER_PAD), lambda i: (b0 + i, 0)),
                   pl.BlockSpec((tm, ROUTER_PAD), lambda i: (b0 + i, 0))],
        out_shape=[jax.ShapeDtypeStruct((n_total * ROW_TILE, LANES), F32),
                   jax.ShapeDtypeStruct((n_total, ROUTER_PAD), F32),
                   jax.ShapeDtypeStruct((n_total, ROUTER_PAD), jnp.int32)],
        input_output_aliases=aliases,
        compiler_params=_cparams("parallel"),
        name="merge",
    )(*args)


assert TOP_K & (TOP_K - 1) == 0
ROW_TILE = D_MODEL // LANES


def _expert_kernel(be_ref, off_ref, cnt_ref, order_ref, h_hbm, wg_ref, wu_ref, wd_ref, out_hbm,
                   xbuf, ybuf, gsem, ssem):
    i = pl.program_id(0)
    last = pl.num_programs(0) - 1
    slot = i & 1
    other = 1 - slot
    n_out = out_hbm.shape[0] // ROW_TILE - 2 * MOE_BLOCK
    block_rows = MOE_BLOCK * ROW_TILE

    def tile(row):
        return pl.ds(pl.multiple_of(row * ROW_TILE, ROW_TILE), ROW_TILE)

    def chunk(c):
        return pl.ds(c, MOE_BLOCK, stride=ROW_TILE)

    def token_of(assignment):
        return lax.shift_right_logical(assignment, TOP_K.bit_length() - 1)

    def gather_wait(s):
        pltpu.make_async_copy(h_hbm.at[pl.ds(0, block_rows)], xbuf.at[s], gsem.at[s]).wait()

    def scatter_wait(s):
        pltpu.make_async_copy(ybuf.at[s], out_hbm.at[pl.ds(0, block_rows)], ssem.at[s]).wait()

    active = cnt_ref[i] > 0
    prev = jnp.maximum(i - 1, 0)
    prev_active = (i > 0) & (cnt_ref[prev] > 0)

    @pl.when(i == 0)
    def _():
        ybuf[...] = jnp.zeros(ybuf.shape, F32)
        pltpu.make_async_copy(ybuf.at[0], out_hbm.at[pl.ds(n_out * ROW_TILE, block_rows)], ssem.at[0]).start()
        base = off_ref[0]
        for r in range(MOE_BLOCK):
            tok = token_of(order_ref[base + r])
            pltpu.make_async_copy(h_hbm.at[tile(tok)], xbuf.at[0, tile(r)], gsem.at[0]).start()

    @pl.when(active | prev_active)
    def _():
        gather_wait(slot)
        scatter_wait(slot)
        nxt_base = off_ref[jnp.minimum(i + 1, last)]
        prev_base = off_ref[prev]
        prev_cnt = jnp.where(i > 0, cnt_ref[prev], 0)
        for r in range(MOE_BLOCK):
            tok = token_of(order_ref[nxt_base + r])
            pltpu.make_async_copy(h_hbm.at[tile(tok)], xbuf.at[other, tile(r)], gsem.at[other]).start()
            dst = jnp.where(r < prev_cnt, order_ref[prev_base + r], n_out + other * MOE_BLOCK + r)
            pltpu.make_async_copy(ybuf.at[other, tile(r)], out_hbm.at[tile(dst)], ssem.at[other]).start()
        xb = jnp.concatenate([xbuf[slot, chunk(c), :].astype(BF16) for c in range(ROW_TILE)], axis=1)
        gate = _dot(xb, wg_ref[...])
        act = (gate * _sigmoid(gate) * _dot(xb, wu_ref[...])).astype(BF16)
        y = _dot(act, wd_ref[...])
        for c in range(ROW_TILE):
            ybuf[slot, chunk(c), :] = y[:, c * LANES:(c + 1) * LANES]

    @pl.when(prev_active & jnp.logical_not(active))
    def _():
        gather_wait(other)
        scatter_wait(other)


def _experts(ht, block_e, block_off, block_cnt, order, wg, wu, wd, *, layer):
    n = ht.shape[0] // ROW_TILE
    n_blocks = block_e.shape[0]
    wmap = lambda i, be, off, cnt, order: (layer, be[i], 0, 0)
    grid_spec = pltpu.PrefetchScalarGridSpec(
        num_scalar_prefetch=4,
        grid=(n_blocks,),
        in_specs=[pl.BlockSpec(memory_space=pl.ANY),
                  pl.BlockSpec((None, None, D_MODEL, D_EXPERT), wmap),
                  pl.BlockSpec((None, None, D_MODEL, D_EXPERT), wmap),
                  pl.BlockSpec((None, None, D_EXPERT, D_MODEL), wmap)],
        out_specs=pl.BlockSpec(memory_space=pl.ANY),
        scratch_shapes=[pltpu.VMEM((2, MOE_BLOCK * ROW_TILE, LANES), F32),
                        pltpu.VMEM((2, MOE_BLOCK * ROW_TILE, LANES), F32),
                        pltpu.SemaphoreType.DMA((2,)), pltpu.SemaphoreType.DMA((2,))],
    )
    return pl.pallas_call(
        _expert_kernel,
        grid_spec=grid_spec,
        out_shape=jax.ShapeDtypeStruct(((TOP_K * n + 2 * MOE_BLOCK) * ROW_TILE, LANES), F32),
        compiler_params=_cparams("arbitrary"),
        name="experts",
    )(block_e, block_off, block_cnt, order, ht, wg, wu, wd)


def _route(eid):
    n = eid.shape[0]
    a = n * TOP_K
    n_blocks = -(-a // MOE_BLOCK) + N_EXPERTS
    skey = jnp.sort(eid.reshape(-1) * a + jnp.arange(a, dtype=jnp.int32))
    order = skey % a
    experts = jnp.arange(N_EXPERTS, dtype=jnp.int32)
    start = jnp.sum(skey[None, :] < (experts * a)[:, None], axis=1, dtype=jnp.int32)
    counts = jnp.sum(skey[None, :] < ((experts + 1) * a)[:, None], axis=1, dtype=jnp.int32) - start
    padded = (counts + MOE_BLOCK - 1) // MOE_BLOCK * MOE_BLOCK
    pend = jnp.cumsum(padded)
    pstart = pend - padded
    blk_start = jnp.arange(n_blocks, dtype=jnp.int32) * MOE_BLOCK
    block_e = jnp.minimum(jnp.sum(pend[None, :] <= blk_start[:, None], axis=1, dtype=jnp.int32), N_EXPERTS - 1)
    pick = (block_e[:, None] == experts[None, :]).astype(jnp.int32)
    within = blk_start - jnp.sum(pick * pstart[None, :], axis=1)
    block_cnt = jnp.where(blk_start < pend[-1],
                          jnp.clip(jnp.sum(pick * counts[None, :], axis=1) - within, 0, MOE_BLOCK), 0)
    block_off = jnp.where(block_cnt > 0, jnp.sum(pick * start[None, :], axis=1) + within, 0)
    order = jnp.concatenate([order, jnp.zeros((MOE_BLOCK,), jnp.int32)])
    return block_e, block_off.astype(jnp.int32), block_cnt.astype(jnp.int32), order


def _ln2_kernel(h_ref, f_ref, gate_ref, g_ref, b_ref, y_ref, yb_ref, *, alpha):
    tm = y_ref.shape[0]
    g0, g1 = gate_ref[:, 0:1], gate_ref[:, 1:2]
    pre = []
    for c in range(ROW_TILE):
        f0 = f_ref[pl.ds(c, tm, stride=TOP_K * ROW_TILE), :]
        f1 = f_ref[pl.ds(ROW_TILE + c, tm, stride=TOP_K * ROW_TILE), :]
        pre.append(alpha * h_ref[pl.ds(c, tm, stride=ROW_TILE), :] + (f0 * g0 + f1 * g1))
    mu = sum(jnp.sum(p, axis=-1, keepdims=True) for p in pre) * (1.0 / D_MODEL)
    cen = [p - mu for p in pre]
    var = sum(jnp.sum(p * p, axis=-1, keepdims=True) for p in cen) * (1.0 / D_MODEL)
    rstd = lax.rsqrt(var + LN_EPS)
    for c in range(ROW_TILE):
        sl = slice(c * LANES, (c + 1) * LANES)
        y = cen[c] * rstd * g_ref[:, sl] + b_ref[:, sl]
        y_ref[:, sl] = y
        yb_ref[:, sl] = y.astype(BF16)


def _ln2(ht, ft, gate, g, b, *, alpha, tm, row0, n):
    vec = pl.BlockSpec((1, D_MODEL), lambda i: (0, 0))
    b0 = row0 // tm
    return pl.pallas_call(
        functools.partial(_ln2_kernel, alpha=alpha),
        grid=(n // tm,),
        in_specs=[pl.BlockSpec((tm * ROW_TILE, LANES), lambda i: (b0 + i, 0)),
                  pl.BlockSpec((tm * TOP_K * ROW_TILE, LANES), lambda i: (b0 + i, 0)),
                  pl.BlockSpec((tm, ROUTER_PAD), lambda i: (b0 + i, 0)), vec, vec],
        out_specs=[pl.BlockSpec((tm, D_MODEL), lambda i: (i, 0))] * 2,
        out_shape=[jax.ShapeDtypeStruct((n, D_MODEL), F32), jax.ShapeDtypeStruct((n, D_MODEL), BF16)],
        compiler_params=_cparams("parallel"),
        name="ln2",
    )(ht, ft, gate, g, b)


def _row(v):
    return v.reshape(1, -1)


def _mixers(x, xb, p, conv_state, shift, wkv, caches, *, alpha, prompt, layer, row0, n_total, moe_prev):
    bsz, t, d = x.shape
    n = bsz * t
    xb2 = xb.reshape(n, d)
    o1 = 2 * CONV_DIM
    o2 = o1 + RWKV_U
    o3 = o2 + ATT_QKV
    tm_in = min(n, 1024)
    z = _proj(xb2, p["w_in"], p["b_in"], layer=layer, col0=0, n_cols=o1, epilogue="glu", tn=D_MODEL, tm=tm_in)
    gates = _proj(xb2, p["w_in"], p["b_in"], layer=layer, col0=o3, n_cols=N_BRANCH * D_MODEL,
                  epilogue="sigmoid_bias", tn=D_MODEL, tm=tm_in)
    ur = _proj(xb2, p["w_rwkv"], p["b_in"], layer=layer, col0=0, n_cols=RWKV_U, epilogue="none",
               tn=RWKV_U, tm=tm_in)
    ua = _proj(xb2, p["w_att"], p["b_in"], layer=layer, col0=0, n_cols=ATT_QKV, epilogue="none",
               tn=ATT_QKV // 2, tm=tm_in)
    z = z.reshape(bsz, t, CONV_DIM)
    ur = ur.reshape(bsz, t, RWKV_U)
    ua = ua.reshape(bsz, t, ATT_QKV)

    hist = jnp.pad(conv_state, ((0, 0), (CONV_HIST - (CONV_K - 1), 0), (0, 0)))
    yc = _conv(z, hist, p["conv_dw"], p["conv_dw_b"], p["conv_ln_g"], p["conv_ln_b"], tt=256 if prompt else t)
    conv_new = jnp.concatenate([conv_state, z], axis=1)[:, -(CONV_K - 1):]

    shift8 = jnp.pad(shift[:, None, :], ((0, 0), (SUBLANES - 1, 0), (0, 0)))
    pre = _rwkv_pre(ur, shift8, p["rwkv_mu"], p["rwkv_w0"], p["rwkv_w2p"], p["rwkv_a0"], p["rwkv_a2p"],
                    p["rwkv_g2"], p["rwkv_kk"], p["rwkv_ka"], p["rwkv_rk"], p["ones_bd"],
                    tt=256 if prompt else t)
    chunk = 64 if prompt else SUBLANES
    if t % chunk:
        pre = [jnp.pad(a, ((0, 0), (0, chunk - t % chunk), (0, 0))) for a in pre]
    yr, wkv_new = _scan(*pre, wkv, p["rwkv_ln_g"], p["rwkv_ln_b"], chunk=chunk)
    yr = yr[:, :t]
    shift_new = ur[:, -1]

    os_, ls_, att_new = [], [], []
    for g, (window, dil) in enumerate(DILATIONS):
        if prompt:
            o, lse = _attn_prompt(ua, p["att_bias"][g], g=g, dil=dil)
            rows = min(window, t)
            kv = [ua[:, t - rows:, (w * N_DIL + g) * ATT_W:(w * N_DIL + g + 1) * ATT_W] for w in (1, 2)]
            att_new.append(jnp.stack([a.reshape(bsz, rows, ATT_HG, HEAD_DIM) for a in kv], axis=1))
        else:
            cache_t, prev = caches[g]
            o, lse, new = _attn_sample(ua, cache_t, prev, *p["att_bias"][g], g=g, layer=layer)
            att_new.append(new)
        os_.append(o.reshape(n, ATT_W))
        ls_.append(lse.reshape(n, ATT_W))

    moe_in = _merge(x.reshape(n, d), yc.reshape(n, CONV_DIM), yr.reshape(n, RWKV_C), os_, ls_, gates,
                    p["conv_proj"], p["rwkv_proj"], p["attn_proj"], p["w_out"], p["ln1_g"], p["ln1_b"],
                    p["router"], alpha=alpha, tm=min(n, 256), row0=row0, n_total=n_total, prev=moe_prev)
    return moe_in, conv_new, shift_new, wkv_new, att_new


def kernel(x_prompt, x_sample, state_conv, state_shift, state_wkv, cache_attn_w128, cache_attn_w512, cache_attn_w2048, rel_bias, w_in, b_gate, conv_dw, conv_dw_b, conv_ln_g, conv_ln_b, conv_proj, rwkv_mu, rwkv_w0, rwkv_w2, rwkv_a0, rwkv_a2, rwkv_g2, rwkv_kk, rwkv_ka, rwkv_rk, rwkv_ln_g, rwkv_ln_b, rwkv_proj, attn_proj, w_out, ln1_g, ln1_b, router_group, router_expert, moe_w_gate, moe_w_up, moe_w_down, ln2_g, ln2_b):
    depth = w_in.shape[0]
    alpha = (2 * depth) ** 0.25
    bp, tp, _ = x_prompt.shape
    ts = x_sample.shape[1]
    n_p, n_s = bp * tp, x_sample.shape[0] * ts
    caches_t = [jnp.transpose(c, (0, 1, 2, 4, 5, 3)) for c in (cache_attn_w128, cache_attn_w512, cache_attn_w2048)]
    new_caches = [None] * N_DIL
    head = np.arange(RWKV_C) // RWKV_N
    ones_bd = jnp.asarray(head[:, None] == head[None, :], BF16)
    tabs = [rel_bias[:, g * ATT_HG:(g + 1) * ATT_HG] for g in range(N_DIL)]
    bias_p = [_prompt_bias(tabs[g], dil) for g, (_, dil) in enumerate(DILATIONS)]
    bias_s = [_sample_bias(tabs[g], dil, caches_t[g].shape[-1], ts) for g, (_, dil) in enumerate(DILATIONS)]

    w_in_b = w_in.astype(BF16)
    o_rwkv = 2 * CONV_DIM
    o_att = o_rwkv + RWKV_U
    w_rwkv_b = w_in_b[:, :, o_rwkv:o_att]
    w_att_b = w_in_b[:, :, o_att:o_att + ATT_QKV]
    b_in =jnp.pad(b_gate, ((0, 0), (w_in.shape[2] - b_gate.shape[1], 0)))[:, None, :]
    moe_b = [w.astype(BF16) for w in (moe_w_gate, moe_w_up, moe_w_down)]

    xp, xs = x_prompt, x_sample
    xpb, xsb = xp.astype(BF16), xs.astype(BF16)
    outs_p, outs_s = [], []
    for l in range(depth):
        zeros_lora = jnp.zeros((LORA_W, RWKV_C), F32)
        router = jnp.concatenate(
            [router_group[l], router_expert[l],
             jnp.zeros((D_MODEL, ROUTER_PAD - N_GROUPS - N_EXPERTS), F32)], axis=1)
        p = dict(
            w_in=w_in_b, b_in=b_in, w_rwkv=w_rwkv_b, w_att=w_att_b,
            conv_dw=conv_dw[l], conv_dw_b=_row(conv_dw_b[l]), conv_ln_g=_row(conv_ln_g[l]),
            conv_ln_b=_row(conv_ln_b[l]), conv_proj=conv_proj[l].astype(BF16),
            rwkv_mu=_row(rwkv_mu[l]), rwkv_w0=_row(rwkv_w0[l]),
            rwkv_w2p=jnp.concatenate([rwkv_w2[l], zeros_lora], axis=0).astype(BF16),
            rwkv_a0=_row(rwkv_a0[l]),
            rwkv_a2p=jnp.concatenate([zeros_lora, rwkv_a2[l]], axis=0).astype(BF16),
            rwkv_g2=rwkv_g2[l].astype(BF16), rwkv_kk=_row(rwkv_kk[l]), rwkv_ka=_row(rwkv_ka[l]),
            rwkv_rk=_row(rwkv_rk[l]), rwkv_ln_g=_row(rwkv_ln_g[l]), rwkv_ln_b=_row(rwkv_ln_b[l]),
            rwkv_proj=rwkv_proj[l].astype(BF16), attn_proj=attn_proj[l].astype(BF16),
            w_out=w_out[l].astype(BF16), ln1_g=_row(ln1_g[l]), ln1_b=_row(ln1_b[l]), router=router,
            ln2_g=_row(ln2_g[l]), ln2_b=_row(ln2_b[l]),
            ones_bd=ones_bd)
        moe_in, *new_p = _mixers(
            xp, xpb, dict(p, att_bias=bias_p),
            jnp.zeros((bp, CONV_K - 1, CONV_DIM), F32), jnp.zeros((bp, RWKV_U), F32),
            jnp.zeros((bp, RWKV_H, RWKV_N, RWKV_N), F32), None, alpha=alpha, prompt=True, layer=l,
            row0=0, n_total=n_p + n_s, moe_prev=None)
        moe_in, *new_s = _mixers(
            xs, xsb, dict(p, att_bias=bias_s), state_conv[l], state_shift[l], state_wkv[l],
            list(zip(caches_t, new_caches)), alpha=alpha, prompt=False, layer=l,
            row0=n_p, n_total=n_p + n_s, moe_prev=moe_in)
        ht, gate, eid = moe_in
        ft = _experts(ht, *_route(eid[:, :TOP_K]), *moe_b, layer=l)
        xp, xpb = _ln2(ht, ft, gate, p["ln2_g"], p["ln2_b"], alpha=alpha, tm=512, row0=0, n=n_p)
        xs, xsb = _ln2(ht, ft, gate, p["ln2_g"], p["ln2_b"], alpha=alpha, tm=n_s, row0=n_p, n=n_s)
        xp, xpb = xp.reshape(x_prompt.shape), xpb.reshape(x_prompt.shape)
        xs, xsb = xs.reshape(x_sample.shape), xsb.reshape(x_sample.shape)
        new_caches = new_s[3]
        outs_p.append(new_p)
        outs_s.append(new_s)

    def stack(outs, i):
        return jnp.stack([o[i] for o in outs])

    att_p = [jnp.stack([o[3][g] for o in outs_p]) for g in range(N_DIL)]
    att_s = [jnp.transpose(c, (0, 1, 2, 5, 3, 4)) for c in new_caches]
    return (xp, xs, stack(outs_p, 0), stack(outs_s, 0), stack(outs_p, 1), stack(outs_s, 1),
            stack(outs_p, 2), stack(outs_s, 2),
            att_p[0], att_s[0], att_p[1], att_s[1], att_p[2], att_s[2])
```

```python
import functools
import math

import jax
import jax.numpy as jnp
import numpy as np
from jax import lax
from jax.experimental import pallas as pl
from jax.experimental.pallas import tpu as pltpu

F32 = jnp.float32
BF16 = jnp.bfloat16

D_MODEL = 1024
CONV_DIM = D_MODEL // 2
CONV_K = 31
RWKV_N = 64
RWKV_H = D_MODEL // 128
RWKV_C = RWKV_H * RWKV_N
LORA_W, LORA_A, LORA_G = 64, 64, 128
RWKV_U = 3 * RWKV_C + LORA_W + LORA_A + LORA_G
RWKV_LN_EPS = 64e-5
HEAD_DIM = 64
ATT_HG = 4
ATT_W = ATT_HG * HEAD_DIM
DILATIONS = ((128, 1), (512, 4), (2048, 16))
N_DIL = len(DILATIONS)
ATT_QKV = 3 * N_DIL * ATT_W
N_BUCKETS = 32
MAX_DISTANCE = 2048
N_BRANCH = 3
N_GROUPS = 4
EXPERTS_PER_GROUP = 8
N_EXPERTS = N_GROUPS * EXPERTS_PER_GROUP
TOP_K = 2
D_EXPERT = D_MODEL // 2
MOE_BLOCK = 128
LN_EPS = 1e-5
NEG_INF = -1e30
ROUTER_PAD = 128

LANES = 128
SUBLANES = 8
VMEM_LIMIT = 48 * 1024 * 1024

NN = (((1,), (0,)), ((), ()))
NT = (((1,), (1,)), ((), ()))
TN = (((0,), (0,)), ((), ()))


def _cparams(*sem):
    return pltpu.CompilerParams(dimension_semantics=sem, vmem_limit_bytes=VMEM_LIMIT)


def _dot(a, b, dims=NN):
    return lax.dot_general(a, b, dims, preferred_element_type=F32)


def _pieces(x, n):
    if x.dtype == BF16:
        return [x]
    out, r = [], x
    for i in range(n):
        p = r.astype(BF16)
        out.append(p)
        if i + 1 < n:
            r = r - p.astype(F32)
    return out


def _mm(a, b, dims=NN, pa=1, pb=1):
    ap, bp = _pieces(a, pa), _pieces(b, pb)
    order = max(len(ap), len(bp))
    acc = None
    for i in reversed(range(len(ap))):
        for j in reversed(range(len(bp))):
            if i + j < order:
                t = _dot(ap[i], bp[j], dims)
                acc = t if acc is None else acc + t
    return acc


def _sigmoid(x):
    return 1.0 / (1.0 + jnp.exp(-x))


def _layer_norm(x, g, b, eps):
    mu = jnp.mean(x, axis=-1, keepdims=True)
    xc = x - mu
    var = jnp.mean(xc * xc, axis=-1, keepdims=True)
    return xc * lax.rsqrt(var + eps) * g + b


def _proj_kernel(x_ref, w_ref, b_ref, o_ref, *, epilogue):
    acc = _dot(x_ref[...], w_ref[...])
    if epilogue == "glu":
        half = acc.shape[1] // 2
        o_ref[...] = acc[:, :half] * _sigmoid(acc[:, half:])
    elif epilogue == "sigmoid_bias":
        o_ref[...] = _sigmoid(acc + b_ref[...])
    else:
        o_ref[...] = acc


def _proj(x, w_all, b_all, *, layer, col0, n_cols, epilogue, tn, tm):
    n, k = x.shape
    out_cols = n_cols // 2 if epilogue == "glu" else n_cols
    out_tn = tn // 2 if epilogue == "glu" else tn
    c0 = col0 // tn
    mt, nt = n // tm, n_cols // tn
    rows_outer = mt * n_cols + n <= nt * n + n_cols
    grid = (mt, nt) if rows_outer else (nt, mt)
    ij = (lambda a, b: (a, b)) if rows_outer else (lambda a, b: (b, a))
    return pl.pallas_call(
        functools.partial(_proj_kernel, epilogue=epilogue),
        grid=grid,
        in_specs=[pl.BlockSpec((tm, k), lambda a, b: (ij(a, b)[0], 0)),
                  pl.BlockSpec((None, k, tn), lambda a, b: (layer, 0, c0 + ij(a, b)[1])),
                  pl.BlockSpec((None, 1, tn), lambda a, b: (layer, 0, c0 + ij(a, b)[1]))],
        out_specs=pl.BlockSpec((tm, out_tn), lambda a, b: ij(a, b)),
        out_shape=jax.ShapeDtypeStruct((n, out_cols), F32),
        compiler_params=_cparams("parallel", "parallel"),
        name="in_proj_" + epilogue,
    )(x, w_all, b_all)


CONV_HIST = 32
CONV_ROWS = 64


def _conv_kernel(z_ref, hist_ref, dw_ref, dwb_ref, g_ref, b_ref, o_ref, zbuf):
    tt = z_ref.shape[1]
    win = CONV_HIST + tt

    @pl.when(pl.program_id(1) == 0)
    def _():
        zbuf[0, 0:CONV_HIST, :] = hist_ref[0]

    @pl.when(pl.program_id(1) != 0)
    def _():
        zbuf[0, 0:CONV_HIST, :] = zbuf[0, tt:tt + CONV_HIST, :]

    zbuf[0, CONV_HIST:win, :] = z_ref[0]
    if tt % SUBLANES == 0:
        for s in range(1, SUBLANES):
            zbuf[s, 0:win - SUBLANES, :] = zbuf[0, s:s + win - SUBLANES, :]
    pad = CONV_HIST - (CONV_K - 1)
    rb = min(tt, CONV_ROWS)
    for r0 in range(0, tt, rb):
        acc = jnp.zeros((rb, CONV_DIM), F32)
        for k in range(CONV_K):
            off = r0 + pad + k
            if tt % SUBLANES == 0:
                tap = zbuf[off % SUBLANES, off - off % SUBLANES:off - off % SUBLANES + rb, :]
            else:
                tap = zbuf[0, off:off + rb, :]
            acc = acc + tap * dw_ref[k:k + 1, :]
        y = _layer_norm(acc + dwb_ref[...], g_ref[...], b_ref[...], LN_EPS)
        o_ref[0, r0:r0 + rb, :] = (y * _sigmoid(y)).astype(BF16)


def _conv(z, hist, dw, dwb, g, b, *, tt):
    bsz, t, _ = z.shape
    vec = pl.BlockSpec((1, CONV_DIM), lambda i, j: (0, 0))
    return pl.pallas_call(
        _conv_kernel,
        grid=(bsz, t // tt),
        in_specs=[pl.BlockSpec((1, tt, CONV_DIM), lambda i, j: (i, j, 0)),
                  pl.BlockSpec((1, CONV_HIST, CONV_DIM), lambda i, j: (i, 0, 0)),
                  pl.BlockSpec((CONV_K, CONV_DIM), lambda i, j: (0, 0)),
                  vec, vec, vec],
        out_specs=pl.BlockSpec((1, tt, CONV_DIM), lambda i, j: (i, j, 0)),
        out_shape=jax.ShapeDtypeStruct((bsz, t, CONV_DIM), BF16),
        scratch_shapes=[pltpu.VMEM((SUBLANES, CONV_HIST + tt, CONV_DIM), F32)],
        compiler_params=_cparams("parallel", "arbitrary"),
        name="conv",
    )(z, hist, dw, dwb, g, b)


def _head_sum(x, ones_bd):
    return _mm(x, ones_bd, NN, pa=3, pb=1)


def _rwkv_pre_kernel(u_ref, shift_ref, mu_ref, w0_ref, w2_ref, a0_ref, a2_ref, g2_ref, kk_ref, ka_ref,
                     rk_ref, ones_ref,
                     r_ref, k_ref, v_ref, lw_ref, al_ref, be_ref, gate_ref, bonus_ref, ubuf):
    tt = u_ref.shape[1]
    c = RWKV_C

    @pl.when(pl.program_id(1) == 0)
    def _():
        ubuf[0:SUBLANES, :] = shift_ref[0]

    @pl.when(pl.program_id(1) != 0)
    def _():
        ubuf[0:SUBLANES, :] = ubuf[tt:tt + SUBLANES, :]

    u = u_ref[0]
    ubuf[SUBLANES:SUBLANES + tt, :] = u
    u_prev = ubuf[SUBLANES - 1:SUBLANES - 1 + tt, :]
    um = u + (u_prev - u) * mu_ref[...]
    r, k, v = um[:, 0:c], um[:, c:2 * c], um[:, 2 * c:3 * c]
    lo = um[:, 3 * c:3 * c + LORA_W + LORA_A]
    lane = lax.broadcasted_iota(jnp.int32, lo.shape, 1)
    lo = jnp.where(lane < LORA_W, jnp.tanh(lo), lo).astype(BF16)
    g_lo = _sigmoid(um[:, 3 * c + LORA_W + LORA_A:]).astype(BF16)
    xw = -(w0_ref[...] + _dot(lo, w2_ref[...]))
    softplus = jnp.maximum(xw, 0.0) + jnp.log(1.0 + jnp.exp(-jnp.abs(xw)))
    lw_ref[0] = -jnp.exp(-softplus - 0.5)
    a = _sigmoid(a0_ref[...] + _dot(lo, a2_ref[...]))
    gate_ref[0] = _dot(g_lo, g2_ref[...])
    ones_bd = ones_ref[...]
    kk = k * kk_ref[...]
    norm = jnp.sqrt(_head_sum(kk * kk, ones_bd))
    kk = kk / jnp.maximum(norm, 1e-12)
    k = k * (1.0 + (a - 1.0) * ka_ref[...])
    r_ref[0] = r
    k_ref[0] = k
    v_ref[0] = v
    al_ref[0] = -kk
    be_ref[0] = kk * a
    bonus_ref[0] = _head_sum(r * k * rk_ref[...], ones_bd) * v


def _rwkv_pre(u, shift8, mu, w0, w2p, a0, a2p, g2, kkp, ka, rk, ones_bd, *, tt):
    bsz, t, _ = u.shape
    full = lambda a: pl.BlockSpec(a.shape, lambda i, j: (0,) * a.ndim)
    seq = pl.BlockSpec((1, tt, RWKV_C), lambda i, j: (i, j, 0))
    return pl.pallas_call(
        _rwkv_pre_kernel,
        grid=(bsz, t // tt),
        in_specs=[pl.BlockSpec((1, tt, RWKV_U), lambda i, j: (i, j, 0)),
                  pl.BlockSpec((1, SUBLANES, RWKV_U), lambda i, j: (i, 0, 0)),
                  full(mu), full(w0), full(w2p), full(a0), full(a2p), full(g2), full(kkp), full(ka),
                  full(rk), full(ones_bd)],
        out_specs=[seq] * 8,
        out_shape=[jax.ShapeDtypeStruct((bsz, t, RWKV_C), F32)] * 8,
        scratch_shapes=[pltpu.VMEM((SUBLANES + tt, RWKV_U), F32)],
        compiler_params=_cparams("parallel", "arbitrary"),
        name="rwkv_pre",
    )(u, shift8, mu, w0, w2p, a0, a2p, g2, kkp, ka, rk, ones_bd)


def _scan_kernel(r_ref, k_ref, v_ref, lw_ref, al_ref, be_ref, gate_ref, bonus_ref, s0_ref, g_ref, b_ref,
                 y_ref, s_ref, state):
    nbt, c = r_ref.shape[0], r_ref.shape[1]
    n = RWKV_N

    @pl.when(pl.program_id(1) == 0)
    def _():
        state[...] = s0_ref[...]

    row = lax.broadcasted_iota(jnp.int32, (c, c), 0)
    col = lax.broadcasted_iota(jnp.int32, (c, c), 1)
    lower = (row >= col).astype(BF16)
    eye = (row == col).astype(F32)
    row2 = lax.broadcasted_iota(jnp.int32, (2 * c, 2 * c), 0)
    col2 = lax.broadcasted_iota(jnp.int32, (2 * c, 2 * c), 1) & (c - 1)
    mask2 = col2 < jnp.where(row2 < c, row2, row2 - c + 1)
    zeros_v = jnp.zeros((c, n), F32)

    units = [(bi, h) for bi in range(nbt) for h in range(RWKV_H)]
    sl = lambda h: slice(h * n, (h + 1) * n)
    wide = []
    for bi in range(nbt):
        lw = lw_ref[bi]
        cum = _mm(lower, lw, NN, pa=1, pb=3)
        e_p = jnp.exp(cum)
        e_m = jnp.exp(-cum)
        cum_end = cum[c - 1:c, :]
        e_end = jnp.exp(cum_end - cum)
        wide.append(dict(
            rq=r_ref[bi] * e_p, aq=al_ref[bi] * jnp.exp(cum - lw), kd=k_ref[bi] * e_m, bd=be_ref[bi] * e_m,
            kend=k_ref[bi] * e_end, bend=be_ref[bi] * e_end, v=v_ref[bi], p_end=jnp.exp(cum_end)))
    xq = [jnp.concatenate([wide[bi]["aq"][:, sl(h)], wide[bi]["rq"][:, sl(h)]], axis=0) for bi, h in units]
    wd = [jnp.concatenate([wide[bi]["bd"][:, sl(h)], wide[bi]["kd"][:, sl(h)]], axis=0) for bi, h in units]
    zv = [jnp.concatenate([zeros_v, wide[bi]["v"][:, sl(h)]], axis=0) for bi, h in units]
    s_old = [state[bi, h] for bi, h in units]
    idx = range(len(units))
    gm = [jnp.where(mask2, _mm(xq[u], wd[u], NT), 0.0) for u in idx]
    xs = [_mm(xq[u], s_old[u], NT) for u in idx]
    gv = [_mm(gm[u], zv[u], NN) for u in idx]
    pw = [gm[u][:c, :c] for u in idx]
    inv = [eye + a for a in pw]
    span = 2
    while span < c:
        pw = [_mm(a, a, NN) for a in pw]
        inv = [i + _mm(i, a, NN) for i, a in zip(inv, pw)]
        span *= 2
    us = [_mm(inv[u], xs[u][:c] + gv[u][:c], NN) for u in idx]
    ys = [xs[u][c:] + gv[u][c:] + _mm(gm[u][c:, :c], us[u], NN) for u in idx]
    for u, (bi, h) in enumerate(units):
        uv = jnp.concatenate([us[u], wide[bi]["v"][:, sl(h)]], axis=0)
        ends = jnp.concatenate([wide[bi]["bend"][:, sl(h)], wide[bi]["kend"][:, sl(h)]], axis=0)
        state[bi, h] = s_old[u] * wide[bi]["p_end"][:, sl(h)] + _mm(uv, ends, TN)
    for u, (bi, h) in enumerate(units):
        y = ys[u]
        mu = jnp.mean(y, axis=-1, keepdims=True)
        yc = y - mu
        var = jnp.mean(yc * yc, axis=-1, keepdims=True)
        y = yc * lax.rsqrt(var + RWKV_LN_EPS) * g_ref[:, sl(h)] + b_ref[:, sl(h)]
        y_ref[bi, :, sl(h)] = ((y + bonus_ref[bi, :, sl(h)]) * gate_ref[bi, :, sl(h)]).astype(BF16)

    @pl.when(pl.program_id(1) == pl.num_programs(1) - 1)
    def _():
        s_ref[...] = state[...]


SCAN_ROWS = 4


def _scan(r, k, v, lw, al, be, gate, bonus, s0, g, b, *, chunk):
    bsz, t, _ = r.shape
    seq = pl.BlockSpec((SCAN_ROWS, chunk, RWKV_C), lambda i, j: (i, j, 0))
    st = pl.BlockSpec((SCAN_ROWS, RWKV_H, RWKV_N, RWKV_N), lambda i, j: (i, 0, 0, 0))
    vec = pl.BlockSpec((1, RWKV_C), lambda i, j: (0, 0))
    return pl.pallas_call(
        _scan_kernel,
        grid=(bsz // SCAN_ROWS, t // chunk),
        in_specs=[seq] * 8 + [st, vec, vec],
        out_specs=[seq, st],
        out_shape=[jax.ShapeDtypeStruct((bsz, t, RWKV_C), BF16),
                   jax.ShapeDtypeStruct((bsz, RWKV_H, RWKV_N, RWKV_N), F32)],
        scratch_shapes=[pltpu.VMEM((SCAN_ROWS, RWKV_H, RWKV_N, RWKV_N), F32)],
        compiler_params=_cparams("parallel", "arbitrary"),
        name="rwkv_scan",
    )(r, k, v, lw, al, be, gate, bonus, s0, g, b)


ATT_SPAN = 128


HEADS_PER_TILE = LANES // HEAD_DIM


ATT_UNITS = 8


def _attn_prompt_kernel(q_ref, kc_ref, kp_ref, vc_ref, vp_ref, bias_ref, o_ref, lse_ref, *, dil, nq):
    scale = HEAD_DIM ** -0.5
    first = jnp.minimum(pl.program_id(2), 1)

    def rows_of(b, res):
        start = b * ATT_SPAN * dil + res
        return pl.ds(start, ATT_SPAN, stride=dil) if dil > 1 else pl.ds(start, ATT_SPAN)

    def run(units):
        loaded = []
        for b, res in units:
            cur = rows_of(b, res)
            q, kc, vc = (ref[0, cur, :] for ref in (q_ref, kc_ref, vc_ref))
            if b == 0:
                kp, vp = (ref[0, rows_of(0, res), :] for ref in (kp_ref, vp_ref))
            else:
                kp, vp = (ref[0, rows_of(b - 1, res), :] for ref in (kc_ref, vc_ref))
            loaded.append((cur, q, kc, kp, vc, vp, first if b == 0 else 1))
        heads = [(u, h) for u in range(len(units)) for h in range(HEADS_PER_TILE)]
        sls = [slice(h * HEAD_DIM, (h + 1) * HEAD_DIM) for h in range(HEADS_PER_TILE)]
        ss = []
        for u, h in heads:
            _, q, kc, kp, _, _, variant = loaded[u]
            qh = q[:, sls[h]].astype(BF16)
            s = jnp.concatenate([_dot(qh, kp[:, sls[h]].astype(BF16), NT),
                                 _dot(qh, kc[:, sls[h]].astype(BF16), NT)], axis=1)
            ss.append(s * scale + bias_ref[variant, h])
        ms = [jnp.max(s, axis=-1, keepdims=True) for s in ss]
        es = [jnp.exp(s - m) for s, m in zip(ss, ms)]
        dens = [jnp.sum(e, axis=-1, keepdims=True) for e in es]
        outs = []
        for (u, h), e, den in zip(heads, es, dens):
            _, _, _, _, vc, vp, _ = loaded[u]
            v2 = jnp.concatenate([vp[:, sls[h]], vc[:, sls[h]]], axis=0).astype(BF16)
            outs.append(_dot((e / den).astype(BF16), v2))
        for u in range(len(units)):
            cur = loaded[u][0]
            mine = [i for i, (uu, _) in enumerate(heads) if uu == u]
            o_ref[0, cur, :] = jnp.concatenate([outs[i] for i in mine], axis=1)
            lse_ref[0, cur, :] = jnp.concatenate(
                [jnp.broadcast_to(ms[i] + jnp.log(dens[i]), outs[i].shape) for i in mine], axis=1)

    if dil > ATT_UNITS:
        def group(gi, carry):
            run([(0, gi * ATT_UNITS + r) for r in range(ATT_UNITS)])
            return carry

        lax.fori_loop(0, dil // ATT_UNITS, group, 0)
    else:
        run([(b, r) for b in range(nq) for r in range(dil)])


def _attn_prompt(ua, bias, *, g, dil):
    bsz, s, _ = ua.shape
    nq = max(ATT_UNITS // dil, 1)
    prev_rows = ATT_SPAN * dil
    rows = nq * prev_rows
    tiles = ATT_W // LANES

    def col(which):
        return lambda b, hp, i: (b, i, (which * N_DIL + g) * tiles + hp)

    def col_prev(which):
        return lambda b, hp, i: (b, jnp.maximum(i * nq - 1, 0), (which * N_DIL + g) * tiles + hp)

    blk = (1, rows, LANES)
    prev = (1, prev_rows, LANES)
    return pl.pallas_call(
        functools.partial(_attn_prompt_kernel, dil=dil, nq=nq),
        grid=(bsz, tiles, s // rows),
        in_specs=[pl.BlockSpec(blk, col(0)), pl.BlockSpec(blk, col(1)), pl.BlockSpec(prev, col_prev(1)),
                  pl.BlockSpec(blk, col(2)), pl.BlockSpec(prev, col_prev(2)),
                  pl.BlockSpec((2, HEADS_PER_TILE, ATT_SPAN, 2 * ATT_SPAN), lambda b, hp, i: (0, hp, 0, 0))],
        out_specs=[pl.BlockSpec(blk, lambda b, hp, i: (b, i, hp))] * 2,
        out_shape=[jax.ShapeDtypeStruct((bsz, s, ATT_W), F32)] * 2,
        compiler_params=_cparams("parallel", "parallel", "parallel"),
        name="attn_prompt",
    )(ua, ua, ua, ua, ua, bias)


def _attn_sample_kernel(u_ref, cache_ref, bo_ref, bn_ref, *rest, g):
    o_ref, lse_ref, new_ref = rest[-3:]
    t = u_ref.shape[1]
    rows = cache_ref.shape[-1]
    scale = HEAD_DIM ** -0.5
    base = g * ATT_W
    lane = lax.broadcasted_iota(jnp.int32, (HEAD_DIM, LANES), 1)
    place = (lax.broadcasted_iota(jnp.int32, (t, LANES), 1)
             == lax.broadcasted_iota(jnp.int32, (t, LANES), 0) + (LANES - t)).astype(BF16)
    for h in range(ATT_HG):
        sl = slice(h * HEAD_DIM, (h + 1) * HEAD_DIM)
        col = base + h * HEAD_DIM
        q = u_ref[0, :, col:col + HEAD_DIM].astype(BF16)
        k_new = u_ref[0, :, N_DIL * ATT_W + col:N_DIL * ATT_W + col + HEAD_DIM]
        v_new = u_ref[0, :, 2 * N_DIL * ATT_W + col:2 * N_DIL * ATT_W + col + HEAD_DIM]
        k_t = cache_ref[0, 0, 0, h]
        v_t = cache_ref[0, 0, 1, h]
        s_old = _dot(q, k_t.astype(BF16)) * scale + bo_ref[h]
        s_new = _dot(q, k_new.astype(BF16), NT) * scale + bn_ref[h]
        m = jnp.maximum(jnp.max(s_old, axis=-1, keepdims=True), jnp.max(s_new, axis=-1, keepdims=True))
        e_old = jnp.exp(s_old - m)
        e_new = jnp.exp(s_new - m)
        den = jnp.sum(e_old, axis=-1, keepdims=True) + jnp.sum(e_new, axis=-1, keepdims=True)
        o = _dot((e_old / den).astype(BF16), v_t.astype(BF16), NT)
        p_new = e_new / den
        for j in range(t):
            o = o + p_new[:, j:j + 1] * v_new[j:j + 1, :]
        o_ref[0, :, sl] = o
        lse_ref[0, :, sl] = jnp.broadcast_to(m + jnp.log(den), o.shape)
        for kv, (old, new) in enumerate(((k_t, k_new), (v_t, v_new))):
            moved = pltpu.roll(old, rows - t, axis=1)
            tail = jnp.where(lane >= LANES - t, _mm(new, place, TN, 3, 1), moved[:, rows - LANES:])
            if rows > LANES:
                new_ref[0, 0, kv, h, :, 0:rows - LANES] = moved[:, 0:rows - LANES]
            new_ref[0, 0, kv, h, :, rows - LANES:rows] = tail


def _attn_sample(ua, cache_t, prev, bias_old, bias_new, *, g, layer):
    bsz, t, _ = ua.shape
    out = pl.BlockSpec((1, t, ATT_W), lambda b: (b, 0, 0))
    cb = pl.BlockSpec((1, 1) + cache_t.shape[2:], lambda b: (layer, b, 0, 0, 0, 0))
    full = lambda a: pl.BlockSpec(a.shape, lambda b: (0,) * a.ndim)
    in_specs = [pl.BlockSpec((1, t, ATT_QKV), lambda b: (b, 0, 0)), cb, full(bias_old), full(bias_new)]
    args = [ua, cache_t, bias_old, bias_new]
    aliases = {}
    if prev is not None:
        in_specs.append(pl.BlockSpec(memory_space=pl.ANY))
        args.append(prev)
        aliases = {len(args) - 1: 2}
    return pl.pallas_call(
        functools.partial(_attn_sample_kernel, g=g),
        grid=(bsz,),
        in_specs=in_specs,
        out_specs=[out, out, cb],
        out_shape=[jax.ShapeDtypeStruct((bsz, t, ATT_W), F32)] * 2
                  + [jax.ShapeDtypeStruct(cache_t.shape, F32)],
        input_output_aliases=aliases,
        compiler_params=_cparams("parallel"),
        name="attn_sample",
    )(*args)


def _rel_bucket(dist):
    max_exact = N_BUCKETS // 2
    large = max_exact + (jnp.log(jnp.maximum(dist, 1).astype(F32) / max_exact)
                         / math.log(MAX_DISTANCE / max_exact) * (N_BUCKETS - max_exact)).astype(jnp.int32)
    return jnp.where(dist < max_exact, dist, jnp.minimum(large, N_BUCKETS - 1))


def _bias_lookup(tab, dist):
    onehot = (_rel_bucket(jnp.asarray(dist))[..., None] == jnp.arange(N_BUCKETS)).astype(F32)
    return jnp.einsum("...b,bh->h...", onehot, tab, precision=lax.Precision.HIGHEST)


def _prompt_bias(tab, dil):
    span = ATT_SPAN
    qi = np.arange(span)[:, None]
    ki = np.arange(2 * span)[None, :]
    j = span + qi - ki
    band = (j >= 0) & (j <= span)
    bias = _bias_lookup(tab, dil * np.clip(j, 0, span))
    rest = jnp.where(band[None], bias, NEG_INF)
    first = jnp.where((band & (ki >= span))[None], bias, NEG_INF)
    return jnp.stack([first, rest])


def _sample_bias(tab, dil, rows, t):
    rho = np.arange(rows + t)[None, :]
    d = rows + np.arange(t)[:, None] - rho
    valid = (d >= 0) & (d % dil == 0) & (d // dil <= ATT_SPAN)
    bias = jnp.where(valid[None], _bias_lookup(tab, np.clip(d, 0, None)), NEG_INF)
    return bias[:, :, :rows], bias[:, :, rows:]


def _merge_kernel(x_ref, yc_ref, yr_ref, o0_ref, o1_ref, o2_ref, l0_ref, l1_ref, l2_ref, gates_ref,
                  wc_ref, wr_ref, wa_ref, wo_ref, g_ref, b_ref, rt_ref, *rest, alpha):
    h_ref, gate_ref, eid_ref = rest[-3:]
    l0, l1, l2 = l0_ref[...], l1_ref[...], l2_ref[...]
    m = jnp.maximum(jnp.maximum(l0, l1), l2)
    e0, e1, e2 = jnp.exp(l0 - m), jnp.exp(l1 - m), jnp.exp(l2 - m)
    den = e0 + e1 + e2
    o = o0_ref[...] * (e0 / den) + o1_ref[...] * (e1 / den) + o2_ref[...] * (e2 / den)
    d = D_MODEL
    merged = (gates_ref[:, 0:d] * _dot(yc_ref[...], wc_ref[...])
              + gates_ref[:, d:2 * d] * _dot(yr_ref[...], wr_ref[...])
              + gates_ref[:, 2 * d:3 * d] * _dot(o.astype(BF16), wa_ref[...]))
    pre = alpha * x_ref[...] + _dot(merged.astype(BF16), wo_ref[...])
    h = _layer_norm(pre, g_ref[...], b_ref[...], LN_EPS)
    for c in range(ROW_TILE):
        h_ref[pl.ds(c, h.shape[0], stride=ROW_TILE), :] = h[:, c * LANES:(c + 1) * LANES]
    logits = _mm(h, rt_ref[...], NN, 2, 2)
    lane = lax.broadcasted_iota(jnp.int32, logits.shape, 1)
    big = jnp.int32(ROUTER_PAD)
    is_grp = lane < N_GROUPS
    lg = jnp.where(is_grp, logits, NEG_INF)
    m_g = jnp.max(lg, axis=-1, keepdims=True)
    grp = jnp.min(jnp.where(lg == m_g, lane, big), axis=-1, keepdims=True)
    p_grp = 1.0 / jnp.sum(jnp.where(is_grp, jnp.exp(logits - m_g), 0.0), axis=-1, keepdims=True)
    lo_lane = N_GROUPS + grp * EXPERTS_PER_GROUP
    in_grp = (lane >= lo_lane) & (lane < lo_lane + EXPERTS_PER_GROUP)
    le = jnp.where(in_grp, logits, NEG_INF)
    v1 = jnp.max(le, axis=-1, keepdims=True)
    i1 = jnp.min(jnp.where(le == v1, lane, big), axis=-1, keepdims=True)
    le = jnp.where(lane == i1, NEG_INF, le)
    v2 = jnp.max(le, axis=-1, keepdims=True)
    i2 = jnp.min(jnp.where(le == v2, lane, big), axis=-1, keepdims=True)
    e2 = jnp.exp(v2 - v1)
    den = 1.0 + e2
    gate_ref[...] = jnp.where(lane == 0, p_grp * (1.0 / den), jnp.where(lane == 1, p_grp * (e2 / den), 0.0))
    eid_ref[...] = jnp.where(lane == 0, i1 - N_GROUPS, jnp.where(lane == 1, i2 - N_GROUPS, 0))


def _merge(x, yc, yr, os_, ls_, gates, wc, wr, wa, wo, g, b, rt, *, alpha, tm, row0, n_total, prev):
    n = x.shape[0]
    row = lambda c: pl.BlockSpec((tm, c), lambda i: (i, 0))
    full = lambda a: pl.BlockSpec(a.shape, lambda i: (0,) * a.ndim)
    b0 = row0 // tm
    in_specs = ([row(D_MODEL), row(CONV_DIM), row(RWKV_C)] + [row(ATT_W)] * 6 + [row(N_BRANCH * D_MODEL)]
                + [full(a) for a in (wc, wr, wa, wo, g, b, rt)])
    args = [x, yc, yr, *os_, *ls_, gates, wc, wr, wa, wo, g, b, rt]
    aliases = {}
    if prev is not None:
        aliases = {len(args) + k: k for k in range(len(prev))}
        in_specs += [pl.BlockSpec(memory_space=pl.ANY)] * len(prev)
        args += list(prev)
    return pl.pallas_call(
        functools.partial(_merge_kernel, alpha=alpha),
        grid=(n // tm,),
        in_specs=in_specs,
        out_specs=[pl.BlockSpec((tm * ROW_TILE, LANES), lambda i: (b0 + i, 0)),
                   pl.BlockSpec((tm, ROUTER_PAD), lambda i: (b0 + i, 0)),
                   pl.BlockSpec((tm, ROUTER_PAD), lambda i: (b0 + i, 0))],
        out_shape=[jax.ShapeDtypeStruct((n_total * ROW_TILE, LANES), F32),
                   jax.ShapeDtypeStruct((n_total, ROUTER_PAD), F32),
                   jax.ShapeDtypeStruct((n_total, ROUTER_PAD), jnp.int32)],
        input_output_aliases=aliases,
        compiler_params=_cparams("parallel"),
        name="merge",
    )(*args)


assert TOP_K & (TOP_K - 1) == 0
ROW_TILE = D_MODEL // LANES


def _expert_kernel(be_ref, off_ref, cnt_ref, order_ref, h_hbm, wg_ref, wu_ref, wd_ref, out_hbm,
                   xbuf, ybuf, gsem, ssem):
    i = pl.program_id(0)
    last = pl.num_programs(0) - 1
    slot = i & 1
    other = 1 - slot
    n_out = out_hbm.shape[0] // ROW_TILE - 2 * MOE_BLOCK
    block_rows = MOE_BLOCK * ROW_TILE

    def tile(row):
        return pl.ds(pl.multiple_of(row * ROW_TILE, ROW_TILE), ROW_TILE)

    def chunk(c):
        return pl.ds(c, MOE_BLOCK, stride=ROW_TILE)

    def token_of(assignment):
        return lax.shift_right_logical(assignment, TOP_K.bit_length() - 1)

    def gather_wait(s):
        pltpu.make_async_copy(h_hbm.at[pl.ds(0, block_rows)], xbuf.at[s], gsem.at[s]).wait()

    def scatter_wait(s):
        pltpu.make_async_copy(ybuf.at[s], out_hbm.at[pl.ds(0, block_rows)], ssem.at[s]).wait()

    active = cnt_ref[i] > 0
    prev = jnp.maximum(i - 1, 0)
    prev_active = (i > 0) & (cnt_ref[prev] > 0)

    @pl.when(i == 0)
    def _():
        ybuf[...] = jnp.zeros(ybuf.shape, F32)
        pltpu.make_async_copy(ybuf.at[0], out_hbm.at[pl.ds(n_out * ROW_TILE, block_rows)], ssem.at[0]).start()
        base = off_ref[0]
        for r in range(MOE_BLOCK):
            tok = token_of(order_ref[base + r])
            pltpu.make_async_copy(h_hbm.at[tile(tok)], xbuf.at[0, tile(r)], gsem.at[0]).start()

    @pl.when(active | prev_active)
    def _():
        gather_wait(slot)
        scatter_wait(slot)
        nxt_base = off_ref[jnp.minimum(i + 1, last)]
        prev_base = off_ref[prev]
        prev_cnt = jnp.where(i > 0, cnt_ref[prev], 0)
        for r in range(MOE_BLOCK):
            tok = token_of(order_ref[nxt_base + r])
            pltpu.make_async_copy(h_hbm.at[tile(tok)], xbuf.at[other, tile(r)], gsem.at[other]).start()
            dst = jnp.where(r < prev_cnt, order_ref[prev_base + r], n_out + other * MOE_BLOCK + r)
            pltpu.make_async_copy(ybuf.at[other, tile(r)], out_hbm.at[tile(dst)], ssem.at[other]).start(priority=1)
        xb = jnp.concatenate([xbuf[slot, chunk(c), :].astype(BF16) for c in range(ROW_TILE)], axis=1)
        gate = _dot(xb, wg_ref[...])
        act = (gate * _sigmoid(gate) * _dot(xb, wu_ref[...])).astype(BF16)
        y = _dot(act, wd_ref[...])
        for c in range(ROW_TILE):
            ybuf[slot, chunk(c), :] = y[:, c * LANES:(c + 1) * LANES]

    @pl.when(prev_active & jnp.logical_not(active))
    def _():
        gather_wait(other)
        scatter_wait(other)


def _experts(ht, block_e, block_off, block_cnt, order, wg, wu, wd, *, layer):
    n = ht.shape[0] // ROW_TILE
    n_blocks = block_e.shape[0]
    wmap = lambda i, be, off, cnt, order: (layer, be[i], 0, 0)
    grid_spec = pltpu.PrefetchScalarGridSpec(
        num_scalar_prefetch=4,
        grid=(n_blocks,),
        in_specs=[pl.BlockSpec(memory_space=pl.ANY),
                  pl.BlockSpec((None, None, D_MODEL, D_EXPERT), wmap),
                  pl.BlockSpec((None, None, D_MODEL, D_EXPERT), wmap),
                  pl.BlockSpec((None, None, D_EXPERT, D_MODEL), wmap)],
        out_specs=pl.BlockSpec(memory_space=pl.ANY),
        scratch_shapes=[pltpu.VMEM((2, MOE_BLOCK * ROW_TILE, LANES), F32),
                        pltpu.VMEM((2, MOE_BLOCK * ROW_TILE, LANES), F32),
                        pltpu.SemaphoreType.DMA((2,)), pltpu.SemaphoreType.DMA((2,))],
    )
    return pl.pallas_call(
        _expert_kernel,
        grid_spec=grid_spec,
        out_shape=jax.ShapeDtypeStruct(((TOP_K * n + 2 * MOE_BLOCK) * ROW_TILE, LANES), F32),
        compiler_params=_cparams("arbitrary"),
        name="experts",
    )(block_e, block_off, block_cnt, order, ht, wg, wu, wd)


def _route(eid):
    n = eid.shape[0]
    a = n * TOP_K
    n_blocks = -(-a // MOE_BLOCK) + N_EXPERTS
    skey = jnp.sort(eid.reshape(-1) * a + jnp.arange(a, dtype=jnp.int32))
    order = skey % a
    experts = jnp.arange(N_EXPERTS, dtype=jnp.int32)
    start = jnp.sum(skey[None, :] < (experts * a)[:, None], axis=1, dtype=jnp.int32)
    counts = jnp.sum(skey[None, :] < ((experts + 1) * a)[:, None], axis=1, dtype=jnp.int32) - start
    padded = (counts + MOE_BLOCK - 1) // MOE_BLOCK * MOE_BLOCK
    pend = jnp.cumsum(padded)
    pstart = pend - padded
    blk_start = jnp.arange(n_blocks, dtype=jnp.int32) * MOE_BLOCK
    block_e = jnp.minimum(jnp.sum(pend[None, :] <= blk_start[:, None], axis=1, dtype=jnp.int32), N_EXPERTS - 1)
    pick = (block_e[:, None] == experts[None, :]).astype(jnp.int32)
    within = blk_start - jnp.sum(pick * pstart[None, :], axis=1)
    block_cnt = jnp.where(blk_start < pend[-1],
                          jnp.clip(jnp.sum(pick * counts[None, :], axis=1) - within, 0, MOE_BLOCK), 0)
    block_off = jnp.where(block_cnt > 0, jnp.sum(pick * start[None, :], axis=1) + within, 0)
    order = jnp.concatenate([order, jnp.zeros((MOE_BLOCK,), jnp.int32)])
    return block_e, block_off.astype(jnp.int32), block_cnt.astype(jnp.int32), order


def _ln2_kernel(h_ref, f_ref, gate_ref, g_ref, b_ref, y_ref, yb_ref, *, alpha):
    tm = y_ref.shape[0]
    g0, g1 = gate_ref[:, 0:1], gate_ref[:, 1:2]
    pre = []
    for c in range(ROW_TILE):
        f0 = f_ref[pl.ds(c, tm, stride=TOP_K * ROW_TILE), :]
        f1 = f_ref[pl.ds(ROW_TILE + c, tm, stride=TOP_K * ROW_TILE), :]
        pre.append(alpha * h_ref[pl.ds(c, tm, stride=ROW_TILE), :] + (f0 * g0 + f1 * g1))
    mu = sum(jnp.sum(p, axis=-1, keepdims=True) for p in pre) * (1.0 / D_MODEL)
    cen = [p - mu for p in pre]
    var = sum(jnp.sum(p * p, axis=-1, keepdims=True) for p in cen) * (1.0 / D_MODEL)
    rstd = lax.rsqrt(var + LN_EPS)
    for c in range(ROW_TILE):
        sl = slice(c * LANES, (c + 1) * LANES)
        y = cen[c] * rstd * g_ref[:, sl] + b_ref[:, sl]
        y_ref[:, sl] = y
        yb_ref[:, sl] = y.astype(BF16)


def _ln2(ht, ft, gate, g, b, *, alpha, tm, row0, n):
    vec = pl.BlockSpec((1, D_MODEL), lambda i: (0, 0))
    b0 = row0 // tm
    return pl.pallas_call(
        functools.partial(_ln2_kernel, alpha=alpha),
        grid=(n // tm,),
        in_specs=[pl.BlockSpec((tm * ROW_TILE, LANES), lambda i: (b0 + i, 0)),
                  pl.BlockSpec((tm * TOP_K * ROW_TILE, LANES), lambda i: (b0 + i, 0)),
                  pl.BlockSpec((tm, ROUTER_PAD), lambda i: (b0 + i, 0)), vec, vec],
        out_specs=[pl.BlockSpec((tm, D_MODEL), lambda i: (i, 0))] * 2,
        out_shape=[jax.ShapeDtypeStruct((n, D_MODEL), F32), jax.ShapeDtypeStruct((n, D_MODEL), BF16)],
        compiler_params=_cparams("parallel"),
        name="ln2",
    )(ht, ft, gate, g, b)


def _row(v):
    return v.reshape(1, -1)


def _mixers(x, xb, p, conv_state, shift, wkv, caches, *, alpha, prompt, layer, row0, n_total, moe_prev):
    bsz, t, d = x.shape
    n = bsz * t
    xb2 = xb.reshape(n, d)
    o1 = 2 * CONV_DIM
    o2 = o1 + RWKV_U
    o3 = o2 + ATT_QKV
    tm_in = min(n, 1024)
    z = _proj(xb2, p["w_in"], p["b_in"], layer=layer, col0=0, n_cols=o1, epilogue="glu", tn=D_MODEL, tm=tm_in)
    gates = _proj(xb2, p["w_in"], p["b_in"], layer=layer, col0=o3, n_cols=N_BRANCH * D_MODEL,
                  epilogue="sigmoid_bias", tn=D_MODEL, tm=tm_in)
    ur = _proj(xb2, p["w_rwkv"], p["b_in"], layer=layer, col0=0, n_cols=RWKV_U, epilogue="none",
               tn=RWKV_U, tm=tm_in)
    ua = _proj(xb2, p["w_att"], p["b_in"], layer=layer, col0=0, n_cols=ATT_QKV, epilogue="none",
               tn=ATT_QKV // 2, tm=tm_in)
    z = z.reshape(bsz, t, CONV_DIM)
    ur = ur.reshape(bsz, t, RWKV_U)
    ua = ua.reshape(bsz, t, ATT_QKV)

    hist = jnp.pad(conv_state, ((0, 0), (CONV_HIST - (CONV_K - 1), 0), (0, 0)))
    yc = _conv(z, hist, p["conv_dw"], p["conv_dw_b"], p["conv_ln_g"], p["conv_ln_b"], tt=256 if prompt else t)
    conv_new = jnp.concatenate([conv_state, z], axis=1)[:, -(CONV_K - 1):]

    shift8 = jnp.pad(shift[:, None, :], ((0, 0), (SUBLANES - 1, 0), (0, 0)))
    pre = _rwkv_pre(ur, shift8, p["rwkv_mu"], p["rwkv_w0"], p["rwkv_w2p"], p["rwkv_a0"], p["rwkv_a2p"],
                    p["rwkv_g2"], p["rwkv_kk"], p["rwkv_ka"], p["rwkv_rk"], p["ones_bd"],
                    tt=256 if prompt else t)
    chunk = 64 if prompt else SUBLANES
    if t % chunk:
        pre = [jnp.pad(a, ((0, 0), (0, chunk - t % chunk), (0, 0))) for a in pre]
    yr, wkv_new = _scan(*pre, wkv, p["rwkv_ln_g"], p["rwkv_ln_b"], chunk=chunk)
    yr = yr[:, :t]
    shift_new = ur[:, -1]

    os_, ls_, att_new = [], [], []
    for g, (window, dil) in enumerate(DILATIONS):
        if prompt:
            o, lse = _attn_prompt(ua, p["att_bias"][g], g=g, dil=dil)
            rows = min(window, t)
            kv = [ua[:, t - rows:, (w * N_DIL + g) * ATT_W:(w * N_DIL + g + 1) * ATT_W] for w in (1, 2)]
            att_new.append(jnp.stack([a.reshape(bsz, rows, ATT_HG, HEAD_DIM) for a in kv], axis=1))
        else:
            cache_t, prev = caches[g]
            o, lse, new = _attn_sample(ua, cache_t, prev, *p["att_bias"][g], g=g, layer=layer)
            att_new.append(new)
        os_.append(o.reshape(n, ATT_W))
        ls_.append(lse.reshape(n, ATT_W))

    moe_in = _merge(x.reshape(n, d), yc.reshape(n, CONV_DIM), yr.reshape(n, RWKV_C), os_, ls_, gates,
                    p["conv_proj"], p["rwkv_proj"], p["attn_proj"], p["w_out"], p["ln1_g"], p["ln1_b"],
                    p["router"], alpha=alpha, tm=min(n, 256), row0=row0, n_total=n_total, prev=moe_prev)
    return moe_in, conv_new, shift_new, wkv_new, att_new


def kernel(x_prompt, x_sample, state_conv, state_shift, state_wkv, cache_attn_w128, cache_attn_w512, cache_attn_w2048, rel_bias, w_in, b_gate, conv_dw, conv_dw_b, conv_ln_g, conv_ln_b, conv_proj, rwkv_mu, rwkv_w0, rwkv_w2, rwkv_a0, rwkv_a2, rwkv_g2, rwkv_kk, rwkv_ka, rwkv_rk, rwkv_ln_g, rwkv_ln_b, rwkv_proj, attn_proj, w_out, ln1_g, ln1_b, router_group, router_expert, moe_w_gate, moe_w_up, moe_w_down, ln2_g, ln2_b):
    depth = w_in.shape[0]
    alpha = (2 * depth) ** 0.25
    bp, tp, _ = x_prompt.shape
    ts = x_sample.shape[1]
    n_p, n_s = bp * tp, x_sample.shape[0] * ts
    caches_t = [jnp.transpose(c, (0, 1, 2, 4, 5, 3)) for c in (cache_attn_w128, cache_attn_w512, cache_attn_w2048)]
    new_caches = [None] * N_DIL
    head = np.arange(RWKV_C) // RWKV_N
    ones_bd = jnp.asarray(head[:, None] == head[None, :], BF16)
    tabs = [rel_bias[:, g * ATT_HG:(g + 1) * ATT_HG] for g in range(N_DIL)]
    bias_p = [_prompt_bias(tabs[g], dil) for g, (_, dil) in enumerate(DILATIONS)]
    bias_s = [_sample_bias(tabs[g], dil, caches_t[g].shape[-1], ts) for g, (_, dil) in enumerate(DILATIONS)]

    w_in_b = w_in.astype(BF16)
    o_rwkv = 2 * CONV_DIM
    o_att = o_rwkv + RWKV_U
    w_rwkv_b = w_in_b[:, :, o_rwkv:o_att]
    w_att_b = w_in_b[:, :, o_att:o_att + ATT_QKV]
    b_in =jnp.pad(b_gate, ((0, 0), (w_in.shape[2] - b_gate.shape[1], 0)))[:, None, :]
    moe_b = [w.astype(BF16) for w in (moe_w_gate, moe_w_up, moe_w_down)]

    xp, xs = x_prompt, x_sample
    xpb, xsb = xp.astype(BF16), xs.astype(BF16)
    outs_p, outs_s = [], []
    for l in range(depth):
        zeros_lora = jnp.zeros((LORA_W, RWKV_C), F32)
        router = jnp.concatenate(
            [router_group[l], router_expert[l],
             jnp.zeros((D_MODEL, ROUTER_PAD - N_GROUPS - N_EXPERTS), F32)], axis=1)
        p = dict(
            w_in=w_in_b, b_in=b_in, w_rwkv=w_rwkv_b, w_att=w_att_b,
            conv_dw=conv_dw[l], conv_dw_b=_row(conv_dw_b[l]), conv_ln_g=_row(conv_ln_g[l]),
            conv_ln_b=_row(conv_ln_b[l]), conv_proj=conv_proj[l].astype(BF16),
            rwkv_mu=_row(rwkv_mu[l]), rwkv_w0=_row(rwkv_w0[l]),
            rwkv_w2p=jnp.concatenate([rwkv_w2[l], zeros_lora], axis=0).astype(BF16),
            rwkv_a0=_row(rwkv_a0[l]),
            rwkv_a2p=jnp.concatenate([zeros_lora, rwkv_a2[l]], axis=0).astype(BF16),
            rwkv_g2=rwkv_g2[l].astype(BF16), rwkv_kk=_row(rwkv_kk[l]), rwkv_ka=_row(rwkv_ka[l]),
            rwkv_rk=_row(rwkv_rk[l]), rwkv_ln_g=_row(rwkv_ln_g[l]), rwkv_ln_b=_row(rwkv_ln_b[l]),
            rwkv_proj=rwkv_proj[l].astype(BF16), attn_proj=attn_proj[l].astype(BF16),
            w_out=w_out[l].astype(BF16), ln1_g=_row(ln1_g[l]), ln1_b=_row(ln1_b[l]), router=router,
            ln2_g=_row(ln2_g[l]), ln2_b=_row(ln2_b[l]),
            ones_bd=ones_bd)
        moe_in, *new_p = _mixers(
            xp, xpb, dict(p, att_bias=bias_p),
            jnp.zeros((bp, CONV_K - 1, CONV_DIM), F32), jnp.zeros((bp, RWKV_U), F32),
            jnp.zeros((bp, RWKV_H, RWKV_N, RWKV_N), F32), None, alpha=alpha, prompt=True, layer=l,
            row0=0, n_total=n_p + n_s, moe_prev=None)
        moe_in, *new_s = _mixers(
            xs, xsb, dict(p, att_bias=bias_s), state_conv[l], state_shift[l], state_wkv[l],
            list(zip(caches_t, new_caches)), alpha=alpha, prompt=False, layer=l,
            row0=n_p, n_total=n_p + n_s, moe_prev=moe_in)
        ht, gate, eid = moe_in
        ft = _experts(ht, *_route(eid[:, :TOP_K]), *moe_b, layer=l)
        xp, xpb = _ln2(ht, ft, gate, p["ln2_g"], p["ln2_b"], alpha=alpha, tm=512, row0=0, n=n_p)
        xs, xsb = _ln2(ht, ft, gate, p["ln2_g"], p["ln2_b"], alpha=alpha, tm=n_s, row0=n_p, n=n_s)
        xp, xpb = xp.reshape(x_prompt.shape), xpb.reshape(x_prompt.shape)
        xs, xsb = xs.reshape(x_sample.shape), xsb.reshape(x_sample.shape)
        new_caches = new_s[3]
        outs_p.append(new_p)
        outs_s.append(new_s)

    def stack(outs, i):
        return jnp.stack([o[i] for o in outs])

    att_p = [jnp.stack([o[3][g] for o in outs_p]) for g in range(N_DIL)]
    att_s = [jnp.transpose(c, (0, 1, 2, 5, 3, 4)) for c in new_caches]
    return (xp, xs, stack(outs_p, 0), stack(outs_s, 0), stack(outs_p, 1), stack(outs_s, 1),
            stack(outs_p, 2), stack(outs_s, 2),
            att_p[0], att_s[0], att_p[1], att_s[1], att_p[2], att_s[2])
```

```python
import functools
import math

import jax
import jax.numpy as jnp
import numpy as np
from jax import lax
from jax.experimental import pallas as pl
from jax.experimental.pallas import tpu as pltpu

F32 = jnp.float32
BF16 = jnp.bfloat16

D_MODEL = 1024
CONV_DIM = D_MODEL // 2
CONV_K = 31
RWKV_N = 64
RWKV_H = D_MODEL // 128
RWKV_C = RWKV_H * RWKV_N
LORA_W, LORA_A, LORA_G = 64, 64, 128
RWKV_U = 3 * RWKV_C + LORA_W + LORA_A + LORA_G
RWKV_LN_EPS = 64e-5
HEAD_DIM = 64
ATT_HG = 4
ATT_W = ATT_HG * HEAD_DIM
DILATIONS = ((128, 1), (512, 4), (2048, 16))
N_DIL = len(DILATIONS)
ATT_QKV = 3 * N_DIL * ATT_W
N_BUCKETS = 32
MAX_DISTANCE = 2048
N_BRANCH = 3
N_GROUPS = 4
EXPERTS_PER_GROUP = 8
N_EXPERTS = N_GROUPS * EXPERTS_PER_GROUP
TOP_K = 2
D_EXPERT = D_MODEL // 2
MOE_BLOCK = 128
LN_EPS = 1e-5
NEG_INF = -1e30
ROUTER_PAD = 128

LANES = 128
SUBLANES = 8
VMEM_LIMIT = 48 * 1024 * 1024

NN = (((1,), (0,)), ((), ()))
NT = (((1,), (1,)), ((), ()))
TN = (((0,), (0,)), ((), ()))


def _cparams(*sem):
    return pltpu.CompilerParams(dimension_semantics=sem, vmem_limit_bytes=VMEM_LIMIT)


def _dot(a, b, dims=NN):
    return lax.dot_general(a, b, dims, preferred_element_type=F32)


def _pieces(x, n):
    if x.dtype == BF16:
        return [x]
    out, r = [], x
    for i in range(n):
        p = r.astype(BF16)
        out.append(p)
        if i + 1 < n:
            r = r - p.astype(F32)
    return out


def _mm(a, b, dims=NN, pa=1, pb=1):
    ap, bp = _pieces(a, pa), _pieces(b, pb)
    order = max(len(ap), len(bp))
    acc = None
    for i in reversed(range(len(ap))):
        for j in reversed(range(len(bp))):
            if i + j < order:
                t = _dot(ap[i], bp[j], dims)
                acc = t if acc is None else acc + t
    return acc


def _sigmoid(x):
    return 1.0 / (1.0 + jnp.exp(-x))


def _layer_norm(x, g, b, eps):
    mu = jnp.mean(x, axis=-1, keepdims=True)
    xc = x - mu
    var = jnp.mean(xc * xc, axis=-1, keepdims=True)
    return xc * lax.rsqrt(var + eps) * g + b


def _proj_kernel(x_ref, w_ref, b_ref, o_ref, *, epilogue):
    acc = _dot(x_ref[...], w_ref[...])
    if epilogue == "glu":
        half = acc.shape[1] // 2
        o_ref[...] = acc[:, :half] * _sigmoid(acc[:, half:])
    elif epilogue == "sigmoid_bias":
        o_ref[...] = _sigmoid(acc + b_ref[...])
    else:
        o_ref[...] = acc


def _proj(x, w_all, b_all, *, layer, col0, n_cols, epilogue, tn, tm):
    n, k = x.shape
    out_cols = n_cols // 2 if epilogue == "glu" else n_cols
    out_tn = tn // 2 if epilogue == "glu" else tn
    c0 = col0 // tn
    mt, nt = n // tm, n_cols // tn
    rows_outer = mt * n_cols + n <= nt * n + n_cols
    grid = (mt, nt) if rows_outer else (nt, mt)
    ij = (lambda a, b: (a, b)) if rows_outer else (lambda a, b: (b, a))
    return pl.pallas_call(
        functools.partial(_proj_kernel, epilogue=epilogue),
        grid=grid,
        in_specs=[pl.BlockSpec((tm, k), lambda a, b: (ij(a, b)[0], 0)),
                  pl.BlockSpec((None, k, tn), lambda a, b: (layer, 0, c0 + ij(a, b)[1])),
                  pl.BlockSpec((None, 1, tn), lambda a, b: (layer, 0, c0 + ij(a, b)[1]))],
        out_specs=pl.BlockSpec((tm, out_tn), lambda a, b: ij(a, b)),
        out_shape=jax.ShapeDtypeStruct((n, out_cols), F32),
        compiler_params=_cparams("parallel", "parallel"),
        name="in_proj_" + epilogue,
    )(x, w_all, b_all)


CONV_HIST = 32
CONV_ROWS = 64


def _conv_kernel(z_ref, hist_ref, dw_ref, dwb_ref, g_ref, b_ref, o_ref, zbuf):
    tt = z_ref.shape[1]
    win = CONV_HIST + tt

    @pl.when(pl.program_id(1) == 0)
    def _():
        zbuf[0, 0:CONV_HIST, :] = hist_ref[0]

    @pl.when(pl.program_id(1) != 0)
    def _():
        zbuf[0, 0:CONV_HIST, :] = zbuf[0, tt:tt + CONV_HIST, :]

    zbuf[0, CONV_HIST:win, :] = z_ref[0]
    if tt % SUBLANES == 0:
        for s in range(1, SUBLANES):
            zbuf[s, 0:win - SUBLANES, :] = zbuf[0, s:s + win - SUBLANES, :]
    pad = CONV_HIST - (CONV_K - 1)
    rb = min(tt, CONV_ROWS)
    for r0 in range(0, tt, rb):
        acc = jnp.zeros((rb, CONV_DIM), F32)
        for k in range(CONV_K):
            off = r0 + pad + k
            if tt % SUBLANES == 0:
                tap = zbuf[off % SUBLANES, off - off % SUBLANES:off - off % SUBLANES + rb, :]
            else:
                tap = zbuf[0, off:off + rb, :]
            acc = acc + tap * dw_ref[k:k + 1, :]
        y = _layer_norm(acc + dwb_ref[...], g_ref[...], b_ref[...], LN_EPS)
        o_ref[0, r0:r0 + rb, :] = (y * _sigmoid(y)).astype(BF16)


def _conv(z, hist, dw, dwb, g, b, *, tt):
    bsz, t, _ = z.shape
    vec = pl.BlockSpec((1, CONV_DIM), lambda i, j: (0, 0))
    return pl.pallas_call(
        _conv_kernel,
        grid=(bsz, t // tt),
        in_specs=[pl.BlockSpec((1, tt, CONV_DIM), lambda i, j: (i, j, 0)),
                  pl.BlockSpec((1, CONV_HIST, CONV_DIM), lambda i, j: (i, 0, 0)),
                  pl.BlockSpec((CONV_K, CONV_DIM), lambda i, j: (0, 0)),
                  vec, vec, vec],
        out_specs=pl.BlockSpec((1, tt, CONV_DIM), lambda i, j: (i, j, 0)),
        out_shape=jax.ShapeDtypeStruct((bsz, t, CONV_DIM), BF16),
        scratch_shapes=[pltpu.VMEM((SUBLANES, CONV_HIST + tt, CONV_DIM), F32)],
        compiler_params=_cparams("parallel", "arbitrary"),
        name="conv",
    )(z, hist, dw, dwb, g, b)


def _head_sum(x, ones_bd):
    return _mm(x, ones_bd, NN, pa=3, pb=1)


def _rwkv_pre_kernel(u_ref, shift_ref, mu_ref, w0_ref, w2_ref, a0_ref, a2_ref, g2_ref, kk_ref, ka_ref,
                     rk_ref, ones_ref,
                     r_ref, k_ref, v_ref, lw_ref, al_ref, be_ref, gate_ref, bonus_ref, ubuf):
    tt = u_ref.shape[1]
    c = RWKV_C

    @pl.when(pl.program_id(1) == 0)
    def _():
        ubuf[0:SUBLANES, :] = shift_ref[0]

    @pl.when(pl.program_id(1) != 0)
    def _():
        ubuf[0:SUBLANES, :] = ubuf[tt:tt + SUBLANES, :]

    u = u_ref[0]
    ubuf[SUBLANES:SUBLANES + tt, :] = u
    u_prev = ubuf[SUBLANES - 1:SUBLANES - 1 + tt, :]
    um = u + (u_prev - u) * mu_ref[...]
    r, k, v = um[:, 0:c], um[:, c:2 * c], um[:, 2 * c:3 * c]
    lo = um[:, 3 * c:3 * c + LORA_W + LORA_A]
    lane = lax.broadcasted_iota(jnp.int32, lo.shape, 1)
    lo = jnp.where(lane < LORA_W, jnp.tanh(lo), lo).astype(BF16)
    g_lo = _sigmoid(um[:, 3 * c + LORA_W + LORA_A:]).astype(BF16)
    xw = -(w0_ref[...] + _dot(lo, w2_ref[...]))
    softplus = jnp.maximum(xw, 0.0) + jnp.log(1.0 + jnp.exp(-jnp.abs(xw)))
    lw_ref[0] = -jnp.exp(-softplus - 0.5)
    a = _sigmoid(a0_ref[...] + _dot(lo, a2_ref[...]))
    gate_ref[0] = _dot(g_lo, g2_ref[...])
    ones_bd = ones_ref[...]
    kk = k * kk_ref[...]
    norm = jnp.sqrt(_head_sum(kk * kk, ones_bd))
    kk = kk / jnp.maximum(norm, 1e-12)
    k = k * (1.0 + (a - 1.0) * ka_ref[...])
    r_ref[0] = r
    k_ref[0] = k
    v_ref[0] = v
    al_ref[0] = -kk
    be_ref[0] = kk * a
    bonus_ref[0] = _head_sum(r * k * rk_ref[...], ones_bd) * v


def _rwkv_pre(u, shift8, mu, w0, w2p, a0, a2p, g2, kkp, ka, rk, ones_bd, *, tt):
    bsz, t, _ = u.shape
    full = lambda a: pl.BlockSpec(a.shape, lambda i, j: (0,) * a.ndim)
    seq = pl.BlockSpec((1, tt, RWKV_C), lambda i, j: (i, j, 0))
    return pl.pallas_call(
        _rwkv_pre_kernel,
        grid=(bsz, t // tt),
        in_specs=[pl.BlockSpec((1, tt, RWKV_U), lambda i, j: (i, j, 0)),
                  pl.BlockSpec((1, SUBLANES, RWKV_U), lambda i, j: (i, 0, 0)),
                  full(mu), full(w0), full(w2p), full(a0), full(a2p), full(g2), full(kkp), full(ka),
                  full(rk), full(ones_bd)],
        out_specs=[seq] * 8,
        out_shape=[jax.ShapeDtypeStruct((bsz, t, RWKV_C), F32)] * 8,
        scratch_shapes=[pltpu.VMEM((SUBLANES + tt, RWKV_U), F32)],
        compiler_params=_cparams("parallel", "arbitrary"),
        name="rwkv_pre",
    )(u, shift8, mu, w0, w2p, a0, a2p, g2, kkp, ka, rk, ones_bd)


def _scan_kernel(r_ref, k_ref, v_ref, lw_ref, al_ref, be_ref, gate_ref, bonus_ref, s0_ref, g_ref, b_ref,
                 y_ref, s_ref, state):
    nbt, c = r_ref.shape[0], r_ref.shape[1]
    n = RWKV_N

    @pl.when(pl.program_id(1) == 0)
    def _():
        state[...] = s0_ref[...]

    row = lax.broadcasted_iota(jnp.int32, (c, c), 0)
    col = lax.broadcasted_iota(jnp.int32, (c, c), 1)
    lower = (row >= col).astype(BF16)
    eye = (row == col).astype(F32)
    row2 = lax.broadcasted_iota(jnp.int32, (2 * c, 2 * c), 0)
    col2 = lax.broadcasted_iota(jnp.int32, (2 * c, 2 * c), 1) & (c - 1)
    mask2 = col2 < jnp.where(row2 < c, row2, row2 - c + 1)
    zeros_v = jnp.zeros((c, n), F32)

    units = [(bi, h) for bi in range(nbt) for h in range(RWKV_H)]
    sl = lambda h: slice(h * n, (h + 1) * n)
    wide = []
    for bi in range(nbt):
        lw = lw_ref[bi]
        cum = _mm(lower, lw, NN, pa=1, pb=3)
        e_p = jnp.exp(cum)
        e_m = jnp.exp(-cum)
        cum_end = cum[c - 1:c, :]
        e_end = jnp.exp(cum_end - cum)
        wide.append(dict(
            rq=r_ref[bi] * e_p, aq=al_ref[bi] * jnp.exp(cum - lw), kd=k_ref[bi] * e_m, bd=be_ref[bi] * e_m,
            kend=k_ref[bi] * e_end, bend=be_ref[bi] * e_end, v=v_ref[bi], p_end=jnp.exp(cum_end)))
    xq = [jnp.concatenate([wide[bi]["aq"][:, sl(h)], wide[bi]["rq"][:, sl(h)]], axis=0) for bi, h in units]
    wd = [jnp.concatenate([wide[bi]["bd"][:, sl(h)], wide[bi]["kd"][:, sl(h)]], axis=0) for bi, h in units]
    zv = [jnp.concatenate([zeros_v, wide[bi]["v"][:, sl(h)]], axis=0) for bi, h in units]
    s_old = [state[bi, h] for bi, h in units]
    idx = range(len(units))
    gm = [jnp.where(mask2, _mm(xq[u], wd[u], NT), 0.0) for u in idx]
    xs = [_mm(xq[u], s_old[u], NT) for u in idx]
    gv = [_mm(gm[u], zv[u], NN) for u in idx]
    pw = [gm[u][:c, :c] for u in idx]
    inv = [eye + a for a in pw]
    span = 2
    while span < c:
        pw = [_mm(a, a, NN) for a in pw]
        inv = [i + _mm(i, a, NN) for i, a in zip(inv, pw)]
        span *= 2
    us = [_mm(inv[u], xs[u][:c] + gv[u][:c], NN) for u in idx]
    ys = [xs[u][c:] + gv[u][c:] + _mm(gm[u][c:, :c], us[u], NN) for u in idx]
    for u, (bi, h) in enumerate(units):
        uv = jnp.concatenate([us[u], wide[bi]["v"][:, sl(h)]], axis=0)
        ends = jnp.concatenate([wide[bi]["bend"][:, sl(h)], wide[bi]["kend"][:, sl(h)]], axis=0)
        state[bi, h] = s_old[u] * wide[bi]["p_end"][:, sl(h)] + _mm(uv, ends, TN)
    for u, (bi, h) in enumerate(units):
        y = ys[u]
        mu = jnp.mean(y, axis=-1, keepdims=True)
        yc = y - mu
        var = jnp.mean(yc * yc, axis=-1, keepdims=True)
        y = yc * lax.rsqrt(var + RWKV_LN_EPS) * g_ref[:, sl(h)] + b_ref[:, sl(h)]
        y_ref[bi, :, sl(h)] = ((y + bonus_ref[bi, :, sl(h)]) * gate_ref[bi, :, sl(h)]).astype(BF16)

    @pl.when(pl.program_id(1) == pl.num_programs(1) - 1)
    def _():
        s_ref[...] = state[...]


SCAN_ROWS = 4


def _scan(r, k, v, lw, al, be, gate, bonus, s0, g, b, *, chunk):
    bsz, t, _ = r.shape
    seq = pl.BlockSpec((SCAN_ROWS, chunk, RWKV_C), lambda i, j: (i, j, 0))
    st = pl.BlockSpec((SCAN_ROWS, RWKV_H, RWKV_N, RWKV_N), lambda i, j: (i, 0, 0, 0))
    vec = pl.BlockSpec((1, RWKV_C), lambda i, j: (0, 0))
    return pl.pallas_call(
        _scan_kernel,
        grid=(bsz // SCAN_ROWS, t // chunk),
        in_specs=[seq] * 8 + [st, vec, vec],
        out_specs=[seq, st],
        out_shape=[jax.ShapeDtypeStruct((bsz, t, RWKV_C), BF16),
                   jax.ShapeDtypeStruct((bsz, RWKV_H, RWKV_N, RWKV_N), F32)],
        scratch_shapes=[pltpu.VMEM((SCAN_ROWS, RWKV_H, RWKV_N, RWKV_N), F32)],
        compiler_params=_cparams("parallel", "arbitrary"),
        name="rwkv_scan",
    )(r, k, v, lw, al, be, gate, bonus, s0, g, b)


ATT_SPAN = 128


HEADS_PER_TILE = LANES // HEAD_DIM


ATT_UNITS = 8


def _attn_prompt_kernel(q_ref, kc_ref, kp_ref, vc_ref, vp_ref, bias_ref, o_ref, lse_ref, *, dil, nq):
    scale = HEAD_DIM ** -0.5
    first = jnp.minimum(pl.program_id(2), 1)

    def rows_of(b, res):
        start = b * ATT_SPAN * dil + res
        return pl.ds(start, ATT_SPAN, stride=dil) if dil > 1 else pl.ds(start, ATT_SPAN)

    def run(units):
        loaded = []
        for b, res in units:
            cur = rows_of(b, res)
            q, kc, vc = (ref[0, cur, :] for ref in (q_ref, kc_ref, vc_ref))
            if b == 0:
                kp, vp = (ref[0, rows_of(0, res), :] for ref in (kp_ref, vp_ref))
            else:
                kp, vp = (ref[0, rows_of(b - 1, res), :] for ref in (kc_ref, vc_ref))
            loaded.append((cur, q, kc, kp, vc, vp, first if b == 0 else 1))
        heads = [(u, h) for u in range(len(units)) for h in range(HEADS_PER_TILE)]
        sls = [slice(h * HEAD_DIM, (h + 1) * HEAD_DIM) for h in range(HEADS_PER_TILE)]
        ss = []
        for u, h in heads:
            _, q, kc, kp, _, _, variant = loaded[u]
            qh = q[:, sls[h]].astype(BF16)
            s = jnp.concatenate([_dot(qh, kp[:, sls[h]].astype(BF16), NT),
                                 _dot(qh, kc[:, sls[h]].astype(BF16), NT)], axis=1)
            ss.append(s * scale + bias_ref[variant, h])
        ms = [jnp.max(s, axis=-1, keepdims=True) for s in ss]
        es = [jnp.exp(s - m) for s, m in zip(ss, ms)]
        dens = [jnp.sum(e, axis=-1, keepdims=True) for e in es]
        outs = []
        for (u, h), e, den in zip(heads, es, dens):
            _, _, _, _, vc, vp, _ = loaded[u]
            v2 = jnp.concatenate([vp[:, sls[h]], vc[:, sls[h]]], axis=0).astype(BF16)
            outs.append(_dot((e / den).astype(BF16), v2))
        for u in range(len(units)):
            cur = loaded[u][0]
            mine = [i for i, (uu, _) in enumerate(heads) if uu == u]
            o_ref[0, cur, :] = jnp.concatenate([outs[i] for i in mine], axis=1)
            lse_ref[0, cur, :] = jnp.concatenate(
                [jnp.broadcast_to(ms[i] + jnp.log(dens[i]), outs[i].shape) for i in mine], axis=1)

    if dil > ATT_UNITS:
        def group(gi, carry):
            run([(0, gi * ATT_UNITS + r) for r in range(ATT_UNITS)])
            return carry

        lax.fori_loop(0, dil // ATT_UNITS, group, 0)
    else:
        run([(b, r) for b in range(nq) for r in range(dil)])


def _attn_prompt(ua, bias, *, g, dil):
    bsz, s, _ = ua.shape
    nq = max(ATT_UNITS // dil, 1)
    prev_rows = ATT_SPAN * dil
    rows = nq * prev_rows
    tiles = ATT_W // LANES

    def col(which):
        return lambda b, hp, i: (b, i, (which * N_DIL + g) * tiles + hp)

    def col_prev(which):
        return lambda b, hp, i: (b, jnp.maximum(i * nq - 1, 0), (which * N_DIL + g) * tiles + hp)

    blk = (1, rows, LANES)
    prev = (1, prev_rows, LANES)
    return pl.pallas_call(
        functools.partial(_attn_prompt_kernel, dil=dil, nq=nq),
        grid=(bsz, tiles, s // rows),
        in_specs=[pl.BlockSpec(blk, col(0)), pl.BlockSpec(blk, col(1)), pl.BlockSpec(prev, col_prev(1)),
                  pl.BlockSpec(blk, col(2)), pl.BlockSpec(prev, col_prev(2)),
                  pl.BlockSpec((2, HEADS_PER_TILE, ATT_SPAN, 2 * ATT_SPAN), lambda b, hp, i: (0, hp, 0, 0))],
        out_specs=[pl.BlockSpec(blk, lambda b, hp, i: (b, i, hp))] * 2,
        out_shape=[jax.ShapeDtypeStruct((bsz, s, ATT_W), F32)] * 2,
        compiler_params=_cparams("parallel", "parallel", "parallel"),
        name="attn_prompt",
    )(ua, ua, ua, ua, ua, bias)


CACHE_SLOTS = 3


def _attn_sample_kernel(u_ref, cache_ref, bo_ref, bn_ref, *rest, g, layer):
    o_ref, lse_ref, new_ref, cbuf, csem = rest[-5:]
    t = u_ref.shape[1]
    rows = cache_ref.shape[-1]
    b = pl.program_id(0)
    nb = pl.num_programs(0)
    slot = b % CACHE_SLOTS

    def fetch(step):
        s = step % CACHE_SLOTS
        return pltpu.make_async_copy(cache_ref.at[layer, step], cbuf.at[s], csem.at[s])

    @pl.when(b == 0)
    def _():
        fetch(0).start()

        @pl.when(nb > 1)
        def _():
            fetch(1).start()

    @pl.when(b + 2 < nb)
    def _():
        fetch(b + 2).start()

    fetch(b).wait()
    scale = HEAD_DIM ** -0.5
    base = g * ATT_W
    lane = lax.broadcasted_iota(jnp.int32, (HEAD_DIM, LANES), 1)
    place = (lax.broadcasted_iota(jnp.int32, (t, LANES), 1)
             == lax.broadcasted_iota(jnp.int32, (t, LANES), 0) + (LANES - t)).astype(BF16)
    for h in range(ATT_HG):
        sl = slice(h * HEAD_DIM, (h + 1) * HEAD_DIM)
        col = base + h * HEAD_DIM
        q = u_ref[0, :, col:col + HEAD_DIM].astype(BF16)
        k_new = u_ref[0, :, N_DIL * ATT_W + col:N_DIL * ATT_W + col + HEAD_DIM]
        v_new = u_ref[0, :, 2 * N_DIL * ATT_W + col:2 * N_DIL * ATT_W + col + HEAD_DIM]
        k_t = cbuf[slot, 0, h]
        v_t = cbuf[slot, 1, h]
        s_old = _dot(q, k_t.astype(BF16)) * scale + bo_ref[h]
        s_new = _dot(q, k_new.astype(BF16), NT) * scale + bn_ref[h]
        m = jnp.maximum(jnp.max(s_old, axis=-1, keepdims=True), jnp.max(s_new, axis=-1, keepdims=True))
        e_old = jnp.exp(s_old - m)
        e_new = jnp.exp(s_new - m)
        den = jnp.sum(e_old, axis=-1, keepdims=True) + jnp.sum(e_new, axis=-1, keepdims=True)
        o = _dot((e_old / den).astype(BF16), v_t.astype(BF16), NT)
        p_new = e_new / den
        for j in range(t):
            o = o + p_new[:, j:j + 1] * v_new[j:j + 1, :]
        o_ref[0, :, sl] = o
        lse_ref[0, :, sl] = jnp.broadcast_to(m + jnp.log(den), o.shape)
        for kv, (old, new) in enumerate(((k_t, k_new), (v_t, v_new))):
            moved = pltpu.roll(old, rows - t, axis=1)
            tail = jnp.where(lane >= LANES - t, _mm(new, place, TN, 3, 1), moved[:, rows - LANES:])
            if rows > LANES:
                new_ref[0, 0, kv, h, :, 0:rows - LANES] = moved[:, 0:rows - LANES]
            new_ref[0, 0, kv, h, :, rows - LANES:rows] = tail


def _attn_sample(ua, cache_t, prev, bias_old, bias_new, *, g, layer):
    bsz, t, _ = ua.shape
    out = pl.BlockSpec((1, t, ATT_W), lambda b: (b, 0, 0))
    cb = pl.BlockSpec((1, 1) + cache_t.shape[2:], lambda b: (layer, b, 0, 0, 0, 0))
    full = lambda a: pl.BlockSpec(a.shape, lambda b: (0,) * a.ndim)
    in_specs = [pl.BlockSpec((1, t, ATT_QKV), lambda b: (b, 0, 0)), pl.BlockSpec(memory_space=pl.ANY),
                full(bias_old), full(bias_new)]
    args = [ua, cache_t, bias_old, bias_new]
    aliases = {}
    if prev is not None:
        in_specs.append(pl.BlockSpec(memory_space=pl.ANY))
        args.append(prev)
        aliases = {len(args) - 1: 2}
    return pl.pallas_call(
        functools.partial(_attn_sample_kernel, g=g, layer=layer),
        grid=(bsz,),
        in_specs=in_specs,
        out_specs=[out, out, cb],
        out_shape=[jax.ShapeDtypeStruct((bsz, t, ATT_W), F32)] * 2
                  + [jax.ShapeDtypeStruct(cache_t.shape, F32)],
        scratch_shapes=[pltpu.VMEM((CACHE_SLOTS,) + cache_t.shape[2:], F32),
                        pltpu.SemaphoreType.DMA((CACHE_SLOTS,))],
        input_output_aliases=aliases,
        compiler_params=_cparams("arbitrary"),
        name="attn_sample",
    )(*args)


def _rel_bucket(dist):
    max_exact = N_BUCKETS // 2
    large = max_exact + (jnp.log(jnp.maximum(dist, 1).astype(F32) / max_exact)
                         / math.log(MAX_DISTANCE / max_exact) * (N_BUCKETS - max_exact)).astype(jnp.int32)
    return jnp.where(dist < max_exact, dist, jnp.minimum(large, N_BUCKETS - 1))


def _bias_lookup(tab, dist):
    onehot = (_rel_bucket(jnp.asarray(dist))[..., None] == jnp.arange(N_BUCKETS)).astype(F32)
    return jnp.einsum("...b,bh->h...", onehot, tab, precision=lax.Precision.HIGHEST)


def _prompt_bias(tab, dil):
    span = ATT_SPAN
    qi = np.arange(span)[:, None]
    ki = np.arange(2 * span)[None, :]
    j = span + qi - ki
    band = (j >= 0) & (j <= span)
    bias = _bias_lookup(tab, dil * np.clip(j, 0, span))
    rest = jnp.where(band[None], bias, NEG_INF)
    first = jnp.where((band & (ki >= span))[None], bias, NEG_INF)
    return jnp.stack([first, rest])


def _sample_bias(tab, dil, rows, t):
    rho = np.arange(rows + t)[None, :]
    d = rows + np.arange(t)[:, None] - rho
    valid = (d >= 0) & (d % dil == 0) & (d // dil <= ATT_SPAN)
    bias = jnp.where(valid[None], _bias_lookup(tab, np.clip(d, 0, None)), NEG_INF)
    return bias[:, :, :rows], bias[:, :, rows:]


def _merge_kernel(x_ref, yc_ref, yr_ref, o0_ref, o1_ref, o2_ref, l0_ref, l1_ref, l2_ref, gates_ref,
                  wc_ref, wr_ref, wa_ref, wo_ref, g_ref, b_ref, rt_ref, *rest, alpha):
    h_ref, gate_ref, eid_ref = rest[-3:]
    l0, l1, l2 = l0_ref[...], l1_ref[...], l2_ref[...]
    m = jnp.maximum(jnp.maximum(l0, l1), l2)
    e0, e1, e2 = jnp.exp(l0 - m), jnp.exp(l1 - m), jnp.exp(l2 - m)
    den = e0 + e1 + e2
    o = o0_ref[...] * (e0 / den) + o1_ref[...] * (e1 / den) + o2_ref[...] * (e2 / den)
    d = D_MODEL
    merged = (gates_ref[:, 0:d] * _dot(yc_ref[...], wc_ref[...])
              + gates_ref[:, d:2 * d] * _dot(yr_ref[...], wr_ref[...])
              + gates_ref[:, 2 * d:3 * d] * _dot(o.astype(BF16), wa_ref[...]))
    pre = alpha * x_ref[...] + _dot(merged.astype(BF16), wo_ref[...])
    h = _layer_norm(pre, g_ref[...], b_ref[...], LN_EPS)
    for c in range(ROW_TILE):
        h_ref[pl.ds(c, h.shape[0], stride=ROW_TILE), :] = h[:, c * LANES:(c + 1) * LANES]
    logits = _mm(h, rt_ref[...], NN, 2, 2)
    lane = lax.broadcasted_iota(jnp.int32, logits.shape, 1)
    big = jnp.int32(ROUTER_PAD)
    is_grp = lane < N_GROUPS
    lg = jnp.where(is_grp, logits, NEG_INF)
    m_g = jnp.max(lg, axis=-1, keepdims=True)
    grp = jnp.min(jnp.where(lg == m_g, lane, big), axis=-1, keepdims=True)
    p_grp = 1.0 / jnp.sum(jnp.where(is_grp, jnp.exp(logits - m_g), 0.0), axis=-1, keepdims=True)
    lo_lane = N_GROUPS + grp * EXPERTS_PER_GROUP
    in_grp = (lane >= lo_lane) & (lane < lo_lane + EXPERTS_PER_GROUP)
    le = jnp.where(in_grp, logits, NEG_INF)
    v1 = jnp.max(le, axis=-1, keepdims=True)
    i1 = jnp.min(jnp.where(le == v1, lane, big), axis=-1, keepdims=True)
    le = jnp.where(lane == i1, NEG_INF, le)
    v2 = jnp.max(le, axis=-1, keepdims=True)
    i2 = jnp.min(jnp.where(le == v2, lane, big), axis=-1, keepdims=True)
    e2 = jnp.exp(v2 - v1)
    den = 1.0 + e2
    gate_ref[...] = jnp.where(lane == 0, p_grp * (1.0 / den), jnp.where(lane == 1, p_grp * (e2 / den), 0.0))
    eid_ref[...] = jnp.where(lane == 0, i1 - N_GROUPS, jnp.where(lane == 1, i2 - N_GROUPS, 0))


def _merge(x, yc, yr, os_, ls_, gates, wc, wr, wa, wo, g, b, rt, *, alpha, tm, row0, n_total, prev):
    n = x.shape[0]
    row = lambda c: pl.BlockSpec((tm, c), lambda i: (i, 0))
    full = lambda a: pl.BlockSpec(a.shape, lambda i: (0,) * a.ndim)
    b0 = row0 // tm
    in_specs = ([row(D_MODEL), row(CONV_DIM), row(RWKV_C)] + [row(ATT_W)] * 6 + [row(N_BRANCH * D_MODEL)]
                + [full(a) for a in (wc, wr, wa, wo, g, b, rt)])
    args = [x, yc, yr, *os_, *ls_, gates, wc, wr, wa, wo, g, b, rt]
    aliases = {}
    if prev is not None:
        aliases = {len(args) + k: k for k in range(len(prev))}
        in_specs += [pl.BlockSpec(memory_space=pl.ANY)] * len(prev)
        args += list(prev)
    return pl.pallas_call(
        functools.partial(_merge_kernel, alpha=alpha),
        grid=(n // tm,),
        in_specs=in_specs,
        out_specs=[pl.BlockSpec((tm * ROW_TILE, LANES), lambda i: (b0 + i, 0)),
                   pl.BlockSpec((tm, ROUTER_PAD), lambda i: (b0 + i, 0)),
                   pl.BlockSpec((tm, ROUTER_PAD), lambda i: (b0 + i, 0))],
        out_shape=[jax.ShapeDtypeStruct((n_total * ROW_TILE, LANES), F32),
                   jax.ShapeDtypeStruct((n_total, ROUTER_PAD), F32),
                   jax.ShapeDtypeStruct((n_total, ROUTER_PAD), jnp.int32)],
        input_output_aliases=aliases,
        compiler_params=_cparams("parallel"),
        name="merge",
    )(*args)


assert TOP_K & (TOP_K - 1) == 0
ROW_TILE = D_MODEL // LANES


def _expert_kernel(be_ref, off_ref, cnt_ref, order_ref, h_hbm, wg_ref, wu_ref, wd_ref, out_hbm,
                   xbuf, ybuf, gsem, ssem):
    i = pl.program_id(0)
    last = pl.num_programs(0) - 1
    slot = i & 1
    other = 1 - slot
    n_out = out_hbm.shape[0] // ROW_TILE - 2 * MOE_BLOCK
    block_rows = MOE_BLOCK * ROW_TILE

    def tile(row):
        return pl.ds(pl.multiple_of(row * ROW_TILE, ROW_TILE), ROW_TILE)

    def chunk(c):
        return pl.ds(c, MOE_BLOCK, stride=ROW_TILE)

    def token_of(assignment):
        return lax.shift_right_logical(assignment, TOP_K.bit_length() - 1)

    def gather_wait(s):
        pltpu.make_async_copy(h_hbm.at[pl.ds(0, block_rows)], xbuf.at[s], gsem.at[s]).wait()

    def scatter_wait(s):
        pltpu.make_async_copy(ybuf.at[s], out_hbm.at[pl.ds(0, block_rows)], ssem.at[s]).wait()

    active = cnt_ref[i] > 0
    prev = jnp.maximum(i - 1, 0)
    prev_active = (i > 0) & (cnt_ref[prev] > 0)

    @pl.when(i == 0)
    def _():
        ybuf[...] = jnp.zeros(ybuf.shape, F32)
        pltpu.make_async_copy(ybuf.at[0], out_hbm.at[pl.ds(n_out * ROW_TILE, block_rows)], ssem.at[0]).start()
        base = off_ref[0]
        for r in range(MOE_BLOCK):
            tok = token_of(order_ref[base + r])
            pltpu.make_async_copy(h_hbm.at[tile(tok)], xbuf.at[0, tile(r)], gsem.at[0]).start()

    @pl.when(active | prev_active)
    def _():
        gather_wait(slot)
        scatter_wait(slot)
        nxt_base = off_ref[jnp.minimum(i + 1, last)]
        prev_base = off_ref[prev]
        prev_cnt = jnp.where(i > 0, cnt_ref[prev], 0)
        for r in range(MOE_BLOCK):
            tok = token_of(order_ref[nxt_base + r])
            pltpu.make_async_copy(h_hbm.at[tile(tok)], xbuf.at[other, tile(r)], gsem.at[other]).start()
            dst = jnp.where(r < prev_cnt, order_ref[prev_base + r], n_out + other * MOE_BLOCK + r)
            pltpu.make_async_copy(ybuf.at[other, tile(r)], out_hbm.at[tile(dst)], ssem.at[other]).start()
        xb = jnp.concatenate([xbuf[slot, chunk(c), :].astype(BF16) for c in range(ROW_TILE)], axis=1)
        gate = _dot(xb, wg_ref[...])
        act = (gate * _sigmoid(gate) * _dot(xb, wu_ref[...])).astype(BF16)
        y = _dot(act, wd_ref[...])
        for c in range(ROW_TILE):
            ybuf[slot, chunk(c), :] = y[:, c * LANES:(c + 1) * LANES]

    @pl.when(prev_active & jnp.logical_not(active))
    def _():
        gather_wait(other)
        scatter_wait(other)


def _experts(ht, block_e, block_off, block_cnt, order, wg, wu, wd, *, layer):
    n = ht.shape[0] // ROW_TILE
    n_blocks = block_e.shape[0]
    wmap = lambda i, be, off, cnt, order: (layer, be[i], 0, 0)
    grid_spec = pltpu.PrefetchScalarGridSpec(
        num_scalar_prefetch=4,
        grid=(n_blocks,),
        in_specs=[pl.BlockSpec(memory_space=pl.ANY),
                  pl.BlockSpec((None, None, D_MODEL, D_EXPERT), wmap),
                  pl.BlockSpec((None, None, D_MODEL, D_EXPERT), wmap),
                  pl.BlockSpec((None, None, D_EXPERT, D_MODEL), wmap)],
        out_specs=pl.BlockSpec(memory_space=pl.ANY),
        scratch_shapes=[pltpu.VMEM((2, MOE_BLOCK * ROW_TILE, LANES), F32),
                        pltpu.VMEM((2, MOE_BLOCK * ROW_TILE, LANES), F32),
                        pltpu.SemaphoreType.DMA((2,)), pltpu.SemaphoreType.DMA((2,))],
    )
    return pl.pallas_call(
        _expert_kernel,
        grid_spec=grid_spec,
        out_shape=jax.ShapeDtypeStruct(((TOP_K * n + 2 * MOE_BLOCK) * ROW_TILE, LANES), F32),
        compiler_params=_cparams("arbitrary"),
        name="experts",
    )(block_e, block_off, block_cnt, order, ht, wg, wu, wd)


def _route(eid):
    n = eid.shape[0]
    a = n * TOP_K
    n_blocks = -(-a // MOE_BLOCK) + N_EXPERTS
    skey = jnp.sort(eid.reshape(-1) * a + jnp.arange(a, dtype=jnp.int32))
    order = skey % a
    experts = jnp.arange(N_EXPERTS, dtype=jnp.int32)
    start = jnp.sum(skey[None, :] < (experts * a)[:, None], axis=1, dtype=jnp.int32)
    counts = jnp.sum(skey[None, :] < ((experts + 1) * a)[:, None], axis=1, dtype=jnp.int32) - start
    padded = (counts + MOE_BLOCK - 1) // MOE_BLOCK * MOE_BLOCK
    pend = jnp.cumsum(padded)
    pstart = pend - padded
    blk_start = jnp.arange(n_blocks, dtype=jnp.int32) * MOE_BLOCK
    block_e = jnp.minimum(jnp.sum(pend[None, :] <= blk_start[:, None], axis=1, dtype=jnp.int32), N_EXPERTS - 1)
    pick = (block_e[:, None] == experts[None, :]).astype(jnp.int32)
    within = blk_start - jnp.sum(pick * pstart[None, :], axis=1)
    block_cnt = jnp.where(blk_start < pend[-1],
                          jnp.clip(jnp.sum(pick * counts[None, :], axis=1) - within, 0, MOE_BLOCK), 0)
    block_off = jnp.where(block_cnt > 0, jnp.sum(pick * start[None, :], axis=1) + within, 0)
    order = jnp.concatenate([order, jnp.zeros((MOE_BLOCK,), jnp.int32)])
    return block_e, block_off.astype(jnp.int32), block_cnt.astype(jnp.int32), order


def _ln2_kernel(h_ref, f_ref, gate_ref, g_ref, b_ref, y_ref, yb_ref, *, alpha):
    tm = y_ref.shape[0]
    g0, g1 = gate_ref[:, 0:1], gate_ref[:, 1:2]
    pre = []
    for c in range(ROW_TILE):
        f0 = f_ref[pl.ds(c, tm, stride=TOP_K * ROW_TILE), :]
        f1 = f_ref[pl.ds(ROW_TILE + c, tm, stride=TOP_K * ROW_TILE), :]
        pre.append(alpha * h_ref[pl.ds(c, tm, stride=ROW_TILE), :] + (f0 * g0 + f1 * g1))
    mu = sum(jnp.sum(p, axis=-1, keepdims=True) for p in pre) * (1.0 / D_MODEL)
    cen = [p - mu for p in pre]
    var = sum(jnp.sum(p * p, axis=-1, keepdims=True) for p in cen) * (1.0 / D_MODEL)
    rstd = lax.rsqrt(var + LN_EPS)
    for c in range(ROW_TILE):
        sl = slice(c * LANES, (c + 1) * LANES)
        y = cen[c] * rstd * g_ref[:, sl] + b_ref[:, sl]
        y_ref[:, sl] = y
        yb_ref[:, sl] = y.astype(BF16)


def _ln2(ht, ft, gate, g, b, *, alpha, tm, row0, n):
    vec = pl.BlockSpec((1, D_MODEL), lambda i: (0, 0))
    b0 = row0 // tm
    return pl.pallas_call(
        functools.partial(_ln2_kernel, alpha=alpha),
        grid=(n // tm,),
        in_specs=[pl.BlockSpec((tm * ROW_TILE, LANES), lambda i: (b0 + i, 0)),
                  pl.BlockSpec((tm * TOP_K * ROW_TILE, LANES), lambda i: (b0 + i, 0)),
                  pl.BlockSpec((tm, ROUTER_PAD), lambda i: (b0 + i, 0)), vec, vec],
        out_specs=[pl.BlockSpec((tm, D_MODEL), lambda i: (i, 0))] * 2,
        out_shape=[jax.ShapeDtypeStruct((n, D_MODEL), F32), jax.ShapeDtypeStruct((n, D_MODEL), BF16)],
        compiler_params=_cparams("parallel"),
        name="ln2",
    )(ht, ft, gate, g, b)


def _row(v):
    return v.reshape(1, -1)


def _mixers(x, xb, p, conv_state, shift, wkv, caches, *, alpha, prompt, layer, row0, n_total, moe_prev):
    bsz, t, d = x.shape
    n = bsz * t
    xb2 = xb.reshape(n, d)
    o1 = 2 * CONV_DIM
    o2 = o1 + RWKV_U
    o3 = o2 + ATT_QKV
    tm_in = min(n, 1024)
    z = _proj(xb2, p["w_in"], p["b_in"], layer=layer, col0=0, n_cols=o1, epilogue="glu", tn=D_MODEL, tm=tm_in)
    gates = _proj(xb2, p["w_in"], p["b_in"], layer=layer, col0=o3, n_cols=N_BRANCH * D_MODEL,
                  epilogue="sigmoid_bias", tn=D_MODEL, tm=tm_in)
    ur = _proj(xb2, p["w_rwkv"], p["b_in"], layer=layer, col0=0, n_cols=RWKV_U, epilogue="none",
               tn=RWKV_U, tm=tm_in)
    ua = _proj(xb2, p["w_att"], p["b_in"], layer=layer, col0=0, n_cols=ATT_QKV, epilogue="none",
               tn=ATT_QKV // 2, tm=tm_in)
    z = z.reshape(bsz, t, CONV_DIM)
    ur = ur.reshape(bsz, t, RWKV_U)
    ua = ua.reshape(bsz, t, ATT_QKV)

    hist = jnp.pad(conv_state, ((0, 0), (CONV_HIST - (CONV_K - 1), 0), (0, 0)))
    yc = _conv(z, hist, p["conv_dw"], p["conv_dw_b"], p["conv_ln_g"], p["conv_ln_b"], tt=256 if prompt else t)
    conv_new = jnp.concatenate([conv_state, z], axis=1)[:, -(CONV_K - 1):]

    shift8 = jnp.pad(shift[:, None, :], ((0, 0), (SUBLANES - 1, 0), (0, 0)))
    pre = _rwkv_pre(ur, shift8, p["rwkv_mu"], p["rwkv_w0"], p["rwkv_w2p"], p["rwkv_a0"], p["rwkv_a2p"],
                    p["rwkv_g2"], p["rwkv_kk"], p["rwkv_ka"], p["rwkv_rk"], p["ones_bd"],
                    tt=256 if prompt else t)
    chunk = 64 if prompt else SUBLANES
    if t % chunk:
        pre = [jnp.pad(a, ((0, 0), (0, chunk - t % chunk), (0, 0))) for a in pre]
    yr, wkv_new = _scan(*pre, wkv, p["rwkv_ln_g"], p["rwkv_ln_b"], chunk=chunk)
    yr = yr[:, :t]
    shift_new = ur[:, -1]

    os_, ls_, att_new = [], [], []
    for g, (window, dil) in enumerate(DILATIONS):
        if prompt:
            o, lse = _attn_prompt(ua, p["att_bias"][g], g=g, dil=dil)
            rows = min(window, t)
            kv = [ua[:, t - rows:, (w * N_DIL + g) * ATT_W:(w * N_DIL + g + 1) * ATT_W] for w in (1, 2)]
            att_new.append(jnp.stack([a.reshape(bsz, rows, ATT_HG, HEAD_DIM) for a in kv], axis=1))
        else:
            cache_t, prev = caches[g]
            o, lse, new = _attn_sample(ua, cache_t, prev, *p["att_bias"][g], g=g, layer=layer)
            att_new.append(new)
        os_.append(o.reshape(n, ATT_W))
        ls_.append(lse.reshape(n, ATT_W))

    moe_in = _merge(x.reshape(n, d), yc.reshape(n, CONV_DIM), yr.reshape(n, RWKV_C), os_, ls_, gates,
                    p["conv_proj"], p["rwkv_proj"], p["attn_proj"], p["w_out"], p["ln1_g"], p["ln1_b"],
                    p["router"], alpha=alpha, tm=min(n, 256), row0=row0, n_total=n_total, prev=moe_prev)
    return moe_in, conv_new, shift_new, wkv_new, att_new


def kernel(x_prompt, x_sample, state_conv, state_shift, state_wkv, cache_attn_w128, cache_attn_w512, cache_attn_w2048, rel_bias, w_in, b_gate, conv_dw, conv_dw_b, conv_ln_g, conv_ln_b, conv_proj, rwkv_mu, rwkv_w0, rwkv_w2, rwkv_a0, rwkv_a2, rwkv_g2, rwkv_kk, rwkv_ka, rwkv_rk, rwkv_ln_g, rwkv_ln_b, rwkv_proj, attn_proj, w_out, ln1_g, ln1_b, router_group, router_expert, moe_w_gate, moe_w_up, moe_w_down, ln2_g, ln2_b):
    depth = w_in.shape[0]
    alpha = (2 * depth) ** 0.25
    bp, tp, _ = x_prompt.shape
    ts = x_sample.shape[1]
    n_p, n_s = bp * tp, x_sample.shape[0] * ts
    caches_t = [jnp.transpose(c, (0, 1, 2, 4, 5, 3)) for c in (cache_attn_w128, cache_attn_w512, cache_attn_w2048)]
    new_caches = [None] * N_DIL
    head = np.arange(RWKV_C) // RWKV_N
    ones_bd = jnp.asarray(head[:, None] == head[None, :], BF16)
    tabs = [rel_bias[:, g * ATT_HG:(g + 1) * ATT_HG] for g in range(N_DIL)]
    bias_p = [_prompt_bias(tabs[g], dil) for g, (_, dil) in enumerate(DILATIONS)]
    bias_s = [_sample_bias(tabs[g], dil, caches_t[g].shape[-1], ts) for g, (_, dil) in enumerate(DILATIONS)]

    w_in_b = w_in.astype(BF16)
    o_rwkv = 2 * CONV_DIM
    o_att = o_rwkv + RWKV_U
    w_rwkv_b = w_in_b[:, :, o_rwkv:o_att]
    w_att_b = w_in_b[:, :, o_att:o_att + ATT_QKV]
    b_in =jnp.pad(b_gate, ((0, 0), (w_in.shape[2] - b_gate.shape[1], 0)))[:, None, :]
    moe_b = [w.astype(BF16) for w in (moe_w_gate, moe_w_up, moe_w_down)]

    xp, xs = x_prompt, x_sample
    xpb, xsb = xp.astype(BF16), xs.astype(BF16)
    outs_p, outs_s = [], []
    for l in range(depth):
        zeros_lora = jnp.zeros((LORA_W, RWKV_C), F32)
        router = jnp.concatenate(
            [router_group[l], router_expert[l],
             jnp.zeros((D_MODEL, ROUTER_PAD - N_GROUPS - N_EXPERTS), F32)], axis=1)
        p = dict(
            w_in=w_in_b, b_in=b_in, w_rwkv=w_rwkv_b, w_att=w_att_b,
            conv_dw=conv_dw[l], conv_dw_b=_row(conv_dw_b[l]), conv_ln_g=_row(conv_ln_g[l]),
            conv_ln_b=_row(conv_ln_b[l]), conv_proj=conv_proj[l].astype(BF16),
            rwkv_mu=_row(rwkv_mu[l]), rwkv_w0=_row(rwkv_w0[l]),
            rwkv_w2p=jnp.concatenate([rwkv_w2[l], zeros_lora], axis=0).astype(BF16),
            rwkv_a0=_row(rwkv_a0[l]),
            rwkv_a2p=jnp.concatenate([zeros_lora, rwkv_a2[l]], axis=0).astype(BF16),
            rwkv_g2=rwkv_g2[l].astype(BF16), rwkv_kk=_row(rwkv_kk[l]), rwkv_ka=_row(rwkv_ka[l]),
            rwkv_rk=_row(rwkv_rk[l]), rwkv_ln_g=_row(rwkv_ln_g[l]), rwkv_ln_b=_row(rwkv_ln_b[l]),
            rwkv_proj=rwkv_proj[l].astype(BF16), attn_proj=attn_proj[l].astype(BF16),
            w_out=w_out[l].astype(BF16), ln1_g=_row(ln1_g[l]), ln1_b=_row(ln1_b[l]), router=router,
            ln2_g=_row(ln2_g[l]), ln2_b=_row(ln2_b[l]),
            ones_bd=ones_bd)
        moe_in, *new_p = _mixers(
            xp, xpb, dict(p, att_bias=bias_p),
            jnp.zeros((bp, CONV_K - 1, CONV_DIM), F32), jnp.zeros((bp, RWKV_U), F32),
            jnp.zeros((bp, RWKV_H, RWKV_N, RWKV_N), F32), None, alpha=alpha, prompt=True, layer=l,
            row0=0, n_total=n_p + n_s, moe_prev=None)
        moe_in, *new_s = _mixers(
            xs, xsb, dict(p, att_bias=bias_s), state_conv[l], state_shift[l], state_wkv[l],
            list(zip(caches_t, new_caches)), alpha=alpha, prompt=False, layer=l,
            row0=n_p, n_total=n_p + n_s, moe_prev=moe_in)
        ht, gate, eid = moe_in
        ft = _experts(ht, *_route(eid[:, :TOP_K]), *moe_b, layer=l)
        xp, xpb = _ln2(ht, ft, gate, p["ln2_g"], p["ln2_b"], alpha=alpha, tm=512, row0=0, n=n_p)
        xs, xsb = _ln2(ht, ft, gate, p["ln2_g"], p["ln2_b"], alpha=alpha, tm=n_s, row0=n_p, n=n_s)
        xp, xpb = xp.reshape(x_prompt.shape), xpb.reshape(x_prompt.shape)
        xs, xsb = xs.reshape(x_sample.shape), xsb.reshape(x_sample.shape)
        new_caches = new_s[3]
        outs_p.append(new_p)
        outs_s.append(new_s)

    def stack(outs, i):
        return jnp.stack([o[i] for o in outs])

    att_p = [jnp.stack([o[3][g] for o in outs_p]) for g in range(N_DIL)]
    att_s = [jnp.transpose(c, (0, 1, 2, 5, 3, 4)) for c in new_caches]
    return (xp, xs, stack(outs_p, 0), stack(outs_s, 0), stack(outs_p, 1), stack(outs_s, 1),
            stack(outs_p, 2), stack(outs_s, 2),
            att_p[0], att_s[0], att_p[1], att_s[1], att_p[2], att_s[2])
```
